```python
import math
import jax, jax.numpy as jnp
from jax import lax
import numpy as np

D_MODEL = 1024
BATCH = 8
SEQ = 4096
DEPTH = 1

SSD_HEADS = 8
SSD_HEAD_DIM = 64
SSD_INNER = SSD_HEADS * SSD_HEAD_DIM
SSD_GROUPS = 2
SSD_STATE = 128
SSD_CONV = 4
SSD_CHUNK = 128
SSD_XBC = SSD_INNER + 2 * SSD_GROUPS * SSD_STATE
MLA_HEADS = 8
MLA_NOPE = 64
MLA_ROPE = 32
MLA_QK = MLA_NOPE + MLA_ROPE
MLA_V = 64
MLA_Q_RANK = 384
MLA_KV_RANK = 256
ROPE_THETA = 10000.0
ATTN_BLOCK = 128
MIX_WIDTH = SSD_INNER + MLA_HEADS * MLA_V
IN_WIDTH = SSD_INNER + SSD_XBC + SSD_HEADS + MLA_Q_RANK + MLA_KV_RANK + MLA_ROPE
IN_SPLITS = (SSD_INNER,
             SSD_INNER + SSD_XBC,
             SSD_INNER + SSD_XBC + SSD_HEADS,
             SSD_INNER + SSD_XBC + SSD_HEADS + MLA_Q_RANK,
             SSD_INNER + SSD_XBC + SSD_HEADS + MLA_Q_RANK + MLA_KV_RANK)
MEM_TOKENS = 256
MEM_HEADS = 4
MEM_HEAD_DIM = D_MODEL // MEM_HEADS
D_FF = 4 * D_MODEL
LN_EPS = 1e-5
RMS_EPS = 1e-6
DEEPNORM_ALPHA = (2.0 * DEPTH) ** 0.25
DEEPNORM_BETA = (8.0 * DEPTH) ** -0.25

kernel_name = "hybrid_ssd_mla_memxattn_deepnorm_layer"


def layer_norm(x, g, b):
    xf = x.astype(jnp.float32)
    mu = jnp.mean(xf, axis=-1, keepdims=True)
    var = jnp.mean(jnp.square(xf - mu), axis=-1, keepdims=True)
    return ((xf - mu) * lax.rsqrt(var + LN_EPS) * g.astype(jnp.float32) + b.astype(jnp.float32)).astype(x.dtype)


def rms_norm(x, g):
    xf = x.astype(jnp.float32)
    ms = jnp.mean(jnp.square(xf), axis=-1, keepdims=True)
    return (xf * lax.rsqrt(ms + RMS_EPS) * g.astype(jnp.float32)).astype(x.dtype)


def grouped_rms_norm(y, g, groups):
    b, s, c = y.shape
    yg = y.reshape(b, s, groups, c // groups)
    yg = yg * lax.rsqrt(jnp.mean(jnp.square(yg), axis=-1, keepdims=True) + RMS_EPS)
    return yg.reshape(b, s, c) * g.astype(jnp.float32)


def apply_rope(x, cos, sin):
    half = x.shape[-1] // 2
    xf = x.astype(jnp.float32)
    x1, x2 = xf[..., :half], xf[..., half:]
    return jnp.concatenate([x1 * cos - x2 * sin, x2 * cos + x1 * sin], axis=-1).astype(x.dtype)


def causal_depthwise_conv(u, w, b):
    c = u.shape[-1]
    y = lax.conv_general_dilated(u, w[:, None, :].astype(u.dtype), window_strides=(1,),
                                 padding=[(SSD_CONV - 1, 0)],
                                 dimension_numbers=("NWC", "WIO", "NWC"),
                                 feature_group_count=c)
    return y + b


def segsum(a):
    t = a.shape[-1]
    aa = jnp.broadcast_to(a[..., :, None], a.shape + (t,))
    aa = jnp.where(jnp.tril(jnp.ones((t, t), dtype=bool), -1), aa, 0.0)
    ss = jnp.cumsum(aa, axis=-2)
    return jnp.where(jnp.tril(jnp.ones((t, t), dtype=bool)), ss, -jnp.inf)


def ssd_chunked_scan(x, dt, a_head, bm, cm):
    b, s, h, p = x.shape
    g, n = bm.shape[-2:]
    e = h // g
    L = SSD_CHUNK
    c = s // L
    xf = (x.astype(jnp.float32) * dt[..., None]).reshape(b, c, L, g, e, p)
    a = jnp.moveaxis((dt * a_head).reshape(b, c, L, g, e), 2, -1)
    bc = bm.astype(jnp.float32).reshape(b, c, L, g, n)
    cc = cm.astype(jnp.float32).reshape(b, c, L, g, n)
    a_cs = jnp.cumsum(a, axis=-1)
    decay_ls = jnp.exp(segsum(a))
    cb = jnp.einsum("bclgn,bcsgn->bcgls", cc, bc)
    y_diag = jnp.einsum("bcgls,bcgels,bcsgep->bclgep", cb, decay_ls, xf)
    decay_to_end = jnp.exp(a_cs[..., -1:] - a_cs)
    states = jnp.einsum("bclgn,bcgel,bclgep->bcgepn", bc, decay_to_end, xf)
    chunk_decay = jnp.exp(a_cs[..., -1])

    def step(carry, inp):
        st, dec = inp
        return carry * dec[..., None, None] + st, carry

    init = jnp.zeros((b, g, e, p, n), jnp.float32)
    _, prev = lax.scan(step, init, (jnp.moveaxis(states, 1, 0), jnp.moveaxis(chunk_decay, 1, 0)))
    prev = jnp.moveaxis(prev, 0, 1)
    y_off = jnp.einsum("bclgn,bcgepn,bcgel->bclgep", cc, prev, jnp.exp(a_cs))
    return (y_diag + y_off).reshape(b, s, h, p)


def causal_block_attention(q, k, v, scale):
    b, s, h, d = q.shape
    nb = s // ATTN_BLOCK
    qb = jnp.moveaxis(q.reshape(b, nb, ATTN_BLOCK, h, d), 1, 0)
    k_pos = jnp.arange(s)

    def one_block(args):
        q_blk, i = args
        q_pos = i * ATTN_BLOCK + jnp.arange(ATTN_BLOCK)
        sc = jnp.einsum("bqhd,bkhd->bhqk", q_blk, k).astype(jnp.float32) * scale
        sc = jnp.where(k_pos[None, :] <= q_pos[:, None], sc, -jnp.inf)
        pr = jax.nn.softmax(sc, axis=-1).astype(v.dtype)
        return jnp.einsum("bhqk,bkhd->bqhd", pr, v)

    out = lax.map(one_block, (qb, jnp.arange(nb)))
    return jnp.moveaxis(out, 0, 1).reshape(b, s, h, v.shape[-1])


def hybrid_mixer(h, cos, sin, w_in, conv_w, conv_b, dt_bias, a_log, d_skip, ssd_norm_g,
                 q_norm_g, w_q_up, kv_norm_g, w_kv_up, w_out):
    b, s, _ = h.shape
    proj = h @ w_in
    z, xbc, dt_raw, q_lat, kv_lat, k_r = jnp.split(proj, IN_SPLITS, axis=-1)
    xbc = jax.nn.silu(causal_depthwise_conv(xbc, conv_w, conv_b))
    xs, bm, cm = jnp.split(xbc, [SSD_INNER, SSD_INNER + SSD_GROUPS * SSD_STATE], axis=-1)
    xs = xs.reshape(b, s, SSD_HEADS, SSD_HEAD_DIM)
    bm = bm.reshape(b, s, SSD_GROUPS, SSD_STATE)
    cm = cm.reshape(b, s, SSD_GROUPS, SSD_STATE)
    dt = jax.nn.softplus(dt_raw.astype(jnp.float32) + dt_bias.astype(jnp.float32))
    a_head = -jnp.exp(a_log.astype(jnp.float32))
    y = ssd_chunked_scan(xs, dt, a_head, bm, cm) + xs.astype(jnp.float32) * d_skip.astype(jnp.float32)[:, None]
    y = y.reshape(b, s, SSD_INNER) * jax.nn.silu(z.astype(jnp.float32))
    y = grouped_rms_norm(y, ssd_norm_g, SSD_GROUPS).astype(h.dtype)
    q = (rms_norm(q_lat, q_norm_g) @ w_q_up).reshape(b, s, MLA_HEADS, MLA_QK)
    q = jnp.concatenate([q[..., :MLA_NOPE], apply_rope(q[..., MLA_NOPE:], cos, sin)], axis=-1)
    kv = (rms_norm(kv_lat, kv_norm_g) @ w_kv_up).reshape(b, s, MLA_HEADS, MLA_NOPE + MLA_V)
    k_pe = apply_rope(k_r[:, :, None, :], cos, sin)
    k = jnp.concatenate([kv[..., :MLA_NOPE],
                         jnp.broadcast_to(k_pe, (b, s, MLA_HEADS, MLA_ROPE))], axis=-1)
    v = kv[..., MLA_NOPE:]
    o = causal_block_attention(q, k, v, MLA_QK ** -0.5).reshape(b, s, MLA_HEADS * MLA_V)
    return jnp.concatenate([y, o], axis=-1) @ w_out


def memory_cross_attention(h, mem, w_q, w_k, w_v, w_o):
    b, s, _ = h.shape
    m = mem.shape[1]
    q = (h @ w_q).reshape(b, s, MEM_HEADS, MEM_HEAD_DIM)
    k = (mem @ w_k).reshape(b, m, MEM_HEADS, MEM_HEAD_DIM)
    v = (mem @ w_v).reshape(b, m, MEM_HEADS, MEM_HEAD_DIM)
    sc = jnp.einsum("bshd,bmhd->bhsm", q, k).astype(jnp.float32) * (MEM_HEAD_DIM ** -0.5)
    pr = jax.nn.softmax(sc, axis=-1).astype(v.dtype)
    o = jnp.einsum("bhsm,bmhd->bshd", pr, v).reshape(b, s, D_MODEL)
    return o @ w_o


def sq_relu_mlp(h, w_up, w_down):
    return jnp.square(jax.nn.relu(h @ w_up)) @ w_down


def setup_inputs(seed: int = 0) -> dict:
    key = jax.random.key(seed)
    ks = jax.random.split(key, 32)
    f32 = jnp.float32

    def w(k, shape, fan_in, scale=1.0):
        return jax.random.normal(k, shape, f32) * (fan_in ** -0.5) * scale

    def gain(k, shape):
        return 1.0 + 0.02 * jax.random.normal(k, shape, f32)

    def bias(k, shape):
        return 0.02 * jax.random.normal(k, shape, f32)

    x = jax.random.normal(ks[0], (BATCH, SEQ, D_MODEL), f32)
    mem = jax.random.normal(ks[1], (BATCH, MEM_TOKENS, D_MODEL), f32)
    start = jax.random.randint(ks[2], (BATCH, 1), 0, 4096, dtype=jnp.int32)
    positions = (start + jnp.arange(SEQ, dtype=jnp.int32)[None, :]).astype(jnp.int32)

    dt0 = jnp.exp(jax.random.uniform(ks[3], (DEPTH, SSD_HEADS), f32,
                                     minval=math.log(1e-3), maxval=math.log(1e-1)))
    dt_bias = dt0 + jnp.log(-jnp.expm1(-dt0))
    a_log = jnp.log(jax.random.uniform(ks[4], (DEPTH, SSD_HEADS), f32, minval=1.0, maxval=16.0))
    v_col = (jnp.arange(MLA_NOPE + MLA_V) >= MLA_NOPE)
    kv_scale = jnp.tile(jnp.where(v_col, DEEPNORM_BETA, 1.0), MLA_HEADS).astype(f32)

    return {
        "x": x,
        "mem": mem,
        "positions": positions,
        "ln_in_g": gain(ks[5], (D_MODEL,)),
        "ln_in_b": bias(ks[6], (D_MODEL,)),
        "w_in": w(ks[7], (DEPTH, D_MODEL, IN_WIDTH), D_MODEL),
        "conv_w": w(ks[8], (DEPTH, SSD_CONV, SSD_XBC), SSD_CONV),
        "conv_b": bias(ks[9], (DEPTH, SSD_XBC)),
        "dt_bias": dt_bias,
        "a_log": a_log,
        "d_skip": gain(ks[10], (DEPTH, SSD_HEADS)),
        "ssd_norm_g": gain(ks[11], (DEPTH, SSD_INNER)),
        "q_norm_g": gain(ks[12], (DEPTH, MLA_Q_RANK)),
        "w_q_up": w(ks[13], (DEPTH, MLA_Q_RANK, MLA_HEADS * MLA_QK), MLA_Q_RANK),
        "kv_norm_g": gain(ks[14], (DEPTH, MLA_KV_RANK)),
        "w_kv_up": w(ks[15], (DEPTH, MLA_KV_RANK, MLA_HEADS * (MLA_NOPE + MLA_V)), MLA_KV_RANK) * kv_scale,
        "w_mix_out": w(ks[16], (DEPTH, MIX_WIDTH, D_MODEL), MIX_WIDTH, DEEPNORM_BETA),
        "ln1_g": gain(ks[17], (DEPTH, D_MODEL)),
        "ln1_b": bias(ks[18], (DEPTH, D_MODEL)),
        "w_mem_q": w(ks[19], (DEPTH, D_MODEL, D_MODEL), D_MODEL),
        "w_mem_k": w(ks[20], (DEPTH, D_MODEL, D_MODEL), D_MODEL),
        "w_mem_v": w(ks[21], (DEPTH, D_MODEL, D_MODEL), D_MODEL, DEEPNORM_BETA),
        "w_mem_o": w(ks[22], (DEPTH, D_MODEL, D_MODEL), D_MODEL, DEEPNORM_BETA),
        "ln2_g": gain(ks[23], (DEPTH, D_MODEL)),
        "ln2_b": bias(ks[24], (DEPTH, D_MODEL)),
        "w_up": w(ks[25], (DEPTH, D_MODEL, D_FF), D_MODEL, DEEPNORM_BETA),
        "w_down": w(ks[26], (DEPTH, D_FF, D_MODEL), D_FF, DEEPNORM_BETA),
        "ln3_g": gain(ks[27], (DEPTH, D_MODEL)),
        "ln3_b": bias(ks[28], (DEPTH, D_MODEL)),
    }


def reference(x, mem, positions, ln_in_g, ln_in_b, w_in, conv_w, conv_b, dt_bias, a_log, d_skip,
              ssd_norm_g, q_norm_g, w_q_up, kv_norm_g, w_kv_up, w_mix_out, ln1_g, ln1_b,
              w_mem_q, w_mem_k, w_mem_v, w_mem_o, ln2_g, ln2_b, w_up, w_down, ln3_g, ln3_b):
    half = MLA_ROPE // 2
    inv_freq = jnp.power(ROPE_THETA, -jnp.arange(half, dtype=jnp.float32) / half)
    ang = positions.astype(jnp.float32)[..., None] * inv_freq
    cos = jnp.cos(ang)[:, :, None, :]
    sin = jnp.sin(ang)[:, :, None, :]

    h = layer_norm(x, ln_in_g, ln_in_b)
    for l in range(DEPTH):
        mix = hybrid_mixer(h, cos, sin, w_in[l], conv_w[l], conv_b[l], dt_bias[l], a_log[l],
                           d_skip[l], ssd_norm_g[l], q_norm_g[l], w_q_up[l], kv_norm_g[l],
                           w_kv_up[l], w_mix_out[l])
        h = layer_norm(DEEPNORM_ALPHA * h + mix, ln1_g[l], ln1_b[l])
        xa = memory_cross_attention(h, mem, w_mem_q[l], w_mem_k[l], w_mem_v[l], w_mem_o[l])
        h = layer_norm(DEEPNORM_ALPHA * h + xa, ln2_g[l], ln2_b[l])
        ff = sq_relu_mlp(h, w_up[l], w_down[l])
        h = layer_norm(DEEPNORM_ALPHA * h + ff, ln3_g[l], ln3_b[l])
    return h
```

```python
import functools

import jax
import jax.numpy as jnp
from jax import lax
from jax.experimental import pallas as pl
from jax.experimental.pallas import tpu as pltpu

F32 = jnp.float32
BF16 = jnp.bfloat16

SSD_HEADS = 8
SSD_HEAD_DIM = 64
SSD_INNER = SSD_HEADS * SSD_HEAD_DIM
SSD_GROUPS = 2
SSD_STATE = 128
SSD_CONV = 4
SSD_CHUNK = 128
SSD_XBC = SSD_INNER + 2 * SSD_GROUPS * SSD_STATE
MLA_HEADS = 8
MLA_NOPE = 64
MLA_ROPE = 32
MLA_QK = MLA_NOPE + MLA_ROPE
MLA_V = 64
MLA_Q_RANK = 384
MLA_KV_RANK = 256
ROPE_THETA = 10000.0
MEM_HEADS = 4
LN_EPS = 1e-5
RMS_EPS = 1e-6
DEPTH = 1
DEEPNORM_ALPHA = (2.0 * DEPTH) ** 0.25

LANES = 128
SUBLANES = 8
VMEM_LIMIT_BYTES = 56 * 1024 * 1024

TOKEN_TILE = 512
SSD_ROWS = 256
ATTN_TILE = 512
FF_CHUNK = 1024

HEAD_PAD = LANES
HALF = LANES // 2


def _params(*semantics):
    return pltpu.CompilerParams(dimension_semantics=semantics, vmem_limit_bytes=VMEM_LIMIT_BYTES)


def _full(shape):
    zeros = (0,) * len(shape)
    return pl.BlockSpec(shape, lambda *_: zeros, pipeline_mode=pl.Buffered(1))


def _layer_norm(x, g, b):
    mu = jnp.mean(x, axis=-1, keepdims=True)
    xc = x - mu
    var = jnp.mean(xc * xc, axis=-1, keepdims=True)
    return xc * lax.rsqrt(var + LN_EPS) * g + b


def _rms_norm(x, g):
    ms = jnp.mean(x * x, axis=-1, keepdims=True)
    return x * lax.rsqrt(ms + RMS_EPS) * g


def _silu(x):
    return x * (1.0 / (1.0 + jnp.exp(-x)))


def _dot(a, b):
    return jnp.dot(a, b, preferred_element_type=F32)


def _dot_nt(a, b):
    return lax.dot_general(a, b, (((1,), (1,)), ((), ())), preferred_element_type=F32)


def _rope_table_kernel(pos_ref, invf_ref, cos_ref, sin_ref):
    ang = pos_ref[...].astype(F32) * invf_ref[...]
    cos_ref[...] = jnp.cos(ang)
    sin_ref[...] = jnp.sin(ang)


def _rope_tables(positions, inv_freq):
    half = inv_freq.shape[0]
    per_row = LANES // half
    tokens = positions.size
    rows = tokens // per_row
    pos = jnp.repeat(positions.reshape(rows, per_row), half, axis=1)
    invf = jnp.tile(inv_freq, per_row)[None, :]
    blk = min(rows, 512)
    cos, sin = pl.pallas_call(
        _rope_table_kernel,
        out_shape=(jax.ShapeDtypeStruct((rows, LANES), F32),) * 2,
        grid=(rows // blk,),
        in_specs=[pl.BlockSpec((blk, LANES), lambda i: (i, 0)), _full((1, LANES))],
        out_specs=(pl.BlockSpec((blk, LANES), lambda i: (i, 0)),) * 2,
        compiler_params=_params("parallel"),
        name="rope_table",
    )(pos, invf)
    return cos.reshape(tokens, half), sin.reshape(tokens, half)


_C_Z = 0
_C_XBC = _C_Z + SSD_INNER
_C_QLAT = _C_XBC + SSD_XBC
_C_KVLAT = _C_QLAT + MLA_Q_RANK
_C_KR = _C_KVLAT + MLA_KV_RANK
_C_DT = _C_KR + LANES
_C_END = _C_DT + LANES


def _in_proj_kernel(x_ref, g_ref, b_ref, w1_ref, qg_ref, wq_ref, kvg_ref, wkv_ref, csq_ref, csk_ref,
                    z_ref, xbc_ref, dt_ref, q_ref, k_ref, v_ref):
    h = _layer_norm(x_ref[...], g_ref[...], b_ref[...]).astype(BF16)
    z_ref[...] = _dot(h, w1_ref[:, _C_Z:_C_XBC]).astype(BF16)
    xbc_ref[...] = _dot(h, w1_ref[:, _C_XBC:_C_QLAT]).astype(BF16)
    dt_ref[...] = _dot(h, w1_ref[:, _C_DT:_C_END])

    q_lat = _dot(h, w1_ref[:, _C_QLAT:_C_KVLAT])
    qn = _rms_norm(q_lat, qg_ref[...]).astype(BF16)
    csq = csq_ref[...]
    for hd in range(MLA_HEADS):
        cols = slice(hd * HEAD_PAD, (hd + 1) * HEAD_PAD)
        q_ref[:, cols] = (_dot(qn, wq_ref[:, cols]) * csq).astype(BF16)

    kr = _dot(h, w1_ref[:, _C_KR:_C_DT]) * csk_ref[...]
    kpe = kr + pltpu.roll(kr, MLA_ROPE, 1)
    lane = lax.broadcasted_iota(jnp.int32, kpe.shape, 1)
    kpe = jnp.where(lane >= MLA_NOPE, kpe, 0.0)

    kv_lat = _dot(h, w1_ref[:, _C_KVLAT:_C_KR])
    kvn = _rms_norm(kv_lat, kvg_ref[...]).astype(BF16)
    kw = MLA_HEADS * HEAD_PAD
    for hd in range(MLA_HEADS):
        cols = slice(hd * HEAD_PAD, (hd + 1) * HEAD_PAD)
        k_ref[:, cols] = (_dot(kvn, wkv_ref[:, cols]) + kpe).astype(BF16)
    v_ref[...] = _dot(kvn, wkv_ref[:, kw:]).astype(BF16)


def _in_proj(x2, ln_g, ln_b, w1, q_norm_g, wq, kv_norm_g, wkv, csq, csk):
    tokens, d = x2.shape
    tm = TOKEN_TILE
    hw = MLA_HEADS * HEAD_PAD

    def row(w):
        return pl.BlockSpec((tm, w), lambda i: (i, 0))

    out_shape = (
        jax.ShapeDtypeStruct((tokens, SSD_INNER), BF16),
        jax.ShapeDtypeStruct((tokens, SSD_XBC), BF16),
        jax.ShapeDtypeStruct((tokens, LANES), F32),
        jax.ShapeDtypeStruct((tokens, hw), BF16),
        jax.ShapeDtypeStruct((tokens, hw), BF16),
        jax.ShapeDtypeStruct((tokens, hw), BF16),
    )
    return pl.pallas_call(
        _in_proj_kernel,
        out_shape=out_shape,
        grid=(tokens // tm,),
        in_specs=[row(d), _full((1, d)), _full((1, d)), _full(w1.shape), _full((1, MLA_Q_RANK)),
                  _full(wq.shape), _full((1, MLA_KV_RANK)), _full(wkv.shape), row(LANES), row(LANES)],
        out_specs=(row(SSD_INNER), row(SSD_XBC), row(LANES), row(hw), row(hw), row(hw)),
        compiler_params=_params("parallel"),
        name="in_proj",
    )(x2, ln_g, ln_b, w1, q_norm_g, wq, kv_norm_g, wkv, csq, csk)


def _cumsum_rows(x):
    n = x.shape[0]
    row = lax.broadcasted_iota(jnp.int32, x.shape, 0)
    shift = 1
    while shift < n:
        x = x + jnp.where(row >= shift, pltpu.roll(x, shift, 0), 0.0)
        shift *= 2
    return x


def _ssd_kernel(xbc_ref, dt_ref, z_ref, cw_ref, cb_ref, dtb_ref, alog_ref, dskip_ref, ng_ref,
                y_ref, ext_ref, state_ref):
    rows = xbc_ref.shape[0]
    L = SSD_CHUNK
    pairs = SSD_HEADS // 2
    pairs_per_group = pairs // SSD_GROUPS

    @pl.when(pl.program_id(1) == 0)
    def _():
        ext_ref[0:SUBLANES, :] = jnp.zeros((SUBLANES, SSD_XBC), F32)
        state_ref[...] = jnp.zeros(state_ref.shape, F32)

    ext_ref[SUBLANES:SUBLANES + rows, :] = xbc_ref[...].astype(F32)
    conv = cb_ref[...]
    for k in range(SSD_CONV):
        off = SUBLANES - (SSD_CONV - 1) + k
        conv = conv + cw_ref[k:k + 1, :] * ext_ref[off:off + rows, :]
    ext_ref[0:SUBLANES, :] = ext_ref[rows:rows + SUBLANES, :]
    xc = _silu(conv)

    a_head = -jnp.exp(alog_ref[...])
    head_lane = lax.broadcasted_iota(jnp.int32, (1, LANES), 1) < SSD_HEADS
    a_head = jnp.where(head_lane, a_head, 0.0)

    sq = (L, L)
    row_i = lax.broadcasted_iota(jnp.int32, sq, 0)
    col_i = lax.broadcasted_iota(jnp.int32, sq, 1)
    causal = row_i >= col_i
    left = col_i < HALF
    left_row = lax.broadcasted_iota(jnp.int32, (1, LANES), 1) < HALF

    def cols2(mat, h0):
        return jnp.where(left, mat[:, h0:h0 + 1], mat[:, h0 + 1:h0 + 2])

    for c in range(rows // L):
        r0 = c * L
        xs = xc[r0:r0 + L, 0:SSD_INNER]
        bm = xc[r0:r0 + L, SSD_INNER:SSD_INNER + SSD_GROUPS * SSD_STATE]
        cm = xc[r0:r0 + L, SSD_INNER + SSD_GROUPS * SSD_STATE:]
        dt_in = dt_ref[r0:r0 + L, :] + dtb_ref[...]
        dt = jnp.maximum(dt_in, 0.0) + jnp.log1p(jnp.exp(-jnp.abs(dt_in)))
        a = dt * a_head
        a_cs = _cumsum_rows(a)
        a_cs_t = a_cs.T
        a_last = a_cs[L - 1:L, :]
        e_cs = jnp.exp(a_cs)
        d_end = jnp.exp(a_last - a_cs)
        c_dec = jnp.exp(a_last)

        y_pairs = []
        for g in range(SSD_GROUPS):
            b_g = bm[:, g * SSD_STATE:(g + 1) * SSD_STATE]
            c_g = cm[:, g * SSD_STATE:(g + 1) * SSD_STATE]
            b_gt = b_g.T.astype(BF16)
            cb = _dot(c_g.astype(BF16), b_gt)
            for j in range(pairs_per_group):
                pj = g * pairs_per_group + j
                h0 = 2 * pj
                x_p = xs[:, pj * LANES:(pj + 1) * LANES]
                xdt = x_p * cols2(dt, h0)
                st = state_ref[pj]
                y_p = x_p * dskip_ref[:, pj * LANES:(pj + 1) * LANES]
                for hh in range(2):
                    hd = h0 + hh
                    seg = a_cs[:, hd:hd + 1] - a_cs_t[hd:hd + 1, :]
                    m = cb * jnp.where(causal, jnp.exp(seg), 0.0)
                    c_s = c_g * e_cs[:, hd:hd + 1]
                    lhs = jnp.concatenate([m, c_s], axis=1).astype(BF16)
                    keep = left if hh == 0 else jnp.logical_not(left)
                    rhs = jnp.concatenate([jnp.where(keep, xdt, 0.0), jnp.where(keep, st, 0.0)],
                                          axis=0).astype(BF16)
                    y_p = y_p + _dot(lhs, rhs)
                new = _dot(b_gt, (xdt * cols2(d_end, h0)).astype(BF16))
                dec = jnp.where(left_row, c_dec[:, h0:h0 + 1], c_dec[:, h0 + 1:h0 + 2])
                state_ref[pj] = st * dec + new
                y_pairs.append(y_p)

        gw = SSD_INNER // SSD_GROUPS
        zt = z_ref[r0:r0 + L, :].astype(F32)
        for g in range(SSD_GROUPS):
            y_g = jnp.concatenate(y_pairs[g * pairs_per_group:(g + 1) * pairs_per_group], axis=1)
            y_g = y_g * _silu(zt[:, g * gw:(g + 1) * gw])
            y_g = _rms_norm(y_g, ng_ref[:, g * gw:(g + 1) * gw])
            y_ref[r0:r0 + L, g * gw:(g + 1) * gw] = y_g.astype(BF16)


def _ssd(xbc, dt, z, conv_w, conv_b, dt_bias, a_log, d_skip, norm_g, batch, seq):
    rows = SSD_ROWS
    steps = seq // rows

    def tok(w):
        return pl.BlockSpec((rows, w), lambda b, c: (b * steps + c, 0))

    pairs = SSD_HEADS // 2
    return pl.pallas_call(
        _ssd_kernel,
        out_shape=jax.ShapeDtypeStruct((batch * seq, SSD_INNER), BF16),
        grid=(batch, steps),
        in_specs=[tok(SSD_XBC), tok(LANES), tok(SSD_INNER), _full(conv_w.shape), _full(conv_b.shape),
                  _full(dt_bias.shape), _full(a_log.shape), _full(d_skip.shape), _full(norm_g.shape)],
        out_specs=tok(SSD_INNER),
        scratch_shapes=[pltpu.VMEM((rows + SUBLANES, SSD_XBC), F32),
                        pltpu.VMEM((pairs, SSD_STATE, LANES), F32)],
        compiler_params=_params("parallel", "arbitrary"),
        name="ssd",
    )(xbc, dt, z, conv_w, conv_b, dt_bias, a_log, d_skip, norm_g)


def _mla_attn_kernel(q_ref, k_ref, v_ref, o_ref):
    seq = q_ref.shape[0]
    t = ATTN_TILE
    row_i = lax.broadcasted_iota(jnp.int32, (t, t), 0)
    col_i = lax.broadcasted_iota(jnp.int32, (t, t), 1)
    causal = row_i >= col_i

    def q_tile(qi, _):
        q0 = pl.multiple_of(qi * t, t)
        out = jnp.zeros((t, LANES), F32)
        for hh in range(2):
            cols = slice(hh * HEAD_PAD, (hh + 1) * HEAD_PAD)
            q = q_ref[pl.ds(q0, t), cols]

            def update(carry, k0, mask):
                m, l, acc = carry
                s = _dot_nt(q, k_ref[pl.ds(k0, t), cols])
                if mask:
                    s = jnp.where(causal, s, -jnp.inf)
                m_new = jnp.maximum(m, jnp.max(s, axis=-1, keepdims=True))
                p = jnp.exp(s - m_new)
                alpha = jnp.exp(m - m_new)
                l = alpha * l + jnp.sum(p, axis=-1, keepdims=True)
                acc = alpha * acc + _dot(p.astype(BF16), v_ref[pl.ds(k0, t), cols])
                return m_new, l, acc

            init = (jnp.full((t, 1), -jnp.inf, F32), jnp.zeros((t, 1), F32), jnp.zeros((t, LANES), F32))
            carry = lax.fori_loop(
                0, qi, lambda kb, cr: update(cr, pl.multiple_of(kb * t, t), False), init)
            _, l, acc = update(carry, q0, True)
            out = out + acc * (1.0 / l)
        o_ref[pl.ds(q0, t), :] = out.astype(BF16)
        return 0

    lax.fori_loop(0, seq // t, q_tile, 0)


def _mla_attn(q, k, v, batch, seq):
    pairs = MLA_HEADS // 2
    pw = 2 * HEAD_PAD
    spec = pl.BlockSpec((seq, pw), lambda b, p: (b, p))
    return pl.pallas_call(
        _mla_attn_kernel,
        out_shape=jax.ShapeDtypeStruct((batch * seq, MLA_HEADS * MLA_V), BF16),
        grid=(batch, pairs),
        in_specs=[spec, spec, spec],
        out_specs=pl.BlockSpec((seq, LANES), lambda b, p: (b, p)),
        compiler_params=_params("parallel", "parallel"),
        name="mla_attn",
    )(q, k, v)


def _mem_kv_kernel(mem_ref, wk_ref, wv_ref, k_ref, v_ref):
    m = mem_ref[...].astype(BF16)
    k_ref[...] = _dot(m, wk_ref[...]).astype(BF16)
    v_ref[...] = _dot(m, wv_ref[...]).astype(BF16)


def _mem_kv(mem2, wk, wv, batch, mem_tokens):
    d = mem2.shape[1]
    spec = pl.BlockSpec((mem_tokens, d), lambda b: (b, 0))
    return pl.pallas_call(
        _mem_kv_kernel,
        out_shape=(jax.ShapeDtypeStruct(mem2.shape, BF16),) * 2,
        grid=(batch,),
        in_specs=[spec, _full(wk.shape), _full(wv.shape)],
        out_specs=(spec, spec),
        compiler_params=_params("parallel"),
        name="mem_kv",
    )(mem2, wk, wv)


def _mix_xattn_kernel(x_ref, y_ref, o_ref, km_ref, vm_ref, g0_ref, b0_ref, wout_ref, g1_ref, b1_ref,
                      wq_ref, wo_ref, g2_ref, b2_ref, h2_ref):
    h = _layer_norm(x_ref[...], g0_ref[...], b0_ref[...])
    mix = _dot(y_ref[...], wout_ref[0:SSD_INNER, :]) + _dot(o_ref[...], wout_ref[SSD_INNER:, :])
    h1 = _layer_norm(DEEPNORM_ALPHA * h + mix, g1_ref[...], b1_ref[...])

    h1b = h1.astype(BF16)
    d = h1.shape[1]
    hd_w = d // MEM_HEADS
    scale = hd_w ** -0.5
    xa = jnp.zeros_like(h1)
    for hd in range(MEM_HEADS):
        cols = slice(hd * hd_w, (hd + 1) * hd_w)
        q = (_dot(h1b, wq_ref[:, cols]) * scale).astype(BF16)
        s = _dot_nt(q, km_ref[:, cols])
        p = jnp.exp(s - jnp.max(s, axis=-1, keepdims=True))
        p = p * (1.0 / jnp.sum(p, axis=-1, keepdims=True))
        oh = _dot(p.astype(BF16), vm_ref[:, cols])
        xa = xa + _dot(oh.astype(BF16), wo_ref[cols, :])
    h2_ref[...] = _layer_norm(DEEPNORM_ALPHA * h1 + xa, g2_ref[...], b2_ref[...])


def _mix_xattn(x2, y, o, k_mem, v_mem, ln_in_g, ln_in_b, w_out, ln1_g, ln1_b, wq, wo, ln2_g, ln2_b,
               seq, mem_tokens):
    tokens, d = x2.shape
    tm = TOKEN_TILE
    per_batch = seq // tm

    def row(w):
        return pl.BlockSpec((tm, w), lambda i: (i, 0))

    mem_spec = pl.BlockSpec((mem_tokens, d), lambda i: (i // per_batch, 0))
    vec = _full((1, d))
    return pl.pallas_call(
        _mix_xattn_kernel,
        out_shape=jax.ShapeDtypeStruct((tokens, d), F32),
        grid=(tokens // tm,),
        in_specs=[row(d), row(y.shape[1]), row(o.shape[1]), mem_spec, mem_spec, vec, vec,
                  _full(w_out.shape), vec, vec, _full(wq.shape), _full(wo.shape), vec, vec],
        out_specs=row(d),
        compiler_params=_params("parallel"),
        name="mix_xattn",
    )(x2, y, o, k_mem, v_mem, ln_in_g, ln_in_b, w_out, ln1_g, ln1_b, wq, wo, ln2_g, ln2_b)


def _mlp_kernel(h_ref, wup_ref, wdn_ref, g_ref, b_ref, out_ref):
    h = h_ref[...]
    hb = h.astype(BF16)
    ff = jnp.zeros_like(h)
    for c in range(wup_ref.shape[1] // FF_CHUNK):
        cols = slice(c * FF_CHUNK, (c + 1) * FF_CHUNK)
        u = jnp.maximum(_dot(hb, wup_ref[:, cols]), 0.0)
        ff = ff + _dot((u * u).astype(BF16), wdn_ref[cols, :])
    out_ref[...] = _layer_norm(DEEPNORM_ALPHA * h + ff, g_ref[...], b_ref[...])


def _mlp(h2, w_up, w_down, ln_g, ln_b):
    tokens, d = h2.shape
    tm = TOKEN_TILE
    row = pl.BlockSpec((tm, d), lambda i: (i, 0))
    return pl.pallas_call(
        _mlp_kernel,
        out_shape=jax.ShapeDtypeStruct((tokens, d), F32),
        grid=(tokens // tm,),
        in_specs=[row, _full(w_up.shape), _full(w_down.shape), _full((1, d)), _full((1, d))],
        out_specs=row,
        compiler_params=_params("parallel"),
        name="mlp",
    )(h2, w_up, w_down, ln_g, ln_b)


def _rot_cols(w):
    half = MLA_ROPE // 2
    return jnp.concatenate([-w[..., half:], w[..., :half]], axis=-1)


def _pack_in_proj(w_in):
    o = 0
    segs = {}
    for name, width in (("z", SSD_INNER), ("xbc", SSD_XBC), ("dt", SSD_HEADS), ("q", MLA_Q_RANK),
                        ("kv", MLA_KV_RANK), ("kr", MLA_ROPE)):
        segs[name] = w_in[:, o:o + width]
        o += width
    kr, krs = segs["kr"], _rot_cols(segs["kr"])
    dt = jnp.pad(segs["dt"], ((0, 0), (0, LANES - SSD_HEADS)))
    return jnp.concatenate([segs["z"], segs["xbc"], segs["q"], segs["kv"], kr, krs, kr, krs, dt],
                           axis=1).astype(BF16)


def _pack_q_up(w_q_up):
    r = w_q_up.shape[0]
    w = w_q_up.reshape(r, MLA_HEADS, MLA_QK)
    rope = w[:, :, MLA_NOPE:]
    return jnp.concatenate([w[:, :, :MLA_NOPE], rope, _rot_cols(rope)], axis=-1).reshape(
        r, MLA_HEADS * HEAD_PAD).astype(BF16)


def _pack_kv_up(w_kv_up):
    r = w_kv_up.shape[0]
    w = w_kv_up.reshape(r, MLA_HEADS, MLA_NOPE + MLA_V)
    zeros = jnp.zeros((r, MLA_HEADS, HEAD_PAD - MLA_NOPE), w.dtype)
    wk = jnp.concatenate([w[:, :, :MLA_NOPE], zeros], axis=-1)
    wv = w[:, :, MLA_NOPE:]
    zv = jnp.zeros_like(wv)
    even = (jnp.arange(MLA_HEADS) % 2 == 0)[None, :, None]
    wv = jnp.concatenate([jnp.where(even, wv, zv), jnp.where(even, zv, wv)], axis=-1)
    return jnp.concatenate([wk.reshape(r, -1), wv.reshape(r, -1)], axis=1).astype(BF16)


def kernel(x, mem, positions, ln_in_g, ln_in_b, w_in, conv_w, conv_b, dt_bias, a_log, d_skip, ssd_norm_g, q_norm_g, w_q_up, kv_norm_g, w_kv_up, w_mix_out, ln1_g, ln1_b, w_mem_q, w_mem_k, w_mem_v, w_mem_o, ln2_g, ln2_b, w_up, w_down, ln3_g, ln3_b):
    batch, seq, d = x.shape
    mem_tokens = mem.shape[1]
    tokens = batch * seq
    assert w_in.shape[0] == DEPTH == 1
    assert seq % ATTN_TILE == 0 and seq % TOKEN_TILE == 0 and seq % SSD_ROWS == 0

    def vec(v):
        return v.reshape(1, -1).astype(F32)

    def pad_lanes(v):
        return jnp.pad(vec(v), ((0, 0), (0, LANES - v.size)))

    x2 = x.reshape(tokens, d)

    half = MLA_ROPE // 2
    inv_freq = jnp.power(ROPE_THETA, -jnp.arange(half, dtype=F32) / half)
    cos, sin = _rope_tables(positions, inv_freq)
    scale = MLA_QK ** -0.5
    csk = jnp.concatenate([cos, cos, sin, sin, cos, cos, sin, sin], axis=1)
    csq = scale * jnp.concatenate([jnp.ones((tokens, MLA_NOPE), F32), cos, cos, sin, sin], axis=1)

    z, xbc, dt, q, k, v = _in_proj(
        x2, vec(ln_in_g), vec(ln_in_b), _pack_in_proj(w_in[0]), vec(q_norm_g[0]), _pack_q_up(w_q_up[0]),
        vec(kv_norm_g[0]), _pack_kv_up(w_kv_up[0]), csq, csk)

    y = _ssd(xbc, dt, z, conv_w[0].astype(F32), vec(conv_b[0]), pad_lanes(dt_bias[0]), pad_lanes(a_log[0]),
             vec(jnp.repeat(d_skip[0], SSD_HEAD_DIM)), vec(ssd_norm_g[0]), batch, seq)
    o = _mla_attn(q, k, v, batch, seq)

    k_mem, v_mem = _mem_kv(mem.reshape(batch * mem_tokens, d), w_mem_k[0].astype(BF16),
                           w_mem_v[0].astype(BF16), batch, mem_tokens)
    h2 = _mix_xattn(x2, y, o, k_mem, v_mem, vec(ln_in_g), vec(ln_in_b), w_mix_out[0].astype(BF16),
                    vec(ln1_g[0]), vec(ln1_b[0]), w_mem_q[0].astype(BF16), w_mem_o[0].astype(BF16),
                    vec(ln2_g[0]), vec(ln2_b[0]), seq, mem_tokens)
    out = _mlp(h2, w_up[0].astype(BF16), w_down[0].astype(BF16), vec(ln3_g[0]), vec(ln3_b[0]))
    return out.reshape(batch, seq, d)
```

```python
import functools

import jax
import jax.numpy as jnp
from jax import lax
from jax.experimental import pallas as pl
from jax.experimental.pallas import tpu as pltpu

F32 = jnp.float32
BF16 = jnp.bfloat16

SSD_HEADS = 8
SSD_HEAD_DIM = 64
SSD_INNER = SSD_HEADS * SSD_HEAD_DIM
SSD_GROUPS = 2
SSD_STATE = 128
SSD_CONV = 4
SSD_CHUNK = 128
SSD_XBC = SSD_INNER + 2 * SSD_GROUPS * SSD_STATE
MLA_HEADS = 8
MLA_NOPE = 64
MLA_ROPE = 32
MLA_QK = MLA_NOPE + MLA_ROPE
MLA_V = 64
MLA_Q_RANK = 384
MLA_KV_RANK = 256
ROPE_THETA = 10000.0
MEM_HEADS = 4
LN_EPS = 1e-5
RMS_EPS = 1e-6
DEPTH = 1
DEEPNORM_ALPHA = (2.0 * DEPTH) ** 0.25

LANES = 128
SUBLANES = 8
VMEM_LIMIT_BYTES = 56 * 1024 * 1024

TOKEN_TILE = 512
SSD_ROWS = 256
ATTN_TILE = 512
ATTN_Q_SUB = 2
FF_CHUNK = 1024

HEAD_PAD = LANES
HALF = LANES // 2


def _params(*semantics):
    return pltpu.CompilerParams(dimension_semantics=semantics, vmem_limit_bytes=VMEM_LIMIT_BYTES)


def _full(shape):
    zeros = (0,) * len(shape)
    return pl.BlockSpec(shape, lambda *_: zeros, pipeline_mode=pl.Buffered(1))


def _layer_norm(x, g, b):
    mu = jnp.mean(x, axis=-1, keepdims=True)
    xc = x - mu
    var = jnp.mean(xc * xc, axis=-1, keepdims=True)
    return xc * lax.rsqrt(var + LN_EPS) * g + b


def _rms_norm(x, g):
    ms = jnp.mean(x * x, axis=-1, keepdims=True)
    return x * lax.rsqrt(ms + RMS_EPS) * g


def _silu(x):
    return x * (1.0 / (1.0 + jnp.exp(-x)))


def _dot(a, b):
    return jnp.dot(a, b, preferred_element_type=F32)


def _dot_nt(a, b):
    return lax.dot_general(a, b, (((1,), (1,)), ((), ())), preferred_element_type=F32)


_C_Z = 0
_C_XBC = _C_Z + SSD_INNER
_C_QLAT = _C_XBC + SSD_XBC
_C_KVLAT = _C_QLAT + MLA_Q_RANK
_C_KR = _C_KVLAT + MLA_KV_RANK
_C_DT = _C_KR + LANES
_C_END = _C_DT + LANES

LOG2_E = 1.4426950408889634
MLA_Q_SCALE = MLA_QK ** -0.5 * LOG2_E


def _in_proj_kernel(x_ref, pos_ref, invf_ref, vone_ref, g_ref, b_ref, w1_ref, qg_ref, wq_ref, kvg_ref, wkv_ref,
                    z_ref, xbc_ref, dt_ref, q_ref, k_ref, v_ref):
    h = _layer_norm(x_ref[...], g_ref[...], b_ref[...]).astype(BF16)
    z_ref[...] = _dot(h, w1_ref[:, _C_Z:_C_XBC]).astype(BF16)
    xbc_ref[...] = _dot(h, w1_ref[:, _C_XBC:_C_QLAT]).astype(BF16)
    dt_ref[...] = _dot(h, w1_ref[:, _C_DT:_C_END])

    ang = pos_ref[...].astype(F32) * invf_ref[...]
    cos = jnp.cos(ang)
    sin = jnp.sin(ang)
    lane = lax.broadcasted_iota(jnp.int32, ang.shape, 1)

    csq = MLA_Q_SCALE * jnp.where(lane < MLA_NOPE, 1.0, jnp.where(lane < MLA_NOPE + MLA_ROPE, cos, sin))
    q_lat = _dot(h, w1_ref[:, _C_QLAT:_C_KVLAT])
    qn = _rms_norm(q_lat, qg_ref[...]).astype(BF16)
    for hd in range(MLA_HEADS):
        cols = slice(hd * HEAD_PAD, (hd + 1) * HEAD_PAD)
        q_ref[:, cols] = (_dot(qn, wq_ref[:, cols]) * csq).astype(BF16)

    csk = jnp.where((lane // MLA_ROPE) % 2 == 0, cos, sin)
    kr = _dot(h, w1_ref[:, _C_KR:_C_DT]) * csk
    kpe = kr + pltpu.roll(kr, MLA_ROPE, 1)
    kpe = jnp.where(lane >= MLA_NOPE, kpe, 0.0)

    kv_lat = _dot(h, w1_ref[:, _C_KVLAT:_C_KR])
    kvn = _rms_norm(kv_lat, kvg_ref[...]).astype(BF16)
    kw = MLA_HEADS * HEAD_PAD
    for hd in range(MLA_HEADS):
        cols = slice(hd * HEAD_PAD, (hd + 1) * HEAD_PAD)
        k_ref[:, cols] = (_dot(kvn, wkv_ref[:, cols]) + kpe).astype(BF16)
    v_ref[...] = (_dot(kvn, wkv_ref[:, kw:]) + vone_ref[...]).astype(BF16)


def _in_proj(x2, pos, invf, vone, ln_g, ln_b, w1, q_norm_g, wq, kv_norm_g, wkv):
    tokens, d = x2.shape
    tm = TOKEN_TILE
    hw = MLA_HEADS * HEAD_PAD

    def row(w):
        return pl.BlockSpec((tm, w), lambda i: (i, 0))

    out_shape = (
        jax.ShapeDtypeStruct((tokens, SSD_INNER), BF16),
        jax.ShapeDtypeStruct((tokens, SSD_XBC), BF16),
        jax.ShapeDtypeStruct((tokens, LANES), F32),
        jax.ShapeDtypeStruct((tokens, hw), BF16),
        jax.ShapeDtypeStruct((tokens, hw), BF16),
        jax.ShapeDtypeStruct((tokens, hw), BF16),
    )
    return pl.pallas_call(
        _in_proj_kernel,
        out_shape=out_shape,
        grid=(tokens // tm,),
        in_specs=[row(d), row(1), _full((1, LANES)), _full((1, hw)), _full((1, d)), _full((1, d)),
                  _full(w1.shape), _full((1, MLA_Q_RANK)), _full(wq.shape), _full((1, MLA_KV_RANK)),
                  _full(wkv.shape)],
        out_specs=(row(SSD_INNER), row(SSD_XBC), row(LANES), row(hw), row(hw), row(hw)),
        compiler_params=_params("parallel"),
        name="in_proj",
    )(x2, pos, invf, vone, ln_g, ln_b, w1, q_norm_g, wq, kv_norm_g, wkv)


def _cumsum_rows(x):
    n = x.shape[0]
    row = lax.broadcasted_iota(jnp.int32, x.shape, 0)
    shift = 1
    while shift < n:
        x = x + jnp.where(row >= shift, pltpu.roll(x, shift, 0), 0.0)
        shift *= 2
    return x


def _ssd_kernel(xbc_ref, dt_ref, z_ref, cw_ref, cb_ref, dtb_ref, alog_ref, dskip_ref, ng_ref,
                y_ref, ext_ref, state_ref):
    rows = xbc_ref.shape[0]
    L = SSD_CHUNK
    pairs = SSD_HEADS // 2
    pairs_per_group = pairs // SSD_GROUPS

    @pl.when(pl.program_id(1) == 0)
    def _():
        ext_ref[0:SUBLANES, :] = jnp.zeros((SUBLANES, SSD_XBC), F32)
        state_ref[...] = jnp.zeros(state_ref.shape, F32)

    ext_ref[SUBLANES:SUBLANES + rows, :] = xbc_ref[...].astype(F32)
    conv = cb_ref[...]
    for k in range(SSD_CONV):
        off = SUBLANES - (SSD_CONV - 1) + k
        conv = conv + cw_ref[k:k + 1, :] * ext_ref[off:off + rows, :]
    ext_ref[0:SUBLANES, :] = ext_ref[rows:rows + SUBLANES, :]
    xc = _silu(conv)

    a_head = -jnp.exp(alog_ref[...])
    head_lane = lax.broadcasted_iota(jnp.int32, (1, LANES), 1) < SSD_HEADS
    a_head = jnp.where(head_lane, a_head, 0.0)

    sq = (L, L)
    row_i = lax.broadcasted_iota(jnp.int32, sq, 0)
    col_i = lax.broadcasted_iota(jnp.int32, sq, 1)
    causal = row_i >= col_i
    left = col_i < HALF
    left_row = lax.broadcasted_iota(jnp.int32, (1, LANES), 1) < HALF

    def cols2(mat, h0):
        return jnp.where(left, mat[:, h0:h0 + 1], mat[:, h0 + 1:h0 + 2])

    for c in range(rows // L):
        r0 = c * L
        xs = xc[r0:r0 + L, 0:SSD_INNER]
        bm = xc[r0:r0 + L, SSD_INNER:SSD_INNER + SSD_GROUPS * SSD_STATE]
        cm = xc[r0:r0 + L, SSD_INNER + SSD_GROUPS * SSD_STATE:]
        dt_in = dt_ref[r0:r0 + L, :] + dtb_ref[...]
        dt = jnp.maximum(dt_in, 0.0) + jnp.log1p(jnp.exp(-jnp.abs(dt_in)))
        a = dt * a_head
        a_cs = _cumsum_rows(a)
        a_cs_t = a_cs.T
        a_last = a_cs[L - 1:L, :]
        e_cs = jnp.exp(a_cs)
        d_end = jnp.exp(a_last - a_cs)
        c_dec = jnp.exp(a_last)

        y_pairs = []
        for g in range(SSD_GROUPS):
            b_g = bm[:, g * SSD_STATE:(g + 1) * SSD_STATE]
            c_g = cm[:, g * SSD_STATE:(g + 1) * SSD_STATE]
            b_gt = b_g.T.astype(BF16)
            cb = _dot(c_g.astype(BF16), b_gt)
            for j in range(pairs_per_group):
                pj = g * pairs_per_group + j
                h0 = 2 * pj
                x_p = xs[:, pj * LANES:(pj + 1) * LANES]
                xdt = x_p * cols2(dt, h0)
                st = state_ref[pj]
                y_p = x_p * dskip_ref[:, pj * LANES:(pj + 1) * LANES]
                for hh in range(2):
                    hd = h0 + hh
                    seg = a_cs[:, hd:hd + 1] - a_cs_t[hd:hd + 1, :]
                    m = cb * jnp.where(causal, jnp.exp(seg), 0.0)
                    c_s = c_g * e_cs[:, hd:hd + 1]
                    lhs = jnp.concatenate([m, c_s], axis=1).astype(BF16)
                    keep = left if hh == 0 else jnp.logical_not(left)
                    rhs = jnp.concatenate([jnp.where(keep, xdt, 0.0), jnp.where(keep, st, 0.0)],
                                          axis=0).astype(BF16)
                    y_p = y_p + _dot(lhs, rhs)
                new = _dot(b_gt, (xdt * cols2(d_end, h0)).astype(BF16))
                dec = jnp.where(left_row, c_dec[:, h0:h0 + 1], c_dec[:, h0 + 1:h0 + 2])
                state_ref[pj] = st * dec + new
                y_pairs.append(y_p)

        gw = SSD_INNER // SSD_GROUPS
        zt = z_ref[r0:r0 + L, :].astype(F32)
        for g in range(SSD_GROUPS):
            y_g = jnp.concatenate(y_pairs[g * pairs_per_group:(g + 1) * pairs_per_group], axis=1)
            y_g = y_g * _silu(zt[:, g * gw:(g + 1) * gw])
            y_g = _rms_norm(y_g, ng_ref[:, g * gw:(g + 1) * gw])
            y_ref[r0:r0 + L, g * gw:(g + 1) * gw] = y_g.astype(BF16)


def _ssd(xbc, dt, z, conv_w, conv_b, dt_bias, a_log, d_skip, norm_g, batch, seq):
    rows = SSD_ROWS
    steps = seq // rows

    def tok(w):
        return pl.BlockSpec((rows, w), lambda b, c: (b * steps + c, 0))

    pairs = SSD_HEADS // 2
    return pl.pallas_call(
        _ssd_kernel,
        out_shape=jax.ShapeDtypeStruct((batch * seq, SSD_INNER), BF16),
        grid=(batch, steps),
        in_specs=[tok(SSD_XBC), tok(LANES), tok(SSD_INNER), _full(conv_w.shape), _full(conv_b.shape),
                  _full(dt_bias.shape), _full(a_log.shape), _full(d_skip.shape), _full(norm_g.shape)],
        out_specs=tok(SSD_INNER),
        scratch_shapes=[pltpu.VMEM((rows + SUBLANES, SSD_XBC), F32),
                        pltpu.VMEM((pairs, SSD_STATE, LANES), F32)],
        compiler_params=_params("parallel", "arbitrary"),
        name="ssd",
    )(xbc, dt, z, conv_w, conv_b, dt_bias, a_log, d_skip, norm_g)


def _mla_attn_kernel(q_ref, k_ref, v_ref, o_ref, m_ref, acc_ref):
    seq = q_ref.shape[0]
    t = ATTN_TILE
    tq = ATTN_Q_SUB * t
    causal = [lax.broadcasted_iota(jnp.int32, (tq - d * t, t), 0) >= lax.broadcasted_iota(jnp.int32, (tq - d * t, t), 1)
              for d in range(ATTN_Q_SUB)]
    lane = lax.broadcasted_iota(jnp.int32, (tq, LANES), 1)
    heads = tuple(slice(hh * HEAD_PAD, (hh + 1) * HEAD_PAD) for hh in range(2))

    def tile_update(hh, r0, q, k0, width, mask):
        c = heads[hh]
        s = _dot_nt(q, k_ref[pl.ds(k0, width), c])
        if mask is not None:
            s = jnp.where(mask, s, -jnp.inf)
        m = m_ref[hh, r0:, :]
        m_new = jnp.maximum(m, jnp.max(s, axis=-1, keepdims=True))
        p = jnp.exp2(s - pltpu.repeat(m_new, width // LANES, axis=1)).astype(BF16)
        acc_ref[hh, r0:, :] = jnp.exp2(m - m_new) * acc_ref[hh, r0:, :] + _dot(p, v_ref[pl.ds(k0, width), c])
        m_ref[hh, r0:, :] = m_new

    def q_tile(qi, _):
        q0 = pl.multiple_of(qi * tq, tq)
        qs = [q_ref[pl.ds(q0, tq), c] for c in heads]
        m_ref[...] = jnp.full(m_ref.shape, -jnp.inf, F32)
        acc_ref[...] = jnp.zeros(acc_ref.shape, F32)

        def full_tile(kb, _):
            k0 = pl.multiple_of(kb * tq, tq)
            for hh in range(2):
                tile_update(hh, 0, qs[hh], k0, tq, None)
            return 0

        lax.fori_loop(0, qi, full_tile, 0)
        for d in range(ATTN_Q_SUB):
            k0 = pl.multiple_of(q0 + d * t, t)
            for hh in range(2):
                tile_update(hh, d * t, qs[hh][d * t:], k0, t, causal[d])
        acc_a, acc_b = acc_ref[0], acc_ref[1]
        l_a = acc_a[:, HALF:HALF + 1]
        l_b = acc_b[:, 0:1]
        out = jnp.where(lane < HALF, acc_a * (1.0 / l_a), acc_b * (1.0 / l_b))
        o_ref[pl.ds(q0, tq), :] = out.astype(BF16)
        return 0

    lax.fori_loop(0, seq // tq, q_tile, 0)


def _mla_attn(q, k, v, batch, seq):
    pairs = MLA_HEADS // 2
    pw = 2 * HEAD_PAD
    spec = pl.BlockSpec((seq, pw), lambda b, p: (b, p))
    return pl.pallas_call(
        _mla_attn_kernel,
        out_shape=jax.ShapeDtypeStruct((batch * seq, MLA_HEADS * MLA_V), BF16),
        grid=(batch, pairs),
        in_specs=[spec, spec, spec],
        out_specs=pl.BlockSpec((seq, LANES), lambda b, p: (b, p)),
        scratch_shapes=[pltpu.VMEM((2, ATTN_Q_SUB * ATTN_TILE, LANES), F32)] * 2,
        compiler_params=_params("parallel", "parallel"),
        name="mla_attn",
    )(q, k, v)


def _mem_kv_kernel(mem_ref, wk_ref, wv_ref, k_ref, v_ref):
    m = mem_ref[...].astype(BF16)
    k_ref[...] = _dot(m, wk_ref[...]).astype(BF16)
    v_ref[...] = _dot(m, wv_ref[...]).astype(BF16)


def _mem_kv(mem2, wk, wv, batch, mem_tokens):
    d = mem2.shape[1]
    spec = pl.BlockSpec((mem_tokens, d), lambda b: (b, 0))
    return pl.pallas_call(
        _mem_kv_kernel,
        out_shape=(jax.ShapeDtypeStruct(mem2.shape, BF16),) * 2,
        grid=(batch,),
        in_specs=[spec, _full(wk.shape), _full(wv.shape)],
        out_specs=(spec, spec),
        compiler_params=_params("parallel"),
        name="mem_kv",
    )(mem2, wk, wv)


def _mix_xattn_kernel(x_ref, y_ref, o_ref, km_ref, vm_ref, g0_ref, b0_ref, wout_ref, g1_ref, b1_ref,
                      wq_ref, wo_ref, g2_ref, b2_ref, h2_ref):
    h = _layer_norm(x_ref[...], g0_ref[...], b0_ref[...])
    mix = _dot(y_ref[...], wout_ref[0:SSD_INNER, :]) + _dot(o_ref[...], wout_ref[SSD_INNER:, :])
    h1 = _layer_norm(DEEPNORM_ALPHA * h + mix, g1_ref[...], b1_ref[...])

    h1b = h1.astype(BF16)
    d = h1.shape[1]
    hd_w = d // MEM_HEADS
    scale = hd_w ** -0.5
    xa = jnp.zeros_like(h1)
    for hd in range(MEM_HEADS):
        cols = slice(hd * hd_w, (hd + 1) * hd_w)
        q = (_dot(h1b, wq_ref[:, cols]) * scale).astype(BF16)
        s = _dot_nt(q, km_ref[:, cols])
        p = jnp.exp(s - jnp.max(s, axis=-1, keepdims=True))
        p = p * (1.0 / jnp.sum(p, axis=-1, keepdims=True))
        oh = _dot(p.astype(BF16), vm_ref[:, cols])
        xa = xa + _dot(oh.astype(BF16), wo_ref[cols, :])
    h2_ref[...] = _layer_norm(DEEPNORM_ALPHA * h1 + xa, g2_ref[...], b2_ref[...])


def _mix_xattn(x2, y, o, k_mem, v_mem, ln_in_g, ln_in_b, w_out, ln1_g, ln1_b, wq, wo, ln2_g, ln2_b,
               seq, mem_tokens):
    tokens, d = x2.shape
    tm = TOKEN_TILE
    per_batch = seq // tm

    def row(w):
        return pl.BlockSpec((tm, w), lambda i: (i, 0))

    mem_spec = pl.BlockSpec((mem_tokens, d), lambda i: (i // per_batch, 0))
    vec = _full((1, d))
    return pl.pallas_call(
        _mix_xattn_kernel,
        out_shape=jax.ShapeDtypeStruct((tokens, d), F32),
        grid=(tokens // tm,),
        in_specs=[row(d), row(y.shape[1]), row(o.shape[1]), mem_spec, mem_spec, vec, vec,
                  _full(w_out.shape), vec, vec, _full(wq.shape), _full(wo.shape), vec, vec],
        out_specs=row(d),
        compiler_params=_params("parallel"),
        name="mix_xattn",
    )(x2, y, o, k_mem, v_mem, ln_in_g, ln_in_b, w_out, ln1_g, ln1_b, wq, wo, ln2_g, ln2_b)


def _mlp_kernel(h_ref, wup_ref, wdn_ref, g_ref, b_ref, out_ref):
    h = h_ref[...]
    hb = h.astype(BF16)
    ff = jnp.zeros_like(h)
    for c in range(wup_ref.shape[1] // FF_CHUNK):
        cols = slice(c * FF_CHUNK, (c + 1) * FF_CHUNK)
        u = jnp.maximum(_dot(hb, wup_ref[:, cols]), 0.0)
        ff = ff + _dot((u * u).astype(BF16), wdn_ref[cols, :])
    out_ref[...] = _layer_norm(DEEPNORM_ALPHA * h + ff, g_ref[...], b_ref[...])


def _mlp(h2, w_up, w_down, ln_g, ln_b):
    tokens, d = h2.shape
    tm = TOKEN_TILE
    row = pl.BlockSpec((tm, d), lambda i: (i, 0))
    return pl.pallas_call(
        _mlp_kernel,
        out_shape=jax.ShapeDtypeStruct((tokens, d), F32),
        grid=(tokens // tm,),
        in_specs=[row, _full(w_up.shape), _full(w_down.shape), _full((1, d)), _full((1, d))],
        out_specs=row,
        compiler_params=_params("parallel"),
        name="mlp",
    )(h2, w_up, w_down, ln_g, ln_b)


def _rot_cols(w):
    half = MLA_ROPE // 2
    return jnp.concatenate([-w[..., half:], w[..., :half]], axis=-1)


def _pack_in_proj(w_in):
    o = 0
    segs = {}
    for name, width in (("z", SSD_INNER), ("xbc", SSD_XBC), ("dt", SSD_HEADS), ("q", MLA_Q_RANK),
                        ("kv", MLA_KV_RANK), ("kr", MLA_ROPE)):
        segs[name] = w_in[:, o:o + width]
        o += width
    kr, krs = segs["kr"], _rot_cols(segs["kr"])
    dt = jnp.pad(segs["dt"], ((0, 0), (0, LANES - SSD_HEADS)))
    return jnp.concatenate([segs["z"], segs["xbc"], segs["q"], segs["kv"], kr, krs, kr, krs, dt],
                           axis=1).astype(BF16)


def _pack_q_up(w_q_up):
    r = w_q_up.shape[0]
    w = w_q_up.reshape(r, MLA_HEADS, MLA_QK)
    rope = w[:, :, MLA_NOPE:]
    return jnp.concatenate([w[:, :, :MLA_NOPE], rope, _rot_cols(rope)], axis=-1).reshape(
        r, MLA_HEADS * HEAD_PAD).astype(BF16)


def _pack_kv_up(w_kv_up):
    r = w_kv_up.shape[0]
    w = w_kv_up.reshape(r, MLA_HEADS, MLA_NOPE + MLA_V)
    zeros = jnp.zeros((r, MLA_HEADS, HEAD_PAD - MLA_NOPE), w.dtype)
    wk = jnp.concatenate([w[:, :, :MLA_NOPE], zeros], axis=-1)
    wv = w[:, :, MLA_NOPE:]
    zv = jnp.zeros_like(wv)
    even = (jnp.arange(MLA_HEADS) % 2 == 0)[None, :, None]
    wv = jnp.concatenate([jnp.where(even, wv, zv), jnp.where(even, zv, wv)], axis=-1)
    return jnp.concatenate([wk.reshape(r, -1), wv.reshape(r, -1)], axis=1).astype(BF16)


def kernel(x, mem, positions, ln_in_g, ln_in_b, w_in, conv_w, conv_b, dt_bias, a_log, d_skip, ssd_norm_g, q_norm_g, w_q_up, kv_norm_g, w_kv_up, w_mix_out, ln1_g, ln1_b, w_mem_q, w_mem_k, w_mem_v, w_mem_o, ln2_g, ln2_b, w_up, w_down, ln3_g, ln3_b):
    batch, seq, d = x.shape
    mem_tokens = mem.shape[1]
    tokens = batch * seq
    assert w_in.shape[0] == DEPTH == 1
    assert seq % (ATTN_TILE * ATTN_Q_SUB) == 0 and seq % TOKEN_TILE == 0 and seq % SSD_ROWS == 0

    def vec(v):
        return v.reshape(1, -1).astype(F32)

    def pad_lanes(v):
        return jnp.pad(vec(v), ((0, 0), (0, LANES - v.size)))

    x2 = x.reshape(tokens, d)

    half = MLA_ROPE // 2
    inv_freq = jnp.power(ROPE_THETA, -jnp.arange(half, dtype=F32) / half)
    invf = jnp.tile(inv_freq, LANES // half)[None, :]
    head_lane = jnp.arange(MLA_HEADS * HEAD_PAD) % HEAD_PAD
    head_odd = (jnp.arange(MLA_HEADS * HEAD_PAD) // HEAD_PAD) % 2
    vone = (head_lane == jnp.where(head_odd == 1, 0, HALF)).astype(F32)[None, :]

    z, xbc, dt, q, k, v = _in_proj(
        x2, positions.reshape(tokens, 1), invf, vone, vec(ln_in_g), vec(ln_in_b), _pack_in_proj(w_in[0]),
        vec(q_norm_g[0]), _pack_q_up(w_q_up[0]), vec(kv_norm_g[0]), _pack_kv_up(w_kv_up[0]))

    y = _ssd(xbc, dt, z, conv_w[0].astype(F32), vec(conv_b[0]), pad_lanes(dt_bias[0]), pad_lanes(a_log[0]),
             vec(jnp.repeat(d_skip[0], SSD_HEAD_DIM)), vec(ssd_norm_g[0]), batch, seq)
    o = _mla_attn(q, k, v, batch, seq)

    k_mem, v_mem = _mem_kv(mem.reshape(batch * mem_tokens, d), w_mem_k[0].astype(BF16),
                           w_mem_v[0].astype(BF16), batch, mem_tokens)
    h2 = _mix_xattn(x2, y, o, k_mem, v_mem, vec(ln_in_g), vec(ln_in_b), w_mix_out[0].astype(BF16),
                    vec(ln1_g[0]), vec(ln1_b[0]), w_mem_q[0].astype(BF16), w_mem_o[0].astype(BF16),
                    vec(ln2_g[0]), vec(ln2_b[0]), seq, mem_tokens)
    out = _mlp(h2, w_up[0].astype(BF16), w_down[0].astype(BF16), vec(ln3_g[0]), vec(ln3_b[0]))
    return out.reshape(batch, seq, d)
```

```python
import functools

import jax
import jax.numpy as jnp
from jax import lax
from jax.experimental import pallas as pl
from jax.experimental.pallas import tpu as pltpu

F32 = jnp.float32
BF16 = jnp.bfloat16

SSD_HEADS = 8
SSD_HEAD_DIM = 64
SSD_INNER = SSD_HEADS * SSD_HEAD_DIM
SSD_GROUPS = 2
SSD_STATE = 128
SSD_CONV = 4
SSD_CHUNK = 128
SSD_XBC = SSD_INNER + 2 * SSD_GROUPS * SSD_STATE
MLA_HEADS = 8
MLA_NOPE = 64
MLA_ROPE = 32
MLA_QK = MLA_NOPE + MLA_ROPE
MLA_V = 64
MLA_Q_RANK = 384
MLA_KV_RANK = 256
ROPE_THETA = 10000.0
MEM_HEADS = 4
LN_EPS = 1e-5
RMS_EPS = 1e-6
DEPTH = 1
DEEPNORM_ALPHA = (2.0 * DEPTH) ** 0.25

LANES = 128
SUBLANES = 8
VMEM_LIMIT_BYTES = 56 * 1024 * 1024

TOKEN_TILE = 1024
ROW_CHAINS = 4
SSD_ROWS = 256
ATTN_TILE = 512
ATTN_Q_SUB = 2
FF_CHUNK = 1024

HEAD_PAD = LANES
HALF = LANES // 2


def _params(*semantics):
    return pltpu.CompilerParams(dimension_semantics=semantics, vmem_limit_bytes=VMEM_LIMIT_BYTES)


def _full(shape):
    zeros = (0,) * len(shape)
    return pl.BlockSpec(shape, lambda *_: zeros, pipeline_mode=pl.Buffered(1))


def _layer_norm(x, g, b):
    mu = jnp.mean(x, axis=-1, keepdims=True)
    xc = x - mu
    var = jnp.mean(xc * xc, axis=-1, keepdims=True)
    return xc * lax.rsqrt(var + LN_EPS) * g + b


def _rms_norm(x, g):
    ms = jnp.mean(x * x, axis=-1, keepdims=True)
    return x * lax.rsqrt(ms + RMS_EPS) * g


def _silu(x):
    return x * (1.0 / (1.0 + jnp.exp(-x)))


def _dot(a, b):
    return jnp.dot(a, b, preferred_element_type=F32)


def _dot_nt(a, b):
    return lax.dot_general(a, b, (((1,), (1,)), ((), ())), preferred_element_type=F32)


_C_Z = 0
_C_XBC = _C_Z + SSD_INNER
_C_QLAT = _C_XBC + SSD_XBC
_C_KVLAT = _C_QLAT + MLA_Q_RANK
_C_KR = _C_KVLAT + MLA_KV_RANK
_C_DT = _C_KR + LANES
_C_END = _C_DT + LANES

LOG2_E = 1.4426950408889634
MLA_Q_SCALE = MLA_QK ** -0.5 * LOG2_E


def _in_proj_kernel(x_ref, pos_ref, invf_ref, vone_ref, g_ref, b_ref, w1_ref, qg_ref, wq_ref, kvg_ref, wkv_ref,
                    z_ref, xbc_ref, dt_ref, q_ref, k_ref, v_ref):
    sub = x_ref.shape[0] // ROW_CHAINS
    chains = [slice(r * sub, (r + 1) * sub) for r in range(ROW_CHAINS)]
    lane = lax.broadcasted_iota(jnp.int32, (sub, LANES), 1)
    kw = MLA_HEADS * HEAD_PAD

    hs = [_layer_norm(x_ref[rows, :], g_ref[...], b_ref[...]).astype(BF16) for rows in chains]
    for h, rows in zip(hs, chains):
        z_ref[rows, :] = _dot(h, w1_ref[:, _C_Z:_C_XBC]).astype(BF16)
        xbc_ref[rows, :] = _dot(h, w1_ref[:, _C_XBC:_C_QLAT]).astype(BF16)
        dt_ref[rows, :] = _dot(h, w1_ref[:, _C_DT:_C_END])
    qn = [_rms_norm(_dot(h, w1_ref[:, _C_QLAT:_C_KVLAT]), qg_ref[...]).astype(BF16) for h in hs]
    kvn = [_rms_norm(_dot(h, w1_ref[:, _C_KVLAT:_C_KR]), kvg_ref[...]).astype(BF16) for h in hs]
    kr = [_dot(h, w1_ref[:, _C_KR:_C_DT]) for h in hs]

    for r, rows in enumerate(chains):
        ang = pos_ref[rows, :].astype(F32) * invf_ref[...]
        cos = jnp.cos(ang)
        sin = jnp.sin(ang)
        csq = MLA_Q_SCALE * jnp.where(lane < MLA_NOPE, 1.0, jnp.where(lane < MLA_NOPE + MLA_ROPE, cos, sin))
        for hd in range(MLA_HEADS):
            cols = slice(hd * HEAD_PAD, (hd + 1) * HEAD_PAD)
            q_ref[rows, cols] = (_dot(qn[r], wq_ref[:, cols]) * csq).astype(BF16)
        t = kr[r] * jnp.where((lane // MLA_ROPE) % 2 == 0, cos, sin)
        kpe = jnp.where(lane >= MLA_NOPE, t + pltpu.roll(t, MLA_ROPE, 1), 0.0)
        for hd in range(MLA_HEADS):
            cols = slice(hd * HEAD_PAD, (hd + 1) * HEAD_PAD)
            k_ref[rows, cols] = (_dot(kvn[r], wkv_ref[:, cols]) + kpe).astype(BF16)
        v_ref[rows, :] = (_dot(kvn[r], wkv_ref[:, kw:]) + vone_ref[...]).astype(BF16)


def _in_proj(x2, pos, invf, vone, ln_g, ln_b, w1, q_norm_g, wq, kv_norm_g, wkv):
    tokens, d = x2.shape
    tm = TOKEN_TILE
    hw = MLA_HEADS * HEAD_PAD

    def row(w):
        return pl.BlockSpec((tm, w), lambda i: (i, 0))

    out_shape = (
        jax.ShapeDtypeStruct((tokens, SSD_INNER), BF16),
        jax.ShapeDtypeStruct((tokens, SSD_XBC), BF16),
        jax.ShapeDtypeStruct((tokens, LANES), F32),
        jax.ShapeDtypeStruct((tokens, hw), BF16),
        jax.ShapeDtypeStruct((tokens, hw), BF16),
        jax.ShapeDtypeStruct((tokens, hw), BF16),
    )
    return pl.pallas_call(
        _in_proj_kernel,
        out_shape=out_shape,
        grid=(tokens // tm,),
        in_specs=[row(d), row(1), _full((1, LANES)), _full((1, hw)), _full((1, d)), _full((1, d)),
                  _full(w1.shape), _full((1, MLA_Q_RANK)), _full(wq.shape), _full((1, MLA_KV_RANK)),
                  _full(wkv.shape)],
        out_specs=(row(SSD_INNER), row(SSD_XBC), row(LANES), row(hw), row(hw), row(hw)),
        compiler_params=_params("parallel"),
        name="in_proj",
    )(x2, pos, invf, vone, ln_g, ln_b, w1, q_norm_g, wq, kv_norm_g, wkv)


def _cumsum_rows(x):
    n = x.shape[0]
    row = lax.broadcasted_iota(jnp.int32, x.shape, 0)
    shift = 1
    while shift < n:
        x = x + jnp.where(row >= shift, pltpu.roll(x, shift, 0), 0.0)
        shift *= 2
    return x


def _ssd_kernel(xbc_ref, dt_ref, z_ref, cw_ref, cb_ref, dtb_ref, alog_ref, dskip_ref, ng_ref,
                y_ref, ext_ref, state_ref):
    rows = xbc_ref.shape[0]
    L = SSD_CHUNK
    pairs = SSD_HEADS // 2
    pairs_per_group = pairs // SSD_GROUPS

    @pl.when(pl.program_id(1) == 0)
    def _():
        ext_ref[0:SUBLANES, :] = jnp.zeros((SUBLANES, SSD_XBC), F32)
        state_ref[...] = jnp.zeros(state_ref.shape, F32)

    ext_ref[SUBLANES:SUBLANES + rows, :] = xbc_ref[...].astype(F32)
    conv = cb_ref[...]
    for k in range(SSD_CONV):
        off = SUBLANES - (SSD_CONV - 1) + k
        conv = conv + cw_ref[k:k + 1, :] * ext_ref[off:off + rows, :]
    ext_ref[0:SUBLANES, :] = ext_ref[rows:rows + SUBLANES, :]
    xc = _silu(conv)

    a_head = -jnp.exp(alog_ref[...])
    head_lane = lax.broadcasted_iota(jnp.int32, (1, LANES), 1) < SSD_HEADS
    a_head = jnp.where(head_lane, a_head, 0.0)

    sq = (L, L)
    row_i = lax.broadcasted_iota(jnp.int32, sq, 0)
    col_i = lax.broadcasted_iota(jnp.int32, sq, 1)
    causal = row_i >= col_i
    left = col_i < HALF
    left_row = lax.broadcasted_iota(jnp.int32, (1, LANES), 1) < HALF

    def cols2(mat, h0):
        return jnp.where(left, mat[:, h0:h0 + 1], mat[:, h0 + 1:h0 + 2])

    for c in range(rows // L):
        r0 = c * L
        xs = xc[r0:r0 + L, 0:SSD_INNER]
        bm = xc[r0:r0 + L, SSD_INNER:SSD_INNER + SSD_GROUPS * SSD_STATE]
        cm = xc[r0:r0 + L, SSD_INNER + SSD_GROUPS * SSD_STATE:]
        dt_in = dt_ref[r0:r0 + L, :] + dtb_ref[...]
        dt = jnp.maximum(dt_in, 0.0) + jnp.log1p(jnp.exp(-jnp.abs(dt_in)))
        a = dt * a_head
        a_cs = _cumsum_rows(a)
        a_cs_t = a_cs.T
        a_last = a_cs[L - 1:L, :]
        e_cs = jnp.exp(a_cs)
        d_end = jnp.exp(a_last - a_cs)
        c_dec = jnp.exp(a_last)

        y_pairs = []
        for g in range(SSD_GROUPS):
            b_g = bm[:, g * SSD_STATE:(g + 1) * SSD_STATE]
            c_g = cm[:, g * SSD_STATE:(g + 1) * SSD_STATE]
            b_gt = b_g.T.astype(BF16)
            cb = _dot(c_g.astype(BF16), b_gt)
            for j in range(pairs_per_group):
                pj = g * pairs_per_group + j
                h0 = 2 * pj
                x_p = xs[:, pj * LANES:(pj + 1) * LANES]
                xdt = x_p * cols2(dt, h0)
                st = state_ref[pj]
                y_p = x_p * dskip_ref[:, pj * LANES:(pj + 1) * LANES]
                for hh in range(2):
                    hd = h0 + hh
                    seg = a_cs[:, hd:hd + 1] - a_cs_t[hd:hd + 1, :]
                    m = cb * jnp.where(causal, jnp.exp(seg), 0.0)
                    c_s = c_g * e_cs[:, hd:hd + 1]
                    lhs = jnp.concatenate([m, c_s], axis=1).astype(BF16)
                    keep = left if hh == 0 else jnp.logical_not(left)
                    rhs = jnp.concatenate([jnp.where(keep, xdt, 0.0), jnp.where(keep, st, 0.0)],
                                          axis=0).astype(BF16)
                    y_p = y_p + _dot(lhs, rhs)
                new = _dot(b_gt, (xdt * cols2(d_end, h0)).astype(BF16))
                dec = jnp.where(left_row, c_dec[:, h0:h0 + 1], c_dec[:, h0 + 1:h0 + 2])
                state_ref[pj] = st * dec + new
                y_pairs.append(y_p)

        gw = SSD_INNER // SSD_GROUPS
        zt = z_ref[r0:r0 + L, :].astype(F32)
        for g in range(SSD_GROUPS):
            y_g = jnp.concatenate(y_pairs[g * pairs_per_group:(g + 1) * pairs_per_group], axis=1)
            y_g = y_g * _silu(zt[:, g * gw:(g + 1) * gw])
            y_g = _rms_norm(y_g, ng_ref[:, g * gw:(g + 1) * gw])
            y_ref[r0:r0 + L, g * gw:(g + 1) * gw] = y_g.astype(BF16)


def _ssd(xbc, dt, z, conv_w, conv_b, dt_bias, a_log, d_skip, norm_g, batch, seq):
    rows = SSD_ROWS
    steps = seq // rows

    def tok(w):
        return pl.BlockSpec((rows, w), lambda b, c: (b * steps + c, 0))

    pairs = SSD_HEADS // 2
    return pl.pallas_call(
        _ssd_kernel,
        out_shape=jax.ShapeDtypeStruct((batch * seq, SSD_INNER), BF16),
        grid=(batch, steps),
        in_specs=[tok(SSD_XBC), tok(LANES), tok(SSD_INNER), _full(conv_w.shape), _full(conv_b.shape),
                  _full(dt_bias.shape), _full(a_log.shape), _full(d_skip.shape), _full(norm_g.shape)],
        out_specs=tok(SSD_INNER),
        scratch_shapes=[pltpu.VMEM((rows + SUBLANES, SSD_XBC), F32),
                        pltpu.VMEM((pairs, SSD_STATE, LANES), F32)],
        compiler_params=_params("parallel", "arbitrary"),
        name="ssd",
    )(xbc, dt, z, conv_w, conv_b, dt_bias, a_log, d_skip, norm_g)


def _mla_attn_kernel(q_ref, k_ref, v_ref, o_ref, m_ref, acc_ref):
    seq = q_ref.shape[0]
    t = ATTN_TILE
    tq = ATTN_Q_SUB * t
    causal = [lax.broadcasted_iota(jnp.int32, (tq - d * t, t), 0) >= lax.broadcasted_iota(jnp.int32, (tq - d * t, t), 1)
              for d in range(ATTN_Q_SUB)]
    lane = lax.broadcasted_iota(jnp.int32, (tq, LANES), 1)
    heads = tuple(slice(hh * HEAD_PAD, (hh + 1) * HEAD_PAD) for hh in range(2))

    def tile_update(hh, r0, q, k0, width, mask):
        c = heads[hh]
        s = _dot_nt(q, k_ref[pl.ds(k0, width), c])
        if mask is not None:
            s = jnp.where(mask, s, -jnp.inf)
        m = m_ref[hh, r0:, :]
        m_new = jnp.maximum(m, jnp.max(s, axis=-1, keepdims=True))
        p = jnp.exp2(s - pltpu.repeat(m_new, width // LANES, axis=1)).astype(BF16)
        acc_ref[hh, r0:, :] = jnp.exp2(m - m_new) * acc_ref[hh, r0:, :] + _dot(p, v_ref[pl.ds(k0, width), c])
        m_ref[hh, r0:, :] = m_new

    def q_tile(qi, _):
        q0 = pl.multiple_of(qi * tq, tq)
        qs = [q_ref[pl.ds(q0, tq), c] for c in heads]
        m_ref[...] = jnp.full(m_ref.shape, -jnp.inf, F32)
        acc_ref[...] = jnp.zeros(acc_ref.shape, F32)

        def full_tile(kb, _):
            k0 = pl.multiple_of(kb * tq, tq)
            for hh in range(2):
                tile_update(hh, 0, qs[hh], k0, tq, None)
            return 0

        lax.fori_loop(0, qi, full_tile, 0)
        for d in range(ATTN_Q_SUB):
            k0 = pl.multiple_of(q0 + d * t, t)
            for hh in range(2):
                tile_update(hh, d * t, qs[hh][d * t:], k0, t, causal[d])
        acc_a, acc_b = acc_ref[0], acc_ref[1]
        l_a = acc_a[:, HALF:HALF + 1]
        l_b = acc_b[:, 0:1]
        out = jnp.where(lane < HALF, acc_a * (1.0 / l_a), acc_b * (1.0 / l_b))
        o_ref[pl.ds(q0, tq), :] = out.astype(BF16)
        return 0

    lax.fori_loop(0, seq // tq, q_tile, 0)


def _mla_attn(q, k, v, batch, seq):
    pairs = MLA_HEADS // 2
    pw = 2 * HEAD_PAD
    spec = pl.BlockSpec((seq, pw), lambda b, p: (b, p))
    return pl.pallas_call(
        _mla_attn_kernel,
        out_shape=jax.ShapeDtypeStruct((batch * seq, MLA_HEADS * MLA_V), BF16),
        grid=(batch, pairs),
        in_specs=[spec, spec, spec],
        out_specs=pl.BlockSpec((seq, LANES), lambda b, p: (b, p)),
        scratch_shapes=[pltpu.VMEM((2, ATTN_Q_SUB * ATTN_TILE, LANES), F32)] * 2,
        compiler_params=_params("parallel", "parallel"),
        name="mla_attn",
    )(q, k, v)


def _mem_kv_kernel(mem_ref, wk_ref, wv_ref, k_ref, v_ref):
    m = mem_ref[...].astype(BF16)
    k_ref[...] = _dot(m, wk_ref[...]).astype(BF16)
    v_ref[...] = _dot(m, wv_ref[...]).astype(BF16)


def _mem_kv(mem2, wk, wv, batch, mem_tokens):
    d = mem2.shape[1]
    spec = pl.BlockSpec((mem_tokens, d), lambda b: (b, 0))
    return pl.pallas_call(
        _mem_kv_kernel,
        out_shape=(jax.ShapeDtypeStruct(mem2.shape, BF16),) * 2,
        grid=(batch,),
        in_specs=[spec, _full(wk.shape), _full(wv.shape)],
        out_specs=(spec, spec),
        compiler_params=_params("parallel"),
        name="mem_kv",
    )(mem2, wk, wv)


def _mix_xattn_kernel(x_ref, y_ref, o_ref, km_ref, vm_ref, g0_ref, b0_ref, wout_ref, g1_ref, b1_ref,
                      wq_ref, wo_ref, g2_ref, b2_ref, h2_ref):
    d = x_ref.shape[1]
    hd_w = d // MEM_HEADS
    scale = hd_w ** -0.5
    sub = x_ref.shape[0] // ROW_CHAINS
    chains = [slice(r * sub, (r + 1) * sub) for r in range(ROW_CHAINS)]
    mix = [_dot(y_ref[rows, :], wout_ref[0:SSD_INNER, :]) + _dot(o_ref[rows, :], wout_ref[SSD_INNER:, :])
           for rows in chains]
    h1 = [_layer_norm(DEEPNORM_ALPHA * _layer_norm(x_ref[rows, :], g0_ref[...], b0_ref[...]) + mix[r],
                      g1_ref[...], b1_ref[...]) for r, rows in enumerate(chains)]
    h1b = [v.astype(BF16) for v in h1]
    xa = [None] * ROW_CHAINS
    for hd in range(MEM_HEADS):
        cols = slice(hd * hd_w, (hd + 1) * hd_w)
        q = [(_dot(v, wq_ref[:, cols]) * scale).astype(BF16) for v in h1b]
        s = [_dot_nt(v, km_ref[:, cols]) for v in q]
        p = [jnp.exp(v - jnp.max(v, axis=-1, keepdims=True)) for v in s]
        p = [(v * (1.0 / jnp.sum(v, axis=-1, keepdims=True))).astype(BF16) for v in p]
        oh = [_dot(v, vm_ref[:, cols]).astype(BF16) for v in p]
        for r in range(ROW_CHAINS):
            part = _dot(oh[r], wo_ref[cols, :])
            xa[r] = part if xa[r] is None else xa[r] + part
    for r, rows in enumerate(chains):
        h2_ref[rows, :] = _layer_norm(DEEPNORM_ALPHA * h1[r] + xa[r], g2_ref[...], b2_ref[...])


def _mix_xattn(x2, y, o, k_mem, v_mem, ln_in_g, ln_in_b, w_out, ln1_g, ln1_b, wq, wo, ln2_g, ln2_b,
               seq, mem_tokens):
    tokens, d = x2.shape
    tm = TOKEN_TILE
    per_batch = seq // tm

    def row(w):
        return pl.BlockSpec((tm, w), lambda i: (i, 0))

    mem_spec = pl.BlockSpec((mem_tokens, d), lambda i: (i // per_batch, 0))
    vec = _full((1, d))
    return pl.pallas_call(
        _mix_xattn_kernel,
        out_shape=jax.ShapeDtypeStruct((tokens, d), F32),
        grid=(tokens // tm,),
        in_specs=[row(d), row(y.shape[1]), row(o.shape[1]), mem_spec, mem_spec, vec, vec,
                  _full(w_out.shape), vec, vec, _full(wq.shape), _full(wo.shape), vec, vec],
        out_specs=row(d),
        compiler_params=_params("parallel"),
        name="mix_xattn",
    )(x2, y, o, k_mem, v_mem, ln_in_g, ln_in_b, w_out, ln1_g, ln1_b, wq, wo, ln2_g, ln2_b)


def _mlp_kernel(h_ref, wup_ref, wdn_ref, g_ref, b_ref, out_ref):
    h = h_ref[...]
    hb = h.astype(BF16)
    ff = jnp.zeros_like(h)
    for c in range(wup_ref.shape[1] // FF_CHUNK):
        cols = slice(c * FF_CHUNK, (c + 1) * FF_CHUNK)
        u = jnp.maximum(_dot(hb, wup_ref[:, cols]), 0.0)
        ff = ff + _dot((u * u).astype(BF16), wdn_ref[cols, :])
    out_ref[...] = _layer_norm(DEEPNORM_ALPHA * h + ff, g_ref[...], b_ref[...])


def _mlp(h2, w_up, w_down, ln_g, ln_b):
    tokens, d = h2.shape
    tm = TOKEN_TILE
    row = pl.BlockSpec((tm, d), lambda i: (i, 0))
    return pl.pallas_call(
        _mlp_kernel,
        out_shape=jax.ShapeDtypeStruct((tokens, d), F32),
        grid=(tokens // tm,),
        in_specs=[row, _full(w_up.shape), _full(w_down.shape), _full((1, d)), _full((1, d))],
        out_specs=row,
        compiler_params=_params("parallel"),
        name="mlp",
    )(h2, w_up, w_down, ln_g, ln_b)


def _rot_cols(w):
    half = MLA_ROPE // 2
    return jnp.concatenate([-w[..., half:], w[..., :half]], axis=-1)


def _pack_in_proj(w_in):
    o = 0
    segs = {}
    for name, width in (("z", SSD_INNER), ("xbc", SSD_XBC), ("dt", SSD_HEADS), ("q", MLA_Q_RANK),
                        ("kv", MLA_KV_RANK), ("kr", MLA_ROPE)):
        segs[name] = w_in[:, o:o + width]
        o += width
    kr, krs = segs["kr"], _rot_cols(segs["kr"])
    dt = jnp.pad(segs["dt"], ((0, 0), (0, LANES - SSD_HEADS)))
    return jnp.concatenate([segs["z"], segs["xbc"], segs["q"], segs["kv"], kr, krs, kr, krs, dt],
                           axis=1).astype(BF16)


def _pack_q_up(w_q_up):
    r = w_q_up.shape[0]
    w = w_q_up.reshape(r, MLA_HEADS, MLA_QK)
    rope = w[:, :, MLA_NOPE:]
    return jnp.concatenate([w[:, :, :MLA_NOPE], rope, _rot_cols(rope)], axis=-1).reshape(
        r, MLA_HEADS * HEAD_PAD).astype(BF16)


def _pack_kv_up(w_kv_up):
    r = w_kv_up.shape[0]
    w = w_kv_up.reshape(r, MLA_HEADS, MLA_NOPE + MLA_V)
    zeros = jnp.zeros((r, MLA_HEADS, HEAD_PAD - MLA_NOPE), w.dtype)
    wk = jnp.concatenate([w[:, :, :MLA_NOPE], zeros], axis=-1)
    wv = w[:, :, MLA_NOPE:]
    zv = jnp.zeros_like(wv)
    even = (jnp.arange(MLA_HEADS) % 2 == 0)[None, :, None]
    wv = jnp.concatenate([jnp.where(even, wv, zv), jnp.where(even, zv, wv)], axis=-1)
    return jnp.concatenate([wk.reshape(r, -1), wv.reshape(r, -1)], axis=1).astype(BF16)


def kernel(x, mem, positions, ln_in_g, ln_in_b, w_in, conv_w, conv_b, dt_bias, a_log, d_skip, ssd_norm_g, q_norm_g, w_q_up, kv_norm_g, w_kv_up, w_mix_out, ln1_g, ln1_b, w_mem_q, w_mem_k, w_mem_v, w_mem_o, ln2_g, ln2_b, w_up, w_down, ln3_g, ln3_b):
    batch, seq, d = x.shape
    mem_tokens = mem.shape[1]
    tokens = batch * seq
    assert w_in.shape[0] == DEPTH == 1
    assert seq % (ATTN_TILE * ATTN_Q_SUB) == 0 and seq % TOKEN_TILE == 0 and seq % SSD_ROWS == 0

    def vec(v):
        return v.reshape(1, -1).astype(F32)

    def pad_lanes(v):
        return jnp.pad(vec(v), ((0, 0), (0, LANES - v.size)))

    x2 = x.reshape(tokens, d)

    half = MLA_ROPE // 2
    inv_freq = jnp.power(ROPE_THETA, -jnp.arange(half, dtype=F32) / half)
    invf = jnp.tile(inv_freq, LANES // half)[None, :]
    head_lane = jnp.arange(MLA_HEADS * HEAD_PAD) % HEAD_PAD
    head_odd = (jnp.arange(MLA_HEADS * HEAD_PAD) // HEAD_PAD) % 2
    vone = (head_lane == jnp.where(head_odd == 1, 0, HALF)).astype(F32)[None, :]

    z, xbc, dt, q, k, v = _in_proj(
        x2, positions.reshape(tokens, 1), invf, vone, vec(ln_in_g), vec(ln_in_b), _pack_in_proj(w_in[0]),
        vec(q_norm_g[0]), _pack_q_up(w_q_up[0]), vec(kv_norm_g[0]), _pack_kv_up(w_kv_up[0]))

    y = _ssd(xbc, dt, z, conv_w[0].astype(F32), vec(conv_b[0]), pad_lanes(dt_bias[0]), pad_lanes(a_log[0]),
             vec(jnp.repeat(d_skip[0], SSD_HEAD_DIM)), vec(ssd_norm_g[0]), batch, seq)
    o = _mla_attn(q, k, v, batch, seq)

    k_mem, v_mem = _mem_kv(mem.reshape(batch * mem_tokens, d), w_mem_k[0].astype(BF16),
                           w_mem_v[0].astype(BF16), batch, mem_tokens)
    h2 = _mix_xattn(x2, y, o, k_mem, v_mem, vec(ln_in_g), vec(ln_in_b), w_mix_out[0].astype(BF16),
                    vec(ln1_g[0]), vec(ln1_b[0]), w_mem_q[0].astype(BF16), w_mem_o[0].astype(BF16),
                    vec(ln2_g[0]), vec(ln2_b[0]), seq, mem_tokens)
    out = _mlp(h2, w_up[0].astype(BF16), w_down[0].astype(BF16), vec(ln3_g[0]), vec(ln3_b[0]))
    return out.reshape(batch, seq, d)
```

```python
import functools

import jax
import jax.numpy as jnp
from jax import lax
from jax.experimental import pallas as pl
from jax.experimental.pallas import tpu as pltpu

F32 = jnp.float32
BF16 = jnp.bfloat16

SSD_HEADS = 8
SSD_HEAD_DIM = 64
SSD_INNER = SSD_HEADS * SSD_HEAD_DIM
SSD_GROUPS = 2
SSD_STATE = 128
SSD_CONV = 4
SSD_CHUNK = 128
SSD_XBC = SSD_INNER + 2 * SSD_GROUPS * SSD_STATE
MLA_HEADS = 8
MLA_NOPE = 64
MLA_ROPE = 32
MLA_QK = MLA_NOPE + MLA_ROPE
MLA_V = 64
MLA_Q_RANK = 384
MLA_KV_RANK = 256
ROPE_THETA = 10000.0
MEM_HEADS = 4
LN_EPS = 1e-5
RMS_EPS = 1e-6
DEPTH = 1
DEEPNORM_ALPHA = (2.0 * DEPTH) ** 0.25

LANES = 128
SUBLANES = 8
VMEM_LIMIT_BYTES = 56 * 1024 * 1024

TOKEN_TILE = 1024
ROW_CHAINS = 4
SSD_ROWS = 512
SSD_CONV_ROWS = 256
ATTN_TILE = 512
ATTN_Q_SUB = 2
FF_CHUNK = 1024
MLP_CHAINS = 2

HEAD_PAD = LANES
HALF = LANES // 2


def _params(*semantics):
    return pltpu.CompilerParams(dimension_semantics=semantics, vmem_limit_bytes=VMEM_LIMIT_BYTES)


def _full(shape):
    zeros = (0,) * len(shape)
    return pl.BlockSpec(shape, lambda *_: zeros, pipeline_mode=pl.Buffered(1))


def _layer_norm(x, g, b):
    mu = jnp.mean(x, axis=-1, keepdims=True)
    xc = x - mu
    var = jnp.mean(xc * xc, axis=-1, keepdims=True)
    return xc * lax.rsqrt(var + LN_EPS) * g + b


def _rms_norm(x, g):
    ms = jnp.mean(x * x, axis=-1, keepdims=True)
    return x * lax.rsqrt(ms + RMS_EPS) * g


def _silu(x):
    hx = 0.5 * x
    return hx + hx * jnp.tanh(hx)


def _dot(a, b):
    return jnp.dot(a, b, preferred_element_type=F32)


def _dot_nt(a, b):
    return lax.dot_general(a, b, (((1,), (1,)), ((), ())), preferred_element_type=F32)


_C_Z = 0
_C_XBC = _C_Z + SSD_INNER
_C_QLAT = _C_XBC + SSD_XBC
_C_KVLAT = _C_QLAT + MLA_Q_RANK
_C_KR = _C_KVLAT + MLA_KV_RANK
_C_DT = _C_KR + LANES
_C_END = _C_DT + LANES

LOG2_E = 1.4426950408889634
MLA_Q_SCALE = MLA_QK ** -0.5 * LOG2_E


def _in_proj_kernel(x_ref, pos_ref, invf_ref, vone_ref, g_ref, b_ref, w1_ref, qg_ref, wq_ref, kvg_ref, wkv_ref,
                    z_ref, xbc_ref, dt_ref, q_ref, k_ref, v_ref):
    sub = x_ref.shape[0] // ROW_CHAINS
    chains = [slice(r * sub, (r + 1) * sub) for r in range(ROW_CHAINS)]
    lane = lax.broadcasted_iota(jnp.int32, (sub, LANES), 1)
    kw = MLA_HEADS * HEAD_PAD

    hs = [_layer_norm(x_ref[rows, :], g_ref[...], b_ref[...]).astype(BF16) for rows in chains]
    for h, rows in zip(hs, chains):
        z_ref[rows, :] = _dot(h, w1_ref[:, _C_Z:_C_XBC]).astype(BF16)
        xbc_ref[rows, :] = _dot(h, w1_ref[:, _C_XBC:_C_QLAT]).astype(BF16)
        dt_ref[rows, :] = _dot(h, w1_ref[:, _C_DT:_C_END])
    qn = [_rms_norm(_dot(h, w1_ref[:, _C_QLAT:_C_KVLAT]), qg_ref[...]).astype(BF16) for h in hs]
    kvn = [_rms_norm(_dot(h, w1_ref[:, _C_KVLAT:_C_KR]), kvg_ref[...]).astype(BF16) for h in hs]
    kr = [_dot(h, w1_ref[:, _C_KR:_C_DT]) for h in hs]

    for r, rows in enumerate(chains):
        ang = pos_ref[rows, :].astype(F32) * invf_ref[...]
        cos = jnp.cos(ang)
        sin = jnp.sin(ang)
        csq = MLA_Q_SCALE * jnp.where(lane < MLA_NOPE, 1.0, jnp.where(lane < MLA_NOPE + MLA_ROPE, cos, sin))
        for hd in range(MLA_HEADS):
            cols = slice(hd * HEAD_PAD, (hd + 1) * HEAD_PAD)
            q_ref[rows, cols] = (_dot(qn[r], wq_ref[:, cols]) * csq).astype(BF16)
        t = kr[r] * jnp.where((lane // MLA_ROPE) % 2 == 0, cos, sin)
        kpe = jnp.where(lane >= MLA_NOPE, t + pltpu.roll(t, MLA_ROPE, 1), 0.0)
        for hd in range(MLA_HEADS):
            cols = slice(hd * HEAD_PAD, (hd + 1) * HEAD_PAD)
            k_ref[rows, cols] = (_dot(kvn[r], wkv_ref[:, cols]) + kpe).astype(BF16)
        v_ref[rows, :] = (_dot(kvn[r], wkv_ref[:, kw:]) + vone_ref[...]).astype(BF16)


def _in_proj(x2, pos, invf, vone, ln_g, ln_b, w1, q_norm_g, wq, kv_norm_g, wkv):
    tokens, d = x2.shape
    tm = TOKEN_TILE
    hw = MLA_HEADS * HEAD_PAD

    def row(w):
        return pl.BlockSpec((tm, w), lambda i: (i, 0))

    out_shape = (
        jax.ShapeDtypeStruct((tokens, SSD_INNER), BF16),
        jax.ShapeDtypeStruct((tokens, SSD_XBC), BF16),
        jax.ShapeDtypeStruct((tokens, LANES), F32),
        jax.ShapeDtypeStruct((tokens, hw), BF16),
        jax.ShapeDtypeStruct((tokens, hw), BF16),
        jax.ShapeDtypeStruct((tokens, hw), BF16),
    )
    return pl.pallas_call(
        _in_proj_kernel,
        out_shape=out_shape,
        grid=(tokens // tm,),
        in_specs=[row(d), row(1), _full((1, LANES)), _full((1, hw)), _full((1, d)), _full((1, d)),
                  _full(w1.shape), _full((1, MLA_Q_RANK)), _full(wq.shape), _full((1, MLA_KV_RANK)),
                  _full(wkv.shape)],
        out_specs=(row(SSD_INNER), row(SSD_XBC), row(LANES), row(hw), row(hw), row(hw)),
        compiler_params=_params("parallel"),
        name="in_proj",
    )(x2, pos, invf, vone, ln_g, ln_b, w1, q_norm_g, wq, kv_norm_g, wkv)


def _cumsum_lanes(x, tri):
    x1 = x.astype(BF16).astype(F32)
    r1 = x - x1
    x2 = r1.astype(BF16).astype(F32)
    x3 = r1 - x2
    parts = jnp.concatenate([x1, x2, x3, jnp.zeros_like(x)], axis=0).astype(BF16)
    sums = _dot(parts, tri)
    n = x.shape[0]
    return sums[0:n] + sums[n:2 * n] + sums[2 * n:3 * n]


def _ssd_kernel(xbc_ref, dt_ref, z_ref, shift_ref, tri_ref, cw_ref, cb_ref, dtb_ref, alog_ref, dskip_ref, ng_ref,
                y_ref, tail_ref, state_ref):
    rows = xbc_ref.shape[0]
    L = SSD_CHUNK
    pairs = SSD_HEADS // 2
    pairs_per_group = pairs // SSD_GROUPS

    @pl.when(pl.program_id(1) == 0)
    def _():
        tail_ref[...] = jnp.zeros(tail_ref.shape, F32)
        state_ref[...] = jnp.zeros(state_ref.shape, F32)

    a_head = -jnp.exp(alog_ref[...])
    scalars = []
    for c in range(rows // L):
        dt_in = dt_ref[c * L:(c + 1) * L, :].T[0:SSD_HEADS, :] + dtb_ref[...]
        dt_t = jnp.maximum(dt_in, 0.0) + jnp.log1p(jnp.exp(-jnp.abs(dt_in)))
        a_cs_t = _cumsum_lanes(dt_t * a_head, tri_ref[...])
        a_last = a_cs_t[:, L - 1:L]
        c_dec = jnp.exp(jnp.broadcast_to(a_last, a_cs_t.shape))
        w_end_t = dt_t * jnp.exp(a_last - a_cs_t)
        a_cs = jnp.concatenate([a_cs_t, jnp.zeros((L - SSD_HEADS, L), F32)], axis=0).T
        scalars.append((dt_t, a_cs_t, w_end_t, c_dec, a_cs))

    cr = shift_ref.shape[1]
    tail = tail_ref[...]
    row8 = lax.broadcasted_iota(jnp.int32, tail.shape, 0)
    xc_blocks = []
    for blk in range(rows // cr):
        sh = _dot(shift_ref[...], xbc_ref[blk * cr:(blk + 1) * cr, :])
        conv = cb_ref[...]
        for k in range(SSD_CONV):
            conv = conv + cw_ref[k:k + 1, :] * sh[k * cr:(k + 1) * cr, :]
        head_fix = jnp.zeros(tail.shape, F32)
        for k in range(SSD_CONV - 1):
            delay = SSD_CONV - 1 - k
            head_fix = head_fix + cw_ref[k:k + 1, :] * jnp.where(row8 < delay, pltpu.roll(tail, delay, 0), 0.0)
        conv = jnp.concatenate([conv[0:SUBLANES, :] + head_fix, conv[SUBLANES:, :]], axis=0)
        tail = sh[SSD_CONV * cr - SUBLANES:, :]
        xc_blocks.append(_silu(conv))
    tail_ref[...] = tail

    sq = (L, L)
    row_i = lax.broadcasted_iota(jnp.int32, sq, 0)
    col_i = lax.broadcasted_iota(jnp.int32, sq, 1)
    causal = row_i >= col_i
    left = col_i < HALF
    left_row = lax.broadcasted_iota(jnp.int32, (1, LANES), 1) < HALF

    states = [state_ref[pj] for pj in range(pairs)]
    for c in range(rows // L):
        r0 = c * L
        xc = xc_blocks[r0 // cr][r0 % cr:r0 % cr + L, :]
        xs = xc[:, 0:SSD_INNER]
        bm = xc[:, SSD_INNER:SSD_INNER + SSD_GROUPS * SSD_STATE]
        cm = xc[:, SSD_INNER + SSD_GROUPS * SSD_STATE:]
        dt_t, a_cs_t, w_end_t, c_dec, a_cs = scalars[c]

        y_pairs = []
        for g in range(SSD_GROUPS):
            b_g = bm[:, g * SSD_STATE:(g + 1) * SSD_STATE]
            c_g = cm[:, g * SSD_STATE:(g + 1) * SSD_STATE]
            b_gt = b_g.T
            cb = _dot(c_g.astype(BF16), b_gt.astype(BF16))
            for j in range(pairs_per_group):
                pj = g * pairs_per_group + j
                x_p = xs[:, pj * LANES:(pj + 1) * LANES]
                st = states[pj]
                y_p = x_p * dskip_ref[:, pj * LANES:(pj + 1) * LANES]
                new = None
                for hh in range(2):
                    hd = 2 * pj + hh
                    cs_l = jnp.broadcast_to(a_cs[:, hd:hd + 1], sq)
                    m = cb * jnp.where(causal, jnp.exp(cs_l - a_cs_t[hd:hd + 1, :]), 0.0) * dt_t[hd:hd + 1, :]
                    lhs = jnp.concatenate([m, c_g * jnp.exp(cs_l)], axis=1).astype(BF16)
                    keep = left if hh == 0 else jnp.logical_not(left)
                    x_h = jnp.where(keep, x_p, 0.0).astype(BF16)
                    rhs = jnp.concatenate([x_h, jnp.where(keep, st, 0.0).astype(BF16)], axis=0)
                    y_p = y_p + _dot(lhs, rhs)
                    part = _dot((b_gt * w_end_t[hd:hd + 1, :]).astype(BF16), x_h)
                    new = part if new is None else new + part
                dec = jnp.where(left_row, c_dec[2 * pj:2 * pj + 1, :], c_dec[2 * pj + 1:2 * pj + 2, :])
                states[pj] = st * dec + new
                y_pairs.append(y_p)

        gw = SSD_INNER // SSD_GROUPS
        zt = z_ref[r0:r0 + L, :].astype(F32)
        for g in range(SSD_GROUPS):
            y_g = jnp.concatenate(y_pairs[g * pairs_per_group:(g + 1) * pairs_per_group], axis=1)
            y_g = y_g * _silu(zt[:, g * gw:(g + 1) * gw])
            y_g = _rms_norm(y_g, ng_ref[:, g * gw:(g + 1) * gw])
            y_ref[r0:r0 + L, g * gw:(g + 1) * gw] = y_g.astype(BF16)
    for pj in range(pairs):
        state_ref[pj] = states[pj]


def _ssd(xbc, dt, z, conv_w, conv_b, dt_bias, a_log, d_skip, norm_g, batch, seq):
    rows = SSD_ROWS
    steps = seq // rows

    def tok(w):
        return pl.BlockSpec((rows, w), lambda b, c: (b * steps + c, 0))

    t_out = jnp.arange(SSD_CONV_ROWS)[:, None]
    t_in = jnp.arange(SSD_CONV_ROWS)[None, :]
    shift = jnp.concatenate([(t_out - t_in == SSD_CONV - 1 - k) for k in range(SSD_CONV)], axis=0).astype(BF16)
    tri = (jnp.arange(SSD_CHUNK)[:, None] <= jnp.arange(SSD_CHUNK)[None, :]).astype(BF16)

    pairs = SSD_HEADS // 2
    return pl.pallas_call(
        _ssd_kernel,
        out_shape=jax.ShapeDtypeStruct((batch * seq, SSD_INNER), BF16),
        grid=(batch, steps),
        in_specs=[tok(SSD_XBC), tok(LANES), tok(SSD_INNER), _full(shift.shape), _full(tri.shape),
                  _full(conv_w.shape), _full(conv_b.shape), _full(dt_bias.shape), _full(a_log.shape),
                  _full(d_skip.shape), _full(norm_g.shape)],
        out_specs=tok(SSD_INNER),
        scratch_shapes=[pltpu.VMEM((SUBLANES, SSD_XBC), F32),
                        pltpu.VMEM((pairs, SSD_STATE, LANES), F32)],
        compiler_params=_params("parallel", "arbitrary"),
        name="ssd",
    )(xbc, dt, z, shift, tri, conv_w, conv_b, dt_bias, a_log, d_skip, norm_g)


def _mla_attn_kernel(q_ref, k_ref, v_ref, o_ref, m_ref, acc_ref):
    seq = q_ref.shape[0]
    t = ATTN_TILE
    tq = ATTN_Q_SUB * t
    causal = [lax.broadcasted_iota(jnp.int32, (tq - d * t, t), 0) >= lax.broadcasted_iota(jnp.int32, (tq - d * t, t), 1)
              for d in range(ATTN_Q_SUB)]
    lane = lax.broadcasted_iota(jnp.int32, (tq, LANES), 1)
    heads = tuple(slice(hh * HEAD_PAD, (hh + 1) * HEAD_PAD) for hh in range(2))

    def tile_update(hh, r0, q, k0, width, mask):
        c = heads[hh]
        s = _dot_nt(q, k_ref[pl.ds(k0, width), c])
        if mask is not None:
            s = jnp.where(mask, s, -jnp.inf)
        m = m_ref[hh, r0:, :]
        m_new = jnp.maximum(m, jnp.max(s, axis=-1, keepdims=True))
        p = jnp.exp2(s - jnp.concatenate([m_new] * (width // LANES), axis=1)).astype(BF16)
        acc_ref[hh, r0:, :] = jnp.exp2(m - m_new) * acc_ref[hh, r0:, :] + _dot(p, v_ref[pl.ds(k0, width), c])
        m_ref[hh, r0:, :] = m_new

    def q_tile(qi, _):
        q0 = pl.multiple_of(qi * tq, tq)
        qs = [q_ref[pl.ds(q0, tq), c] for c in heads]
        m_ref[...] = jnp.full(m_ref.shape, -jnp.inf, F32)
        acc_ref[...] = jnp.zeros(acc_ref.shape, F32)

        def full_tile(kb, _):
            k0 = pl.multiple_of(kb * tq, tq)
            for hh in range(2):
                tile_update(hh, 0, qs[hh], k0, tq, None)
            return 0

        lax.fori_loop(0, qi, full_tile, 0)
        for d in range(ATTN_Q_SUB):
            k0 = pl.multiple_of(q0 + d * t, t)
            for hh in range(2):
                tile_update(hh, d * t, qs[hh][d * t:], k0, t, causal[d])
        acc_a, acc_b = acc_ref[0], acc_ref[1]
        l_a = acc_a[:, HALF:HALF + 1]
        l_b = acc_b[:, 0:1]
        out = jnp.where(lane < HALF, acc_a * (1.0 / l_a), acc_b * (1.0 / l_b))
        o_ref[pl.ds(q0, tq), :] = out.astype(BF16)
        return 0

    lax.fori_loop(0, seq // tq, q_tile, 0)


def _mla_attn(q, k, v, batch, seq):
    pairs = MLA_HEADS // 2
    pw = 2 * HEAD_PAD
    spec = pl.BlockSpec((seq, pw), lambda b, p: (b, p))
    return pl.pallas_call(
        _mla_attn_kernel,
        out_shape=jax.ShapeDtypeStruct((batch * seq, MLA_HEADS * MLA_V), BF16),
        grid=(batch, pairs),
        in_specs=[spec, spec, spec],
        out_specs=pl.BlockSpec((seq, LANES), lambda b, p: (b, p)),
        scratch_shapes=[pltpu.VMEM((2, ATTN_Q_SUB * ATTN_TILE, LANES), F32)] * 2,
        compiler_params=_params("parallel", "parallel"),
        name="mla_attn",
    )(q, k, v)


def _mem_kv_kernel(mem_ref, wk_ref, wv_ref, k_ref, v_ref):
    m = mem_ref[...].astype(BF16)
    k_ref[...] = _dot(m, wk_ref[...]).astype(BF16)
    v_ref[...] = _dot(m, wv_ref[...]).astype(BF16)


def _mem_kv(mem2, wk, wv, batch, mem_tokens):
    d = mem2.shape[1]
    spec = pl.BlockSpec((mem_tokens, d), lambda b: (b, 0))
    return pl.pallas_call(
        _mem_kv_kernel,
        out_shape=(jax.ShapeDtypeStruct(mem2.shape, BF16),) * 2,
        grid=(batch,),
        in_specs=[spec, _full(wk.shape), _full(wv.shape)],
        out_specs=(spec, spec),
        compiler_params=_params("parallel"),
        name="mem_kv",
    )(mem2, wk, wv)


def _mix_xattn_kernel(x_ref, y_ref, o_ref, km_ref, vm_ref, g0_ref, b0_ref, wout_ref, g1_ref, b1_ref,
                      wq_ref, wo_ref, g2_ref, b2_ref, h2_ref):
    d = x_ref.shape[1]
    hd_w = d // MEM_HEADS
    scale = hd_w ** -0.5
    sub = x_ref.shape[0] // ROW_CHAINS
    chains = [slice(r * sub, (r + 1) * sub) for r in range(ROW_CHAINS)]
    mix = [_dot(y_ref[rows, :], wout_ref[0:SSD_INNER, :]) + _dot(o_ref[rows, :], wout_ref[SSD_INNER:, :])
           for rows in chains]
    h1 = [_layer_norm(DEEPNORM_ALPHA * _layer_norm(x_ref[rows, :], g0_ref[...], b0_ref[...]) + mix[r],
                      g1_ref[...], b1_ref[...]) for r, rows in enumerate(chains)]
    h1b = [v.astype(BF16) for v in h1]
    xa = [None] * ROW_CHAINS
    for hd in range(MEM_HEADS):
        cols = slice(hd * hd_w, (hd + 1) * hd_w)
        q = [(_dot(v, wq_ref[:, cols]) * scale).astype(BF16) for v in h1b]
        s = [_dot_nt(v, km_ref[:, cols]) for v in q]
        p = [jnp.exp(v - jnp.max(v, axis=-1, keepdims=True)) for v in s]
        p = [(v * (1.0 / jnp.sum(v, axis=-1, keepdims=True))).astype(BF16) for v in p]
        oh = [_dot(v, vm_ref[:, cols]).astype(BF16) for v in p]
        for r in range(ROW_CHAINS):
            part = _dot(oh[r], wo_ref[cols, :])
            xa[r] = part if xa[r] is None else xa[r] + part
    for r, rows in enumerate(chains):
        h2_ref[rows, :] = _layer_norm(DEEPNORM_ALPHA * h1[r] + xa[r], g2_ref[...], b2_ref[...])


def _mix_xattn(x2, y, o, k_mem, v_mem, ln_in_g, ln_in_b, w_out, ln1_g, ln1_b, wq, wo, ln2_g, ln2_b,
               seq, mem_tokens):
    tokens, d = x2.shape
    tm = TOKEN_TILE
    per_batch = seq // tm

    def row(w):
        return pl.BlockSpec((tm, w), lambda i: (i, 0))

    mem_spec = pl.BlockSpec((mem_tokens, d), lambda i: (i // per_batch, 0))
    vec = _full((1, d))
    return pl.pallas_call(
        _mix_xattn_kernel,
        out_shape=jax.ShapeDtypeStruct((tokens, d), F32),
        grid=(tokens // tm,),
        in_specs=[row(d), row(y.shape[1]), row(o.shape[1]), mem_spec, mem_spec, vec, vec,
                  _full(w_out.shape), vec, vec, _full(wq.shape), _full(wo.shape), vec, vec],
        out_specs=row(d),
        compiler_params=_params("parallel"),
        name="mix_xattn",
    )(x2, y, o, k_mem, v_mem, ln_in_g, ln_in_b, w_out, ln1_g, ln1_b, wq, wo, ln2_g, ln2_b)


def _mlp_kernel(h_ref, wup_ref, wdn_ref, g_ref, b_ref, out_ref):
    sub = h_ref.shape[0] // MLP_CHAINS
    chains = [slice(r * sub, (r + 1) * sub) for r in range(MLP_CHAINS)]
    hb = [h_ref[rows, :].astype(BF16) for rows in chains]
    ff = [None] * MLP_CHAINS
    for c in range(wup_ref.shape[1] // FF_CHUNK):
        cols = slice(c * FF_CHUNK, (c + 1) * FF_CHUNK)
        for r in range(MLP_CHAINS):
            u = jnp.maximum(_dot(hb[r], wup_ref[:, cols]), 0.0)
            part = _dot((u * u).astype(BF16), wdn_ref[cols, :])
            ff[r] = part if ff[r] is None else ff[r] + part
    for r, rows in enumerate(chains):
        out_ref[rows, :] = _layer_norm(DEEPNORM_ALPHA * h_ref[rows, :] + ff[r], g_ref[...], b_ref[...])


def _mlp(h2, w_up, w_down, ln_g, ln_b):
    tokens, d = h2.shape
    tm = TOKEN_TILE
    row = pl.BlockSpec((tm, d), lambda i: (i, 0))
    return pl.pallas_call(
        _mlp_kernel,
        out_shape=jax.ShapeDtypeStruct((tokens, d), F32),
        grid=(tokens // tm,),
        in_specs=[row, _full(w_up.shape), _full(w_down.shape), _full((1, d)), _full((1, d))],
        out_specs=row,
        compiler_params=_params("parallel"),
        name="mlp",
    )(h2, w_up, w_down, ln_g, ln_b)


def _rot_cols(w):
    half = MLA_ROPE // 2
    return jnp.concatenate([-w[..., half:], w[..., :half]], axis=-1)


def _pack_in_proj(w_in):
    o = 0
    segs = {}
    for name, width in (("z", SSD_INNER), ("xbc", SSD_XBC), ("dt", SSD_HEADS), ("q", MLA_Q_RANK),
                        ("kv", MLA_KV_RANK), ("kr", MLA_ROPE)):
        segs[name] = w_in[:, o:o + width]
        o += width
    kr, krs = segs["kr"], _rot_cols(segs["kr"])
    dt = jnp.pad(segs["dt"], ((0, 0), (0, LANES - SSD_HEADS)))
    return jnp.concatenate([segs["z"], segs["xbc"], segs["q"], segs["kv"], kr, krs, kr, krs, dt],
                           axis=1).astype(BF16)


def _pack_q_up(w_q_up):
    r = w_q_up.shape[0]
    w = w_q_up.reshape(r, MLA_HEADS, MLA_QK)
    rope = w[:, :, MLA_NOPE:]
    return jnp.concatenate([w[:, :, :MLA_NOPE], rope, _rot_cols(rope)], axis=-1).reshape(
        r, MLA_HEADS * HEAD_PAD).astype(BF16)


def _pack_kv_up(w_kv_up):
    r = w_kv_up.shape[0]
    w = w_kv_up.reshape(r, MLA_HEADS, MLA_NOPE + MLA_V)
    zeros = jnp.zeros((r, MLA_HEADS, HEAD_PAD - MLA_NOPE), w.dtype)
    wk = jnp.concatenate([w[:, :, :MLA_NOPE], zeros], axis=-1)
    wv = w[:, :, MLA_NOPE:]
    zv = jnp.zeros_like(wv)
    even = (jnp.arange(MLA_HEADS) % 2 == 0)[None, :, None]
    wv = jnp.concatenate([jnp.where(even, wv, zv), jnp.where(even, zv, wv)], axis=-1)
    return jnp.concatenate([wk.reshape(r, -1), wv.reshape(r, -1)], axis=1).astype(BF16)


def kernel(x, mem, positions, ln_in_g, ln_in_b, w_in, conv_w, conv_b, dt_bias, a_log, d_skip, ssd_norm_g, q_norm_g, w_q_up, kv_norm_g, w_kv_up, w_mix_out, ln1_g, ln1_b, w_mem_q, w_mem_k, w_mem_v, w_mem_o, ln2_g, ln2_b, w_up, w_down, ln3_g, ln3_b):
    batch, seq, d = x.shape
    mem_tokens = mem.shape[1]
    tokens = batch * seq
    assert w_in.shape[0] == DEPTH == 1
    assert seq % (ATTN_TILE * ATTN_Q_SUB) == 0 and seq % TOKEN_TILE == 0 and seq % SSD_ROWS == 0

    def vec(v):
        return v.reshape(1, -1).astype(F32)

    def head_rows(v):
        return jnp.broadcast_to(v.astype(F32)[:, None], (v.size, LANES))

    x2 = x.reshape(tokens, d)

    half = MLA_ROPE // 2
    inv_freq = jnp.power(ROPE_THETA, -jnp.arange(half, dtype=F32) / half)
    invf = jnp.tile(inv_freq, LANES // half)[None, :]
    head_lane = jnp.arange(MLA_HEADS * HEAD_PAD) % HEAD_PAD
    head_odd = (jnp.arange(MLA_HEADS * HEAD_PAD) // HEAD_PAD) % 2
    vone = (head_lane == jnp.where(head_odd == 1, 0, HALF)).astype(F32)[None, :]

    z, xbc, dt, q, k, v = _in_proj(
        x2, positions.reshape(tokens, 1), invf, vone, vec(ln_in_g), vec(ln_in_b), _pack_in_proj(w_in[0]),
        vec(q_norm_g[0]), _pack_q_up(w_q_up[0]), vec(kv_norm_g[0]), _pack_kv_up(w_kv_up[0]))

    y = _ssd(xbc, dt, z, conv_w[0].astype(F32), vec(conv_b[0]), head_rows(dt_bias[0]), head_rows(a_log[0]),
             vec(jnp.repeat(d_skip[0], SSD_HEAD_DIM)), vec(ssd_norm_g[0]), batch, seq)
    o = _mla_attn(q, k, v, batch, seq)

    k_mem, v_mem = _mem_kv(mem.reshape(batch * mem_tokens, d), w_mem_k[0].astype(BF16),
                           w_mem_v[0].astype(BF16), batch, mem_tokens)
    h2 = _mix_xattn(x2, y, o, k_mem, v_mem, vec(ln_in_g), vec(ln_in_b), w_mix_out[0].astype(BF16),
                    vec(ln1_g[0]), vec(ln1_b[0]), w_mem_q[0].astype(BF16), w_mem_o[0].astype(BF16),
                    vec(ln2_g[0]), vec(ln2_b[0]), seq, mem_tokens)
    out = _mlp(h2, w_up[0].astype(BF16), w_down[0].astype(BF16), vec(ln3_g[0]), vec(ln3_b[0]))
    return out.reshape(batch, seq, d)
```

```python
import functools

import jax
import jax.numpy as jnp
from jax import lax
from jax.experimental import pallas as pl
from jax.experimental.pallas import tpu as pltpu

F32 = jnp.float32
BF16 = jnp.bfloat16

SSD_HEADS = 8
SSD_HEAD_DIM = 64
SSD_INNER = SSD_HEADS * SSD_HEAD_DIM
SSD_GROUPS = 2
SSD_STATE = 128
SSD_CONV = 4
SSD_CHUNK = 128
SSD_XBC = SSD_INNER + 2 * SSD_GROUPS * SSD_STATE
MLA_HEADS = 8
MLA_NOPE = 64
MLA_ROPE = 32
MLA_QK = MLA_NOPE + MLA_ROPE
MLA_V = 64
MLA_Q_RANK = 384
MLA_KV_RANK = 256
ROPE_THETA = 10000.0
MEM_HEADS = 4
LN_EPS = 1e-5
RMS_EPS = 1e-6
DEPTH = 1
DEEPNORM_ALPHA = (2.0 * DEPTH) ** 0.25

LANES = 128
SUBLANES = 8
VMEM_LIMIT_BYTES = 56 * 1024 * 1024

TOKEN_TILE = 1024
ROW_CHAINS = 4
SSD_ROWS = 512
SSD_CONV_ROWS = 256
ATTN_TILE = 512
ATTN_Q_SUB = 2
FF_CHUNK = 1024
MLP_CHAINS = 2

HEAD_PAD = LANES
HALF = LANES // 2


def _params(*semantics):
    return pltpu.CompilerParams(dimension_semantics=semantics, vmem_limit_bytes=VMEM_LIMIT_BYTES)


def _full(shape):
    zeros = (0,) * len(shape)
    return pl.BlockSpec(shape, lambda *_: zeros, pipeline_mode=pl.Buffered(1))


def _layer_norm(x, g, b):
    mu = jnp.mean(x, axis=-1, keepdims=True)
    xc = x - mu
    var = jnp.mean(xc * xc, axis=-1, keepdims=True)
    return xc * lax.rsqrt(var + LN_EPS) * g + b


def _rms_norm(x, g):
    ms = jnp.mean(x * x, axis=-1, keepdims=True)
    return x * lax.rsqrt(ms + RMS_EPS) * g


def _silu(x):
    hx = 0.5 * x
    return hx + hx * jnp.tanh(hx)


def _dot(a, b):
    return jnp.dot(a, b, preferred_element_type=F32)


def _dot_nt(a, b):
    return lax.dot_general(a, b, (((1,), (1,)), ((), ())), preferred_element_type=F32)


_C_Z = 0
_C_XBC = _C_Z + SSD_INNER
_C_QLAT = _C_XBC + SSD_XBC
_C_KVLAT = _C_QLAT + MLA_Q_RANK
_C_KR = _C_KVLAT + MLA_KV_RANK
_C_DT = _C_KR + LANES
_C_END = _C_DT + LANES

LOG2_E = 1.4426950408889634
MLA_Q_SCALE = MLA_QK ** -0.5 * LOG2_E


def _in_proj_kernel(x_ref, pos_ref, invf_ref, vone_ref, g_ref, b_ref, w1_ref, qg_ref, wq_ref, kvg_ref, wkv_ref,
                    h_ref, z_ref, xbc_ref, dt_ref, q_ref, k_ref, v_ref):
    sub = x_ref.shape[0] // ROW_CHAINS
    chains = [slice(r * sub, (r + 1) * sub) for r in range(ROW_CHAINS)]
    lane = lax.broadcasted_iota(jnp.int32, (sub, LANES), 1)
    kw = MLA_HEADS * HEAD_PAD

    hs = []
    for rows in chains:
        hf = _layer_norm(x_ref[rows, :], g_ref[...], b_ref[...])
        h_ref[rows, :] = hf
        hs.append(hf.astype(BF16))
    csq, csk, kr = [], [], []
    for h, rows in zip(hs, chains):
        z_ref[rows, :] = _dot(h, w1_ref[:, _C_Z:_C_XBC]).astype(BF16)
        xbc_ref[rows, :] = _dot(h, w1_ref[:, _C_XBC:_C_QLAT]).astype(BF16)
        kr_dt = _dot(h, w1_ref[:, _C_KR:_C_END])
        kr.append(kr_dt[:, :LANES])
        dt_ref[rows, :] = kr_dt[:, LANES:]
        ang = pos_ref[rows, :].astype(F32) * invf_ref[...]
        cos = jnp.cos(ang)
        sin = jnp.sin(ang)
        csq.append(MLA_Q_SCALE * jnp.where(lane < MLA_NOPE, 1.0, jnp.where(lane < MLA_NOPE + MLA_ROPE, cos, sin)))
        csk.append(jnp.where((lane // MLA_ROPE) % 2 == 0, cos, sin))
    qn =[_rms_norm(_dot(h, w1_ref[:, _C_QLAT:_C_KVLAT]), qg_ref[...]).astype(BF16) for h in hs]
    kvn = [_rms_norm(_dot(h, w1_ref[:, _C_KVLAT:_C_KR]), kvg_ref[...]).astype(BF16) for h in hs]

    for r, rows in enumerate(chains):
        q_all = _dot(qn[r], wq_ref[...])
        for hd in range(MLA_HEADS):
            cols = slice(hd * HEAD_PAD, (hd + 1) * HEAD_PAD)
            q_ref[rows, cols] = (q_all[:, cols] * csq[r]).astype(BF16)
        t = kr[r] * csk[r]
        kpe = jnp.where(lane >= MLA_NOPE, t + pltpu.roll(t, MLA_ROPE, 1), 0.0)
        kv_all = _dot(kvn[r], wkv_ref[...])
        for hd in range(MLA_HEADS):
            cols = slice(hd * HEAD_PAD, (hd + 1) * HEAD_PAD)
            k_ref[rows, cols] = (kv_all[:, cols] + kpe).astype(BF16)
        v_ref[rows, :] = (kv_all[:, kw:] + vone_ref[...]).astype(BF16)


def _in_proj(x2, pos, invf, vone, ln_g, ln_b, w1, q_norm_g, wq, kv_norm_g, wkv):
    tokens, d = x2.shape
    tm = TOKEN_TILE
    hw = MLA_HEADS * HEAD_PAD

    def row(w):
        return pl.BlockSpec((tm, w), lambda i: (i, 0))

    out_shape = (
        jax.ShapeDtypeStruct((tokens, d), F32),
        jax.ShapeDtypeStruct((tokens, SSD_INNER), BF16),
        jax.ShapeDtypeStruct((tokens, SSD_XBC), BF16),
        jax.ShapeDtypeStruct((tokens, LANES), F32),
        jax.ShapeDtypeStruct((tokens, hw), BF16),
        jax.ShapeDtypeStruct((tokens, hw), BF16),
        jax.ShapeDtypeStruct((tokens, hw), BF16),
    )
    return pl.pallas_call(
        _in_proj_kernel,
        out_shape=out_shape,
        grid=(tokens // tm,),
        in_specs=[row(d), row(1), _full((1, LANES)), _full((1, hw)), _full((1, d)), _full((1, d)),
                  _full(w1.shape), _full((1, MLA_Q_RANK)), _full(wq.shape), _full((1, MLA_KV_RANK)),
                  _full(wkv.shape)],
        out_specs=(row(d), row(SSD_INNER), row(SSD_XBC), row(LANES), row(hw), row(hw), row(hw)),
        compiler_params=_params("parallel"),
        name="in_proj",
    )(x2, pos, invf, vone, ln_g, ln_b, w1, q_norm_g, wq, kv_norm_g, wkv)


def _cumsum_lanes(x, tri):
    x1 = x.astype(BF16).astype(F32)
    r1 = x - x1
    x2 = r1.astype(BF16).astype(F32)
    x3 = r1 - x2
    parts = jnp.concatenate([x1, x2, x3, jnp.zeros_like(x)], axis=0).astype(BF16)
    sums = _dot(parts, tri)
    n = x.shape[0]
    return sums[0:n] + sums[n:2 * n] + sums[2 * n:3 * n]


def _ssd_kernel(xbc_ref, dt_ref, z_ref, shift_ref, tri_ref, cw_ref, cb_ref, dtb_ref, alog_ref, dskip_ref, ng_ref,
                y_ref, tail_ref, state_ref):
    rows = xbc_ref.shape[0]
    L = SSD_CHUNK
    pairs = SSD_HEADS // 2
    pairs_per_group = pairs // SSD_GROUPS

    @pl.when(pl.program_id(1) == 0)
    def _():
        tail_ref[...] = jnp.zeros(tail_ref.shape, F32)
        state_ref[...] = jnp.zeros(state_ref.shape, F32)

    a_head = -jnp.exp(alog_ref[...])
    scalars = []
    for c in range(rows // L):
        dt_in = dt_ref[c * L:(c + 1) * L, :].T[0:SSD_HEADS, :] + dtb_ref[...]
        dt_t = jnp.maximum(dt_in, 0.0) + jnp.log1p(jnp.exp(-jnp.abs(dt_in)))
        a_cs_t = _cumsum_lanes(dt_t * a_head, tri_ref[...])
        a_last = a_cs_t[:, L - 1:L]
        c_dec = jnp.exp(jnp.broadcast_to(a_last, a_cs_t.shape))
        w_end_t = dt_t * jnp.exp(a_last - a_cs_t)
        a_cs = jnp.concatenate([a_cs_t, jnp.zeros((L - SSD_HEADS, L), F32)], axis=0).T
        scalars.append((dt_t, a_cs_t, w_end_t, c_dec, a_cs))

    cr = shift_ref.shape[1]
    tail = tail_ref[...]
    row8 = lax.broadcasted_iota(jnp.int32, tail.shape, 0)
    xc_blocks = []
    for blk in range(rows // cr):
        sh = _dot(shift_ref[...], xbc_ref[blk * cr:(blk + 1) * cr, :])
        conv = cb_ref[...]
        for k in range(SSD_CONV):
            conv = conv + cw_ref[k:k + 1, :] * sh[k * cr:(k + 1) * cr, :]
        head_fix = jnp.zeros(tail.shape, F32)
        for k in range(SSD_CONV - 1):
            delay = SSD_CONV - 1 - k
            head_fix = head_fix + cw_ref[k:k + 1, :] * jnp.where(row8 < delay, pltpu.roll(tail, delay, 0), 0.0)
        conv = jnp.concatenate([conv[0:SUBLANES, :] + head_fix, conv[SUBLANES:, :]], axis=0)
        tail = sh[SSD_CONV * cr - SUBLANES:, :]
        xc_blocks.append(_silu(conv))
    tail_ref[...] = tail

    sq = (L, L)
    row_i = lax.broadcasted_iota(jnp.int32, sq, 0)
    col_i = lax.broadcasted_iota(jnp.int32, sq, 1)
    causal = row_i >= col_i
    left = col_i < HALF
    left_row = lax.broadcasted_iota(jnp.int32, (1, LANES), 1) < HALF

    states = [state_ref[pj] for pj in range(pairs)]
    for c in range(rows // L):
        r0 = c * L
        xc = xc_blocks[r0 // cr][r0 % cr:r0 % cr + L, :]
        xs = xc[:, 0:SSD_INNER]
        bm = xc[:, SSD_INNER:SSD_INNER + SSD_GROUPS * SSD_STATE]
        cm = xc[:, SSD_INNER + SSD_GROUPS * SSD_STATE:]
        dt_t, a_cs_t, w_end_t, c_dec, a_cs = scalars[c]

        y_pairs = []
        for g in range(SSD_GROUPS):
            b_g = bm[:, g * SSD_STATE:(g + 1) * SSD_STATE]
            c_g = cm[:, g * SSD_STATE:(g + 1) * SSD_STATE]
            b_gt = b_g.T
            cb = _dot(c_g.astype(BF16), b_gt.astype(BF16))
            for j in range(pairs_per_group):
                pj = g * pairs_per_group + j
                x_p = xs[:, pj * LANES:(pj + 1) * LANES]
                st = states[pj]
                y_p = x_p * dskip_ref[:, pj * LANES:(pj + 1) * LANES]
                new = None
                for hh in range(2):
                    hd = 2 * pj + hh
                    cs_l = jnp.broadcast_to(a_cs[:, hd:hd + 1], sq)
                    m = cb * jnp.where(causal, jnp.exp(cs_l - a_cs_t[hd:hd + 1, :]), 0.0) * dt_t[hd:hd + 1, :]
                    lhs = jnp.concatenate([m, c_g * jnp.exp(cs_l)], axis=1).astype(BF16)
                    keep = left if hh == 0 else jnp.logical_not(left)
                    x_h = jnp.where(keep, x_p, 0.0).astype(BF16)
                    rhs = jnp.concatenate([x_h, jnp.where(keep, st, 0.0).astype(BF16)], axis=0)
                    y_p = y_p + _dot(lhs, rhs)
                    part = _dot((b_gt * w_end_t[hd:hd + 1, :]).astype(BF16), x_h)
                    new = part if new is None else new + part
                dec = jnp.where(left_row, c_dec[2 * pj:2 * pj + 1, :], c_dec[2 * pj + 1:2 * pj + 2, :])
                states[pj] = st * dec + new
                y_pairs.append(y_p)

        gw = SSD_INNER // SSD_GROUPS
        zt = z_ref[r0:r0 + L, :].astype(F32)
        for g in range(SSD_GROUPS):
            y_g = jnp.concatenate(y_pairs[g * pairs_per_group:(g + 1) * pairs_per_group], axis=1)
            y_g = y_g * _silu(zt[:, g * gw:(g + 1) * gw])
            y_g = _rms_norm(y_g, ng_ref[:, g * gw:(g + 1) * gw])
            y_ref[r0:r0 + L, g * gw:(g + 1) * gw] = y_g.astype(BF16)
    for pj in range(pairs):
        state_ref[pj] = states[pj]


def _ssd(xbc, dt, z, conv_w, conv_b, dt_bias, a_log, d_skip, norm_g, batch, seq):
    rows = SSD_ROWS
    steps = seq // rows

    def tok(w):
        return pl.BlockSpec((rows, w), lambda b, c: (b * steps + c, 0))

    t_out = jnp.arange(SSD_CONV_ROWS)[:, None]
    t_in = jnp.arange(SSD_CONV_ROWS)[None, :]
    shift = jnp.concatenate([(t_out - t_in == SSD_CONV - 1 - k) for k in range(SSD_CONV)], axis=0).astype(BF16)
    tri = (jnp.arange(SSD_CHUNK)[:, None] <= jnp.arange(SSD_CHUNK)[None, :]).astype(BF16)

    pairs = SSD_HEADS // 2
    return pl.pallas_call(
        _ssd_kernel,
        out_shape=jax.ShapeDtypeStruct((batch * seq, SSD_INNER), BF16),
        grid=(batch, steps),
        in_specs=[tok(SSD_XBC), tok(LANES), tok(SSD_INNER), _full(shift.shape), _full(tri.shape),
                  _full(conv_w.shape), _full(conv_b.shape), _full(dt_bias.shape), _full(a_log.shape),
                  _full(d_skip.shape), _full(norm_g.shape)],
        out_specs=tok(SSD_INNER),
        scratch_shapes=[pltpu.VMEM((SUBLANES, SSD_XBC), F32),
                        pltpu.VMEM((pairs, SSD_STATE, LANES), F32)],
        compiler_params=_params("parallel", "arbitrary"),
        name="ssd",
    )(xbc, dt, z, shift, tri, conv_w, conv_b, dt_bias, a_log, d_skip, norm_g)


def _mla_attn_kernel(q_ref, k_ref, v_ref, o_ref, m_ref, acc_ref):
    seq = q_ref.shape[0]
    t = ATTN_TILE
    tq = ATTN_Q_SUB * t
    causal = [lax.broadcasted_iota(jnp.int32, (tq - d * t, t), 0) >= lax.broadcasted_iota(jnp.int32, (tq - d * t, t), 1)
              for d in range(ATTN_Q_SUB)]
    lane = lax.broadcasted_iota(jnp.int32, (tq, LANES), 1)
    heads = tuple(slice(hh * HEAD_PAD, (hh + 1) * HEAD_PAD) for hh in range(2))

    def tile_update(hh, r0, q, k0, width, mask):
        c = heads[hh]
        s = _dot_nt(q, k_ref[pl.ds(k0, width), c])
        if mask is not None:
            s = jnp.where(mask, s, -jnp.inf)
        m = m_ref[hh, r0:, :]
        m_new = jnp.maximum(m, jnp.max(s, axis=-1, keepdims=True))
        p = jnp.exp2(s - jnp.concatenate([m_new] * (width // LANES), axis=1)).astype(BF16)
        acc_ref[hh, r0:, :] = jnp.exp2(m - m_new) * acc_ref[hh, r0:, :] + _dot(p, v_ref[pl.ds(k0, width), c])
        m_ref[hh, r0:, :] = m_new

    def q_tile(qi, _):
        q0 = pl.multiple_of(qi * tq, tq)
        qs = [q_ref[pl.ds(q0, tq), c] for c in heads]
        m_ref[...] = jnp.full(m_ref.shape, -jnp.inf, F32)
        acc_ref[...] = jnp.zeros(acc_ref.shape, F32)

        def full_tile(kb, _):
            k0 = pl.multiple_of(kb * tq, tq)
            for hh in range(2):
                tile_update(hh, 0, qs[hh], k0, tq, None)
            return 0

        lax.fori_loop(0, qi, full_tile, 0)
        for d in range(ATTN_Q_SUB):
            k0 = pl.multiple_of(q0 + d * t, t)
            for hh in range(2):
                tile_update(hh, d * t, qs[hh][d * t:], k0, t, causal[d])
        acc_a, acc_b = acc_ref[0], acc_ref[1]
        l_a = acc_a[:, HALF:HALF + 1]
        l_b = acc_b[:, 0:1]
        out = jnp.where(lane < HALF, acc_a * (1.0 / l_a), acc_b * (1.0 / l_b))
        o_ref[pl.ds(q0, tq), :] = out.astype(BF16)
        return 0

    lax.fori_loop(0, seq // tq, q_tile, 0)


def _mla_attn(q, k, v, batch, seq):
    pairs = MLA_HEADS // 2
    pw = 2 * HEAD_PAD
    spec = pl.BlockSpec((seq, pw), lambda b, p: (b, p))
    return pl.pallas_call(
        _mla_attn_kernel,
        out_shape=jax.ShapeDtypeStruct((batch * seq, MLA_HEADS * MLA_V), BF16),
        grid=(batch, pairs),
        in_specs=[spec, spec, spec],
        out_specs=pl.BlockSpec((seq, LANES), lambda b, p: (b, p)),
        scratch_shapes=[pltpu.VMEM((2, ATTN_Q_SUB * ATTN_TILE, LANES), F32)] * 2,
        compiler_params=_params("parallel", "parallel"),
        name="mla_attn",
    )(q, k, v)


def _mem_kv_kernel(mem_ref, wk_ref, wv_ref, k_ref, v_ref):
    m = mem_ref[...].astype(BF16)
    k_ref[...] = _dot(m, wk_ref[...]).astype(BF16)
    v_ref[...] = _dot(m, wv_ref[...]).astype(BF16)


def _mem_kv(mem2, wk, wv, batch, mem_tokens):
    d = mem2.shape[1]
    spec = pl.BlockSpec((mem_tokens, d), lambda b: (b, 0))
    return pl.pallas_call(
        _mem_kv_kernel,
        out_shape=(jax.ShapeDtypeStruct(mem2.shape, BF16),) * 2,
        grid=(batch,),
        in_specs=[spec, _full(wk.shape), _full(wv.shape)],
        out_specs=(spec, spec),
        compiler_params=_params("parallel"),
        name="mem_kv",
    )(mem2, wk, wv)


def _mix_xattn_kernel(h_ref, y_ref, o_ref, km_ref, vm_ref, wout_ref, g1_ref, b1_ref,
                      wq_ref, wo_ref, g2_ref, b2_ref, h2_ref):
    d = h_ref.shape[1]
    hd_w = d // MEM_HEADS
    scale = hd_w ** -0.5
    sub = h_ref.shape[0] // ROW_CHAINS
    chains = [slice(r * sub, (r + 1) * sub) for r in range(ROW_CHAINS)]
    mix = [_dot(y_ref[rows, :], wout_ref[0:SSD_INNER, :]) + _dot(o_ref[rows, :], wout_ref[SSD_INNER:, :])
           for rows in chains]
    h1 = [_layer_norm(DEEPNORM_ALPHA * h_ref[rows, :] + mix[r], g1_ref[...], b1_ref[...])
          for r, rows in enumerate(chains)]
    h1b = [v.astype(BF16) for v in h1]
    xa = [None] * ROW_CHAINS
    for hd in range(MEM_HEADS):
        cols = slice(hd * hd_w, (hd + 1) * hd_w)
        q = [(_dot(v, wq_ref[:, cols]) * scale).astype(BF16) for v in h1b]
        s = [_dot_nt(v, km_ref[:, cols]) for v in q]
        p = [jnp.exp(v - jnp.max(v, axis=-1, keepdims=True)) for v in s]
        p = [(v * (1.0 / jnp.sum(v, axis=-1, keepdims=True))).astype(BF16) for v in p]
        oh = [_dot(v, vm_ref[:, cols]).astype(BF16) for v in p]
        for r in range(ROW_CHAINS):
            part = _dot(oh[r], wo_ref[cols, :])
            xa[r] = part if xa[r] is None else xa[r] + part
    for r, rows in enumerate(chains):
        h2_ref[rows, :] = _layer_norm(DEEPNORM_ALPHA * h1[r] + xa[r], g2_ref[...], b2_ref[...])


def _mix_xattn(h, y, o, k_mem, v_mem, w_out, ln1_g, ln1_b, wq, wo, ln2_g, ln2_b, seq, mem_tokens):
    tokens, d = h.shape
    tm = TOKEN_TILE
    per_batch = seq // tm

    def row(w):
        return pl.BlockSpec((tm, w), lambda i: (i, 0))

    mem_spec = pl.BlockSpec((mem_tokens, d), lambda i: (i // per_batch, 0))
    vec = _full((1, d))
    return pl.pallas_call(
        _mix_xattn_kernel,
        out_shape=jax.ShapeDtypeStruct((tokens, d), F32),
        grid=(tokens // tm,),
        in_specs=[row(d), row(y.shape[1]), row(o.shape[1]), mem_spec, mem_spec,
                  _full(w_out.shape), vec, vec, _full(wq.shape), _full(wo.shape), vec, vec],
        out_specs=row(d),
        compiler_params=_params("parallel"),
        name="mix_xattn",
    )(h, y, o, k_mem, v_mem, w_out, ln1_g, ln1_b, wq, wo, ln2_g, ln2_b)


def _mlp_kernel(h_ref, wup_ref, wdn_ref, g_ref, b_ref, out_ref):
    sub = h_ref.shape[0] // MLP_CHAINS
    chains = [slice(r * sub, (r + 1) * sub) for r in range(MLP_CHAINS)]
    hb = [h_ref[rows, :].astype(BF16) for rows in chains]
    ff = [None] * MLP_CHAINS
    for c in range(wup_ref.shape[1] // FF_CHUNK):
        cols = slice(c * FF_CHUNK, (c + 1) * FF_CHUNK)
        for r in range(MLP_CHAINS):
            u = jnp.maximum(_dot(hb[r], wup_ref[:, cols]), 0.0)
            part = _dot((u * u).astype(BF16), wdn_ref[cols, :])
            ff[r] = part if ff[r] is None else ff[r] + part
    for r, rows in enumerate(chains):
        out_ref[rows, :] = _layer_norm(DEEPNORM_ALPHA * h_ref[rows, :] + ff[r], g_ref[...], b_ref[...])


def _mlp(h2, w_up, w_down, ln_g, ln_b):
    tokens, d = h2.shape
    tm = TOKEN_TILE
    row = pl.BlockSpec((tm, d), lambda i: (i, 0))
    return pl.pallas_call(
        _mlp_kernel,
        out_shape=jax.ShapeDtypeStruct((tokens, d), F32),
        grid=(tokens // tm,),
        in_specs=[row, _full(w_up.shape), _full(w_down.shape), _full((1, d)), _full((1, d))],
        out_specs=row,
        compiler_params=_params("parallel"),
        name="mlp",
    )(h2, w_up, w_down, ln_g, ln_b)


def _rot_cols(w):
    half = MLA_ROPE // 2
    return jnp.concatenate([-w[..., half:], w[..., :half]], axis=-1)


def _pack_in_proj(w_in):
    o = 0
    segs = {}
    for name, width in (("z", SSD_INNER), ("xbc", SSD_XBC), ("dt", SSD_HEADS), ("q", MLA_Q_RANK),
                        ("kv", MLA_KV_RANK), ("kr", MLA_ROPE)):
        segs[name] = w_in[:, o:o + width]
        o += width
    kr, krs = segs["kr"], _rot_cols(segs["kr"])
    dt = jnp.pad(segs["dt"], ((0, 0), (0, LANES - SSD_HEADS)))
    return jnp.concatenate([segs["z"], segs["xbc"], segs["q"], segs["kv"], kr, krs, kr, krs, dt],
                           axis=1).astype(BF16)


def _pack_q_up(w_q_up):
    r = w_q_up.shape[0]
    w = w_q_up.reshape(r, MLA_HEADS, MLA_QK)
    rope = w[:, :, MLA_NOPE:]
    return jnp.concatenate([w[:, :, :MLA_NOPE], rope, _rot_cols(rope)], axis=-1).reshape(
        r, MLA_HEADS * HEAD_PAD).astype(BF16)


def _pack_kv_up(w_kv_up):
    r = w_kv_up.shape[0]
    w = w_kv_up.reshape(r, MLA_HEADS, MLA_NOPE + MLA_V)
    zeros = jnp.zeros((r, MLA_HEADS, HEAD_PAD - MLA_NOPE), w.dtype)
    wk = jnp.concatenate([w[:, :, :MLA_NOPE], zeros], axis=-1)
    wv = w[:, :, MLA_NOPE:]
    zv = jnp.zeros_like(wv)
    even = (jnp.arange(MLA_HEADS) % 2 == 0)[None, :, None]
    wv = jnp.concatenate([jnp.where(even, wv, zv), jnp.where(even, zv, wv)], axis=-1)
    return jnp.concatenate([wk.reshape(r, -1), wv.reshape(r, -1)], axis=1).astype(BF16)


def kernel(x, mem, positions, ln_in_g, ln_in_b, w_in, conv_w, conv_b, dt_bias, a_log, d_skip, ssd_norm_g, q_norm_g, w_q_up, kv_norm_g, w_kv_up, w_mix_out, ln1_g, ln1_b, w_mem_q, w_mem_k, w_mem_v, w_mem_o, ln2_g, ln2_b, w_up, w_down, ln3_g, ln3_b):
    batch, seq, d = x.shape
    mem_tokens = mem.shape[1]
    tokens = batch * seq
    assert w_in.shape[0] == DEPTH == 1
    assert seq % (ATTN_TILE * ATTN_Q_SUB) == 0 and seq % TOKEN_TILE == 0 and seq % SSD_ROWS == 0

    def vec(v):
        return v.reshape(1, -1).astype(F32)

    def head_rows(v):
        return jnp.broadcast_to(v.astype(F32)[:, None], (v.size, LANES))

    x2 = x.reshape(tokens, d)

    half = MLA_ROPE // 2
    inv_freq = jnp.power(ROPE_THETA, -jnp.arange(half, dtype=F32) / half)
    invf = jnp.tile(inv_freq, LANES // half)[None, :]
    head_lane = jnp.arange(MLA_HEADS * HEAD_PAD) % HEAD_PAD
    head_odd = (jnp.arange(MLA_HEADS * HEAD_PAD) // HEAD_PAD) % 2
    vone = (head_lane == jnp.where(head_odd == 1, 0, HALF)).astype(F32)[None, :]

    h, z, xbc, dt, q, k, v = _in_proj(
        x2, positions.reshape(tokens, 1), invf, vone, vec(ln_in_g), vec(ln_in_b), _pack_in_proj(w_in[0]),
        vec(q_norm_g[0]), _pack_q_up(w_q_up[0]), vec(kv_norm_g[0]), _pack_kv_up(w_kv_up[0]))

    y = _ssd(xbc, dt, z, conv_w[0].astype(F32), vec(conv_b[0]), head_rows(dt_bias[0]), head_rows(a_log[0]),
             vec(jnp.repeat(d_skip[0], SSD_HEAD_DIM)), vec(ssd_norm_g[0]), batch, seq)
    o = _mla_attn(q, k, v, batch, seq)

    k_mem, v_mem = _mem_kv(mem.reshape(batch * mem_tokens, d), w_mem_k[0].astype(BF16),
                           w_mem_v[0].astype(BF16), batch, mem_tokens)
    h2 = _mix_xattn(h, y, o, k_mem, v_mem, w_mix_out[0].astype(BF16),
                    vec(ln1_g[0]), vec(ln1_b[0]), w_mem_q[0].astype(BF16), w_mem_o[0].astype(BF16),
                    vec(ln2_g[0]), vec(ln2_b[0]), seq, mem_tokens)
    out = _mlp(h2, w_up[0].astype(BF16), w_down[0].astype(BF16), vec(ln3_g[0]), vec(ln3_b[0]))
    return out.reshape(batch, seq, d)
```

```python
import functools

import jax
import jax.numpy as jnp
from jax import lax
from jax.experimental import pallas as pl
from jax.experimental.pallas import tpu as pltpu

F32 = jnp.float32
BF16 = jnp.bfloat16

SSD_HEADS = 8
SSD_HEAD_DIM = 64
SSD_INNER = SSD_HEADS * SSD_HEAD_DIM
SSD_GROUPS = 2
SSD_STATE = 128
SSD_CONV = 4
SSD_CHUNK = 128
SSD_XBC = SSD_INNER + 2 * SSD_GROUPS * SSD_STATE
MLA_HEADS = 8
MLA_NOPE = 64
MLA_ROPE = 32
MLA_QK = MLA_NOPE + MLA_ROPE
MLA_V = 64
MLA_Q_RANK = 384
MLA_KV_RANK = 256
ROPE_THETA = 10000.0
MEM_HEADS = 4
LN_EPS = 1e-5
RMS_EPS = 1e-6
DEPTH = 1
DEEPNORM_ALPHA = (2.0 * DEPTH) ** 0.25

LANES = 128
SUBLANES = 8
VMEM_LIMIT_BYTES = 56 * 1024 * 1024

TOKEN_TILE = 1024
ROW_CHAINS = 4
SSD_ROWS = 1024
SSD_CONV_ROWS = 256
ATTN_TILE = 512
ATTN_Q_SUB = 2
FF_CHUNK = 1024
MLP_CHAINS = 4

HEAD_PAD = LANES
HALF = LANES // 2


def _params(*semantics):
    return pltpu.CompilerParams(dimension_semantics=semantics, vmem_limit_bytes=VMEM_LIMIT_BYTES)


def _full(shape):
    zeros = (0,) * len(shape)
    return pl.BlockSpec(shape, lambda *_: zeros, pipeline_mode=pl.Buffered(1))


def _layer_norm(x, g, b):
    mu = jnp.mean(x, axis=-1, keepdims=True)
    xc = x - mu
    var = jnp.mean(xc * xc, axis=-1, keepdims=True)
    return xc * lax.rsqrt(var + LN_EPS) * g + b


def _rms_norm(x, g):
    ms = jnp.mean(x * x, axis=-1, keepdims=True)
    return x * lax.rsqrt(ms + RMS_EPS) * g


def _silu(x):
    hx = 0.5 * x
    return hx + hx * jnp.tanh(hx)


def _dot(a, b):
    return jnp.dot(a, b, preferred_element_type=F32)


def _dot_nt(a, b):
    return lax.dot_general(a, b, (((1,), (1,)), ((), ())), preferred_element_type=F32)


_C_Z = 0
_C_XBC = _C_Z + SSD_INNER
_C_QLAT = _C_XBC + SSD_XBC
_C_KVLAT = _C_QLAT + MLA_Q_RANK
_C_KR = _C_KVLAT + MLA_KV_RANK
_C_END = _C_KR + LANES
_KR_DT_LANE = 2 * MLA_ROPE

LOG2_E = 1.4426950408889634
MLA_Q_SCALE = MLA_QK ** -0.5 * LOG2_E


def _in_proj_kernel(x_ref, pos_ref, invf_ref, vone_ref, g_ref, b_ref, w1_ref, qg_ref, wq_ref, kvg_ref, wkv_ref,
                    h_ref, z_ref, xbc_ref, dt_ref, q_ref, k_ref, v_ref):
    sub = x_ref.shape[0] // ROW_CHAINS
    chains = [slice(r * sub, (r + 1) * sub) for r in range(ROW_CHAINS)]
    lane = lax.broadcasted_iota(jnp.int32, (sub, LANES), 1)
    kw = MLA_HEADS * HEAD_PAD

    hs = []
    for rows in chains:
        hf = _layer_norm(x_ref[rows, :], g_ref[...], b_ref[...])
        h_ref[rows, :] = hf
        hs.append(hf.astype(BF16))
    q_lat, kv_lat, kr = [], [], []
    for h, rows in zip(hs, chains):
        lat = _dot(h, w1_ref[:, _C_QLAT:_C_END])
        q_lat.append(lat[:, :MLA_Q_RANK])
        kv_lat.append(lat[:, MLA_Q_RANK:MLA_Q_RANK + MLA_KV_RANK])
        kr.append(lat[:, MLA_Q_RANK + MLA_KV_RANK:])
        dt_ref[rows, :] = kr[-1]
    csq, csk = [], []
    for h, rows in zip(hs, chains):
        z_ref[rows, :] = _dot(h, w1_ref[:, _C_Z:_C_XBC]).astype(BF16)
        xbc_ref[rows, :] = _dot(h, w1_ref[:, _C_XBC:_C_QLAT]).astype(BF16)
        ang = pos_ref[rows, :].astype(F32) * invf_ref[...]
        cos = jnp.cos(ang)
        sin = jnp.sin(ang)
        csq.append(MLA_Q_SCALE * jnp.where(lane < MLA_NOPE, 1.0, jnp.where(lane < MLA_NOPE + MLA_ROPE, cos, sin)))
        csk.append(jnp.where(lane < MLA_ROPE, cos, jnp.where(lane < 2 * MLA_ROPE, sin, 0.0)))
    qn = [_rms_norm(v, qg_ref[...]).astype(BF16) for v in q_lat]
    kvn = [_rms_norm(v, kvg_ref[...]).astype(BF16) for v in kv_lat]

    for r, rows in enumerate(chains):
        q_all = _dot(qn[r], wq_ref[...])
        for hd in range(MLA_HEADS):
            cols = slice(hd * HEAD_PAD, (hd + 1) * HEAD_PAD)
            q_ref[rows, cols] = (q_all[:, cols] * csq[r]).astype(BF16)
        t = kr[r] * csk[r]
        rk = jnp.where((lane >= MLA_ROPE) & (lane < 2 * MLA_ROPE), t + pltpu.roll(t, MLA_ROPE, 1), 0.0)
        kpe = pltpu.roll(rk, MLA_ROPE, 1) + pltpu.roll(rk, 2 * MLA_ROPE, 1)
        kv_all = _dot(kvn[r], wkv_ref[...])
        for hd in range(MLA_HEADS):
            cols = slice(hd * HEAD_PAD, (hd + 1) * HEAD_PAD)
            k_ref[rows, cols] = (kv_all[:, cols] + kpe).astype(BF16)
        v_ref[rows, :] = (kv_all[:, kw:] + vone_ref[...]).astype(BF16)


def _in_proj(x2, pos, invf, vone, ln_g, ln_b, w1, q_norm_g, wq, kv_norm_g, wkv):
    tokens, d = x2.shape
    tm = TOKEN_TILE
    hw = MLA_HEADS * HEAD_PAD

    def row(w):
        return pl.BlockSpec((tm, w), lambda i: (i, 0))

    out_shape = (
        jax.ShapeDtypeStruct((tokens, d), F32),
        jax.ShapeDtypeStruct((tokens, SSD_INNER), BF16),
        jax.ShapeDtypeStruct((tokens, SSD_XBC), BF16),
        jax.ShapeDtypeStruct((tokens, LANES), F32),
        jax.ShapeDtypeStruct((tokens, hw), BF16),
        jax.ShapeDtypeStruct((tokens, hw), BF16),
        jax.ShapeDtypeStruct((tokens, hw), BF16),
    )
    return pl.pallas_call(
        _in_proj_kernel,
        out_shape=out_shape,
        grid=(tokens // tm,),
        in_specs=[row(d), row(1), _full((1, LANES)), _full((1, hw)), _full((1, d)), _full((1, d)),
                  _full(w1.shape), _full((1, MLA_Q_RANK)), _full(wq.shape), _full((1, MLA_KV_RANK)),
                  _full(wkv.shape)],
        out_specs=(row(d), row(SSD_INNER), row(SSD_XBC), row(LANES), row(hw), row(hw), row(hw)),
        compiler_params=_params("parallel"),
        name="in_proj",
    )(x2, pos, invf, vone, ln_g, ln_b, w1, q_norm_g, wq, kv_norm_g, wkv)


def _cumsum_lanes(x, tri):
    x1 = x.astype(BF16).astype(F32)
    r1 = x - x1
    x2 = r1.astype(BF16).astype(F32)
    x3 = r1 - x2
    parts = jnp.concatenate([x1, x2, x3, jnp.zeros_like(x)], axis=0).astype(BF16)
    sums = _dot(parts, tri)
    n = x.shape[0]
    return sums[0:n] + sums[n:2 * n] + sums[2 * n:3 * n]


def _ssd_kernel(xbc_ref, dt_ref, z_ref, shift_ref, tri_ref, cw_ref, cb_ref, dtb_ref, alog_ref, dskip_ref, ng_ref,
                y_ref, tail_ref, state_ref):
    rows = xbc_ref.shape[0]
    L = SSD_CHUNK
    pairs = SSD_HEADS // 2
    pairs_per_group = pairs // SSD_GROUPS

    @pl.when(pl.program_id(1) == 0)
    def _():
        tail_ref[...] = jnp.zeros(tail_ref.shape, F32)
        state_ref[...] = jnp.zeros(state_ref.shape, F32)

    a_head = -jnp.exp(alog_ref[...])
    scalars = []
    for c in range(rows // L):
        dt_in = dt_ref[c * L:(c + 1) * L, :].T[_KR_DT_LANE:_KR_DT_LANE + SSD_HEADS, :] + dtb_ref[...]
        dt_t = jnp.maximum(dt_in, 0.0) + jnp.log1p(jnp.exp(-jnp.abs(dt_in)))
        a_cs_t = _cumsum_lanes(dt_t * a_head, tri_ref[...])
        a_last = a_cs_t[:, L - 1:L]
        c_dec = jnp.exp(jnp.broadcast_to(a_last, a_cs_t.shape))
        w_end_t = dt_t * jnp.exp(a_last - a_cs_t)
        a_cs = jnp.concatenate([a_cs_t, jnp.zeros((L - SSD_HEADS, L), F32)], axis=0).T
        scalars.append((dt_t, a_cs_t, w_end_t, c_dec, a_cs))

    cr = shift_ref.shape[1]
    tail = tail_ref[...]
    row8 = lax.broadcasted_iota(jnp.int32, tail.shape, 0)
    xc_blocks = []
    for blk in range(rows // cr):
        sh = _dot(shift_ref[...], xbc_ref[blk * cr:(blk + 1) * cr, :])
        conv = cb_ref[...]
        for k in range(SSD_CONV):
            conv = conv + cw_ref[k:k + 1, :] * sh[k * cr:(k + 1) * cr, :]
        head_fix = jnp.zeros(tail.shape, F32)
        for k in range(SSD_CONV - 1):
            delay = SSD_CONV - 1 - k
            head_fix = head_fix + cw_ref[k:k + 1, :] * jnp.where(row8 < delay, pltpu.roll(tail, delay, 0), 0.0)
        conv = jnp.concatenate([conv[0:SUBLANES, :] + head_fix, conv[SUBLANES:, :]], axis=0)
        tail = sh[SSD_CONV * cr - SUBLANES:, :]
        xc_blocks.append(_silu(conv))
    tail_ref[...] = tail

    sq = (L, L)
    row_i = lax.broadcasted_iota(jnp.int32, sq, 0)
    col_i = lax.broadcasted_iota(jnp.int32, sq, 1)
    causal = row_i >= col_i
    left = col_i < HALF
    left_row = lax.broadcasted_iota(jnp.int32, (1, LANES), 1) < HALF

    states = [state_ref[pj] for pj in range(pairs)]
    for c in range(rows // L):
        r0 = c * L
        xc = xc_blocks[r0 // cr][r0 % cr:r0 % cr + L, :]
        xs = xc[:, 0:SSD_INNER]
        bm = xc[:, SSD_INNER:SSD_INNER + SSD_GROUPS * SSD_STATE]
        cm = xc[:, SSD_INNER + SSD_GROUPS * SSD_STATE:]
        dt_t, a_cs_t, w_end_t, c_dec, a_cs = scalars[c]

        y_pairs = []
        for g in range(SSD_GROUPS):
            b_g = bm[:, g * SSD_STATE:(g + 1) * SSD_STATE]
            c_g = cm[:, g * SSD_STATE:(g + 1) * SSD_STATE]
            b_gt = b_g.T
            cb = _dot(c_g.astype(BF16), b_gt.astype(BF16))
            for j in range(pairs_per_group):
                pj = g * pairs_per_group + j
                x_p = xs[:, pj * LANES:(pj + 1) * LANES]
                st = states[pj]
                y_p = x_p * dskip_ref[:, pj * LANES:(pj + 1) * LANES]
                new = None
                for hh in range(2):
                    hd = 2 * pj + hh
                    cs_l = jnp.broadcast_to(a_cs[:, hd:hd + 1], sq)
                    m = cb * jnp.where(causal, jnp.exp(cs_l - a_cs_t[hd:hd + 1, :]), 0.0) * dt_t[hd:hd + 1, :]
                    lhs = jnp.concatenate([m, c_g * jnp.exp(cs_l)], axis=1).astype(BF16)
                    keep = left if hh == 0 else jnp.logical_not(left)
                    x_h = jnp.where(keep, x_p, 0.0).astype(BF16)
                    rhs = jnp.concatenate([x_h, jnp.where(keep, st, 0.0).astype(BF16)], axis=0)
                    y_p = y_p + _dot(lhs, rhs)
                    part = _dot((b_gt * w_end_t[hd:hd + 1, :]).astype(BF16), x_h)
                    new = part if new is None else new + part
                dec = jnp.where(left_row, c_dec[2 * pj:2 * pj + 1, :], c_dec[2 * pj + 1:2 * pj + 2, :])
                states[pj] = st * dec + new
                y_pairs.append(y_p)

        gw = SSD_INNER // SSD_GROUPS
        zt = z_ref[r0:r0 + L, :].astype(F32)
        for g in range(SSD_GROUPS):
            y_g = jnp.concatenate(y_pairs[g * pairs_per_group:(g + 1) * pairs_per_group], axis=1)
            y_g = y_g * _silu(zt[:, g * gw:(g + 1) * gw])
            y_g = _rms_norm(y_g, ng_ref[:, g * gw:(g + 1) * gw])
            y_ref[r0:r0 + L, g * gw:(g + 1) * gw] = y_g.astype(BF16)
    for pj in range(pairs):
        state_ref[pj] = states[pj]


def _ssd(xbc, dt, z, conv_w, conv_b, dt_bias, a_log, d_skip, norm_g, batch, seq):
    rows = SSD_ROWS
    steps = seq // rows

    def tok(w):
        return pl.BlockSpec((rows, w), lambda b, c: (b * steps + c, 0))

    t_out = jnp.arange(SSD_CONV_ROWS)[:, None]
    t_in = jnp.arange(SSD_CONV_ROWS)[None, :]
    shift = jnp.concatenate([(t_out - t_in == SSD_CONV - 1 - k) for k in range(SSD_CONV)], axis=0).astype(BF16)
    tri = (jnp.arange(SSD_CHUNK)[:, None] <= jnp.arange(SSD_CHUNK)[None, :]).astype(BF16)

    pairs = SSD_HEADS // 2
    return pl.pallas_call(
        _ssd_kernel,
        out_shape=jax.ShapeDtypeStruct((batch * seq, SSD_INNER), BF16),
        grid=(batch, steps),
        in_specs=[tok(SSD_XBC), tok(LANES), tok(SSD_INNER), _full(shift.shape), _full(tri.shape),
                  _full(conv_w.shape), _full(conv_b.shape), _full(dt_bias.shape), _full(a_log.shape),
                  _full(d_skip.shape), _full(norm_g.shape)],
        out_specs=tok(SSD_INNER),
        scratch_shapes=[pltpu.VMEM((SUBLANES, SSD_XBC), F32),
                        pltpu.VMEM((pairs, SSD_STATE, LANES), F32)],
        compiler_params=_params("parallel", "arbitrary"),
        name="ssd",
    )(xbc, dt, z, shift, tri, conv_w, conv_b, dt_bias, a_log, d_skip, norm_g)


def _mla_attn_kernel(q_ref, k_ref, v_ref, o_ref, m_ref, acc_ref):
    seq = q_ref.shape[0]
    t = ATTN_TILE
    tq = ATTN_Q_SUB * t
    causal = [lax.broadcasted_iota(jnp.int32, (tq - d * t, t), 0) >= lax.broadcasted_iota(jnp.int32, (tq - d * t, t), 1)
              for d in range(ATTN_Q_SUB)]
    lane = lax.broadcasted_iota(jnp.int32, (tq, LANES), 1)
    heads = tuple(slice(hh * HEAD_PAD, (hh + 1) * HEAD_PAD) for hh in range(2))

    def tile_update(hh, r0, q, k0, width, mask):
        c = heads[hh]
        s = _dot_nt(q, k_ref[pl.ds(k0, width), c])
        if mask is not None:
            s = jnp.where(mask, s, -jnp.inf)
        m = m_ref[hh, r0:, :]
        m_new = jnp.maximum(m, jnp.max(s, axis=-1, keepdims=True))
        p = jnp.exp2(s - jnp.concatenate([m_new] * (width // LANES), axis=1)).astype(BF16)
        acc_ref[hh, r0:, :] = jnp.exp2(m - m_new) * acc_ref[hh, r0:, :] + _dot(p, v_ref[pl.ds(k0, width), c])
        m_ref[hh, r0:, :] = m_new

    def q_tile(qi, _):
        q0 = pl.multiple_of(qi * tq, tq)
        qs = [q_ref[pl.ds(q0, tq), c] for c in heads]
        m_ref[...] = jnp.full(m_ref.shape, -jnp.inf, F32)
        acc_ref[...] = jnp.zeros(acc_ref.shape, F32)

        def full_tile(kb, _):
            k0 = pl.multiple_of(kb * tq, tq)
            for hh in range(2):
                tile_update(hh, 0, qs[hh], k0, tq, None)
            return 0

        lax.fori_loop(0, qi, full_tile, 0)
        for d in range(ATTN_Q_SUB):
            k0 = pl.multiple_of(q0 + d * t, t)
            for hh in range(2):
                tile_update(hh, d * t, qs[hh][d * t:], k0, t, causal[d])
        acc_a, acc_b = acc_ref[0], acc_ref[1]
        l_a = acc_a[:, HALF:HALF + 1]
        l_b = acc_b[:, 0:1]
        out = jnp.where(lane < HALF, acc_a * (1.0 / l_a), acc_b * (1.0 / l_b))
        o_ref[pl.ds(q0, tq), :] = out.astype(BF16)
        return 0

    lax.fori_loop(0, seq // tq, q_tile, 0)


def _mla_attn(q, k, v, batch, seq):
    pairs = MLA_HEADS // 2
    pw = 2 * HEAD_PAD
    spec = pl.BlockSpec((seq, pw), lambda b, p: (b, p))
    return pl.pallas_call(
        _mla_attn_kernel,
        out_shape=jax.ShapeDtypeStruct((batch * seq, MLA_HEADS * MLA_V), BF16),
        grid=(batch, pairs),
        in_specs=[spec, spec, spec],
        out_specs=pl.BlockSpec((seq, LANES), lambda b, p: (b, p)),
        scratch_shapes=[pltpu.VMEM((2, ATTN_Q_SUB * ATTN_TILE, LANES), F32)] * 2,
        compiler_params=_params("parallel", "parallel"),
        name="mla_attn",
    )(q, k, v)


def _mem_kv_kernel(mem_ref, wk_ref, wv_ref, k_ref, v_ref):
    m = mem_ref[...].astype(BF16)
    k_ref[...] = _dot(m, wk_ref[...]).astype(BF16)
    v_ref[...] = _dot(m, wv_ref[...]).astype(BF16)


def _mem_kv(mem2, wk, wv, batch, mem_tokens):
    d = mem2.shape[1]
    spec = pl.BlockSpec((mem_tokens, d), lambda b: (b, 0))
    return pl.pallas_call(
        _mem_kv_kernel,
        out_shape=(jax.ShapeDtypeStruct(mem2.shape, BF16),) * 2,
        grid=(batch,),
        in_specs=[spec, _full(wk.shape), _full(wv.shape)],
        out_specs=(spec, spec),
        compiler_params=_params("parallel"),
        name="mem_kv",
    )(mem2, wk, wv)


def _mix_xattn_kernel(h_ref, y_ref, o_ref, km_ref, vm_ref, wout_ref, g1_ref, b1_ref,
                      wq_ref, wo_ref, g2_ref, b2_ref, h2_ref):
    d = h_ref.shape[1]
    hd_w = d // MEM_HEADS
    scale = hd_w ** -0.5
    sub = h_ref.shape[0] // ROW_CHAINS
    chains = [slice(r * sub, (r + 1) * sub) for r in range(ROW_CHAINS)]
    mix = [_dot(y_ref[rows, :], wout_ref[0:SSD_INNER, :]) + _dot(o_ref[rows, :], wout_ref[SSD_INNER:, :])
           for rows in chains]
    h1 = [_layer_norm(DEEPNORM_ALPHA * h_ref[rows, :] + mix[r], g1_ref[...], b1_ref[...])
          for r, rows in enumerate(chains)]
    h1b = [v.astype(BF16) for v in h1]
    xa = [None] * ROW_CHAINS
    for hd in range(MEM_HEADS):
        cols = slice(hd * hd_w, (hd + 1) * hd_w)
        q = [(_dot(v, wq_ref[:, cols]) * scale).astype(BF16) for v in h1b]
        s = [_dot_nt(v, km_ref[:, cols]) for v in q]
        p = [jnp.exp(v - jnp.max(v, axis=-1, keepdims=True)) for v in s]
        p = [(v * (1.0 / jnp.sum(v, axis=-1, keepdims=True))).astype(BF16) for v in p]
        oh = [_dot(v, vm_ref[:, cols]).astype(BF16) for v in p]
        for r in range(ROW_CHAINS):
            part = _dot(oh[r], wo_ref[cols, :])
            xa[r] = part if xa[r] is None else xa[r] + part
    for r, rows in enumerate(chains):
        h2_ref[rows, :] = _layer_norm(DEEPNORM_ALPHA * h1[r] + xa[r], g2_ref[...], b2_ref[...])


def _mix_xattn(h, y, o, k_mem, v_mem, w_out, ln1_g, ln1_b, wq, wo, ln2_g, ln2_b, seq, mem_tokens):
    tokens, d = h.shape
    tm = TOKEN_TILE
    per_batch = seq // tm

    def row(w):
        return pl.BlockSpec((tm, w), lambda i: (i, 0))

    mem_spec = pl.BlockSpec((mem_tokens, d), lambda i: (i // per_batch, 0))
    vec = _full((1, d))
    return pl.pallas_call(
        _mix_xattn_kernel,
        out_shape=jax.ShapeDtypeStruct((tokens, d), F32),
        grid=(tokens // tm,),
        in_specs=[row(d), row(y.shape[1]), row(o.shape[1]), mem_spec, mem_spec,
                  _full(w_out.shape), vec, vec, _full(wq.shape), _full(wo.shape), vec, vec],
        out_specs=row(d),
        compiler_params=_params("parallel"),
        name="mix_xattn",
    )(h, y, o, k_mem, v_mem, w_out, ln1_g, ln1_b, wq, wo, ln2_g, ln2_b)


def _mlp_kernel(h_ref, wup_ref, wdn_ref, g_ref, b_ref, out_ref):
    sub = h_ref.shape[0] // MLP_CHAINS
    chains = [slice(r * sub, (r + 1) * sub) for r in range(MLP_CHAINS)]
    hb = [h_ref[rows, :].astype(BF16) for rows in chains]
    ff = [None] * MLP_CHAINS
    for c in range(wup_ref.shape[1] // FF_CHUNK):
        cols = slice(c * FF_CHUNK, (c + 1) * FF_CHUNK)
        for r in range(MLP_CHAINS):
            u = jnp.maximum(_dot(hb[r], wup_ref[:, cols]), 0.0)
            part = _dot((u * u).astype(BF16), wdn_ref[cols, :])
            ff[r] = part if ff[r] is None else ff[r] + part
    for r, rows in enumerate(chains):
        out_ref[rows, :] = _layer_norm(DEEPNORM_ALPHA * h_ref[rows, :] + ff[r], g_ref[...], b_ref[...])


def _mlp(h2, w_up, w_down, ln_g, ln_b):
    tokens, d = h2.shape
    tm = TOKEN_TILE
    row = pl.BlockSpec((tm, d), lambda i: (i, 0))
    return pl.pallas_call(
        _mlp_kernel,
        out_shape=jax.ShapeDtypeStruct((tokens, d), F32),
        grid=(tokens // tm,),
        in_specs=[row, _full(w_up.shape), _full(w_down.shape), _full((1, d)), _full((1, d))],
        out_specs=row,
        compiler_params=_params("parallel"),
        name="mlp",
    )(h2, w_up, w_down, ln_g, ln_b)


def _rot_cols(w):
    half = MLA_ROPE // 2
    return jnp.concatenate([-w[..., half:], w[..., :half]], axis=-1)


def _pack_in_proj(w_in):
    o = 0
    segs = {}
    for name, width in (("z", SSD_INNER), ("xbc", SSD_XBC), ("dt", SSD_HEADS), ("q", MLA_Q_RANK),
                        ("kv", MLA_KV_RANK), ("kr", MLA_ROPE)):
        segs[name] = w_in[:, o:o + width]
        o += width
    kr, krs = segs["kr"], _rot_cols(segs["kr"])
    dt = jnp.pad(segs["dt"], ((0, 0), (0, LANES - _KR_DT_LANE - SSD_HEADS)))
    return jnp.concatenate([segs["z"], segs["xbc"], segs["q"], segs["kv"], kr, krs, dt], axis=1).astype(BF16)


def _pack_q_up(w_q_up):
    r = w_q_up.shape[0]
    w = w_q_up.reshape(r, MLA_HEADS, MLA_QK)
    rope = w[:, :, MLA_NOPE:]
    return jnp.concatenate([w[:, :, :MLA_NOPE], rope, _rot_cols(rope)], axis=-1).reshape(
        r, MLA_HEADS * HEAD_PAD).astype(BF16)


def _pack_kv_up(w_kv_up):
    r = w_kv_up.shape[0]
    w = w_kv_up.reshape(r, MLA_HEADS, MLA_NOPE + MLA_V)
    zeros = jnp.zeros((r, MLA_HEADS, HEAD_PAD - MLA_NOPE), w.dtype)
    wk = jnp.concatenate([w[:, :, :MLA_NOPE], zeros], axis=-1)
    wv = w[:, :, MLA_NOPE:]
    zv = jnp.zeros_like(wv)
    even = (jnp.arange(MLA_HEADS) % 2 == 0)[None, :, None]
    wv = jnp.concatenate([jnp.where(even, wv, zv), jnp.where(even, zv, wv)], axis=-1)
    return jnp.concatenate([wk.reshape(r, -1), wv.reshape(r, -1)], axis=1).astype(BF16)


def kernel(x, mem, positions, ln_in_g, ln_in_b, w_in, conv_w, conv_b, dt_bias, a_log, d_skip, ssd_norm_g, q_norm_g, w_q_up, kv_norm_g, w_kv_up, w_mix_out, ln1_g, ln1_b, w_mem_q, w_mem_k, w_mem_v, w_mem_o, ln2_g, ln2_b, w_up, w_down, ln3_g, ln3_b):
    batch, seq, d = x.shape
    mem_tokens = mem.shape[1]
    tokens = batch * seq
    assert w_in.shape[0] == DEPTH == 1
    assert seq % (ATTN_TILE * ATTN_Q_SUB) == 0 and seq % TOKEN_TILE == 0 and seq % SSD_ROWS == 0

    def vec(v):
        return v.reshape(1, -1).astype(F32)

    def head_rows(v):
        return jnp.broadcast_to(v.astype(F32)[:, None], (v.size, LANES))

    x2 = x.reshape(tokens, d)

    half = MLA_ROPE // 2
    inv_freq = jnp.power(ROPE_THETA, -jnp.arange(half, dtype=F32) / half)
    invf = jnp.tile(inv_freq, LANES // half)[None, :]
    head_lane = jnp.arange(MLA_HEADS * HEAD_PAD) % HEAD_PAD
    head_odd = (jnp.arange(MLA_HEADS * HEAD_PAD) // HEAD_PAD) % 2
    vone = (head_lane == jnp.where(head_odd == 1, 0, HALF)).astype(F32)[None, :]

    h, z, xbc, dt, q, k, v = _in_proj(
        x2, positions.reshape(tokens, 1), invf, vone, vec(ln_in_g), vec(ln_in_b), _pack_in_proj(w_in[0]),
        vec(q_norm_g[0]), _pack_q_up(w_q_up[0]), vec(kv_norm_g[0]), _pack_kv_up(w_kv_up[0]))

    y = _ssd(xbc, dt, z, conv_w[0].astype(F32), vec(conv_b[0]), head_rows(dt_bias[0]), head_rows(a_log[0]),
             vec(jnp.repeat(d_skip[0], SSD_HEAD_DIM)), vec(ssd_norm_g[0]), batch, seq)
    o = _mla_attn(q, k, v, batch, seq)

    k_mem, v_mem = _mem_kv(mem.reshape(batch * mem_tokens, d), w_mem_k[0].astype(BF16),
                           w_mem_v[0].astype(BF16), batch, mem_tokens)
    h2 = _mix_xattn(h, y, o, k_mem, v_mem, w_mix_out[0].astype(BF16),
                    vec(ln1_g[0]), vec(ln1_b[0]), w_mem_q[0].astype(BF16), w_mem_o[0].astype(BF16),
                    vec(ln2_g[0]), vec(ln2_b[0]), seq, mem_tokens)
    out = _mlp(h2, w_up[0].astype(BF16), w_down[0].astype(BF16), vec(ln3_g[0]), vec(ln3_b[0]))
    return out.reshape(batch, seq, d)
```

```python
import functools

import jax
import jax.numpy as jnp
from jax import lax
from jax.experimental import pallas as pl
from jax.experimental.pallas import tpu as pltpu

F32 = jnp.float32
BF16 = jnp.bfloat16

SSD_HEADS = 8
SSD_HEAD_DIM = 64
SSD_INNER = SSD_HEADS * SSD_HEAD_DIM
SSD_GROUPS = 2
SSD_STATE = 128
SSD_CONV = 4
SSD_CHUNK = 128
SSD_XBC = SSD_INNER + 2 * SSD_GROUPS * SSD_STATE
MLA_HEADS = 8
MLA_NOPE = 64
MLA_ROPE = 32
MLA_QK = MLA_NOPE + MLA_ROPE
MLA_V = 64
MLA_Q_RANK = 384
MLA_KV_RANK = 256
ROPE_THETA = 10000.0
MEM_HEADS = 4
LN_EPS = 1e-5
RMS_EPS = 1e-6
DEPTH = 1
DEEPNORM_ALPHA = (2.0 * DEPTH) ** 0.25

LANES = 128
SUBLANES = 8
VMEM_LIMIT_BYTES = 56 * 1024 * 1024

TOKEN_TILE = 1024
ROW_CHAINS = 4
SSD_ROWS = 1024
SSD_CONV_ROWS = 256
ATTN_TILE = 512
ATTN_Q_SUB = 2
FF_CHUNK = 1024
MLP_CHAINS = 4

HEAD_PAD = LANES
HALF = LANES // 2


def _params(*semantics):
    return pltpu.CompilerParams(dimension_semantics=semantics, vmem_limit_bytes=VMEM_LIMIT_BYTES)


def _full(shape):
    zeros = (0,) * len(shape)
    return pl.BlockSpec(shape, lambda *_: zeros, pipeline_mode=pl.Buffered(1))


def _layer_norm(x, g, b):
    mu = jnp.mean(x, axis=-1, keepdims=True)
    xc = x - mu
    var = jnp.mean(xc * xc, axis=-1, keepdims=True)
    return xc * lax.rsqrt(var + LN_EPS) * g + b


def _rms_norm(x, g):
    ms = jnp.mean(x * x, axis=-1, keepdims=True)
    return x * lax.rsqrt(ms + RMS_EPS) * g


def _silu(x):
    hx = 0.5 * x
    return hx + hx * jnp.tanh(hx)


def _dot(a, b):
    return jnp.dot(a, b, preferred_element_type=F32)


def _dot_nt(a, b):
    return lax.dot_general(a, b, (((1,), (1,)), ((), ())), preferred_element_type=F32)


_C_Z = 0
_C_XBC = _C_Z + SSD_INNER
_C_QLAT = _C_XBC + SSD_XBC
_C_KVLAT = _C_QLAT + MLA_Q_RANK
_C_KR = _C_KVLAT + MLA_KV_RANK
_C_END = _C_KR + LANES
_KR_DT_LANE = 2 * MLA_ROPE

LOG2_E = 1.4426950408889634
MLA_Q_SCALE = MLA_QK ** -0.5 * LOG2_E


def _in_proj_kernel(x_ref, pos_ref, invf_ref, vone_ref, g_ref, b_ref, w1_ref, qg_ref, wq_ref, kvg_ref, wkv_ref,
                    h_ref, z_ref, xbc_ref, dt_ref, q_ref, k_ref, v_ref):
    sub = x_ref.shape[0] // ROW_CHAINS
    chains = [slice(r * sub, (r + 1) * sub) for r in range(ROW_CHAINS)]
    lane = lax.broadcasted_iota(jnp.int32, (sub, LANES), 1)
    kw = MLA_HEADS * HEAD_PAD

    hs = []
    for rows in chains:
        hf = _layer_norm(x_ref[rows, :], g_ref[...], b_ref[...])
        h_ref[rows, :] = hf
        hs.append(hf.astype(BF16))
    q_lat, kv_lat, kr = [], [], []
    for h, rows in zip(hs, chains):
        lat = _dot(h, w1_ref[:, _C_QLAT:_C_END])
        q_lat.append(lat[:, :MLA_Q_RANK])
        kv_lat.append(lat[:, MLA_Q_RANK:MLA_Q_RANK + MLA_KV_RANK])
        kr.append(lat[:, MLA_Q_RANK + MLA_KV_RANK:])
        dt_ref[rows, :] = kr[-1]
    csq, csk = [], []
    for h, rows in zip(hs, chains):
        z_ref[rows, :] = _dot(h, w1_ref[:, _C_Z:_C_XBC]).astype(BF16)
        xbc_ref[rows, :] = _dot(h, w1_ref[:, _C_XBC:_C_QLAT]).astype(BF16)
        ang = pos_ref[rows, :].astype(F32) * invf_ref[...]
        cos = jnp.cos(ang)
        sin = jnp.sin(ang)
        csq.append(MLA_Q_SCALE * jnp.where(lane < MLA_NOPE, 1.0, jnp.where(lane < MLA_NOPE + MLA_ROPE, cos, sin)))
        csk.append(jnp.where(lane < MLA_ROPE, cos, jnp.where(lane < 2 * MLA_ROPE, sin, 0.0)))
    qn = [_rms_norm(v, qg_ref[...]).astype(BF16) for v in q_lat]
    kvn = [_rms_norm(v, kvg_ref[...]).astype(BF16) for v in kv_lat]

    for r, rows in enumerate(chains):
        q_all = _dot(qn[r], wq_ref[...])
        for hd in range(MLA_HEADS):
            cols = slice(hd * HEAD_PAD, (hd + 1) * HEAD_PAD)
            q_ref[rows, cols] = (q_all[:, cols] * csq[r]).astype(BF16)
        t = kr[r] * csk[r]
        rk = jnp.where((lane >= MLA_ROPE) & (lane < 2 * MLA_ROPE), t + pltpu.roll(t, MLA_ROPE, 1), 0.0)
        kpe = pltpu.roll(rk, MLA_ROPE, 1) + pltpu.roll(rk, 2 * MLA_ROPE, 1)
        kv_all = _dot(kvn[r], wkv_ref[...])
        for hd in range(MLA_HEADS):
            cols = slice(hd * HEAD_PAD, (hd + 1) * HEAD_PAD)
            k_ref[rows, cols] = (kv_all[:, cols] + kpe).astype(BF16)
        v_ref[rows, :] = (kv_all[:, kw:] + vone_ref[...]).astype(BF16)


def _in_proj(x2, pos, invf, vone, ln_g, ln_b, w1, q_norm_g, wq, kv_norm_g, wkv):
    tokens, d = x2.shape
    tm = TOKEN_TILE
    hw = MLA_HEADS * HEAD_PAD

    def row(w):
        return pl.BlockSpec((tm, w), lambda i: (i, 0))

    out_shape = (
        jax.ShapeDtypeStruct((tokens, d), F32),
        jax.ShapeDtypeStruct((tokens, SSD_INNER), BF16),
        jax.ShapeDtypeStruct((tokens, SSD_XBC), BF16),
        jax.ShapeDtypeStruct((tokens, LANES), F32),
        jax.ShapeDtypeStruct((tokens, hw), BF16),
        jax.ShapeDtypeStruct((tokens, hw), BF16),
        jax.ShapeDtypeStruct((tokens, hw), BF16),
    )
    return pl.pallas_call(
        _in_proj_kernel,
        out_shape=out_shape,
        grid=(tokens // tm,),
        in_specs=[row(d), row(1), _full((1, LANES)), _full((1, hw)), _full((1, d)), _full((1, d)),
                  _full(w1.shape), _full((1, MLA_Q_RANK)), _full(wq.shape), _full((1, MLA_KV_RANK)),
                  _full(wkv.shape)],
        out_specs=(row(d), row(SSD_INNER), row(SSD_XBC), row(LANES), row(hw), row(hw), row(hw)),
        compiler_params=_params("parallel"),
        name="in_proj",
    )(x2, pos, invf, vone, ln_g, ln_b, w1, q_norm_g, wq, kv_norm_g, wkv)


def _cumsum_lanes(x, tri):
    x1 = x.astype(BF16).astype(F32)
    r1 = x - x1
    x2 = r1.astype(BF16).astype(F32)
    x3 = r1 - x2
    parts = jnp.concatenate([x1, x2, x3, jnp.zeros_like(x)], axis=0).astype(BF16)
    sums = _dot(parts, tri)
    n = x.shape[0]
    return sums[0:n] + sums[n:2 * n] + sums[2 * n:3 * n]


def _ssd_kernel(xbc_ref, dt_ref, z_ref, shift_ref, tri_ref, cw_ref, cb_ref, dtb_ref, alog_ref, dskip_ref, ng_ref,
                y_ref, tail_ref, state_ref):
    rows = xbc_ref.shape[0]
    L = SSD_CHUNK
    pairs = SSD_HEADS // 2
    pairs_per_group = pairs // SSD_GROUPS

    @pl.when(pl.program_id(1) == 0)
    def _():
        tail_ref[...] = jnp.zeros(tail_ref.shape, F32)
        state_ref[...] = jnp.zeros(state_ref.shape, F32)

    a_head = -jnp.exp(alog_ref[...])
    scalars = []
    for c in range(rows // L):
        dt_in = dt_ref[c * L:(c + 1) * L, :].T[_KR_DT_LANE:_KR_DT_LANE + SSD_HEADS, :] + dtb_ref[...]
        dt_t = jnp.maximum(dt_in, 0.0) + jnp.log1p(jnp.exp(-jnp.abs(dt_in)))
        a_cs_t = _cumsum_lanes(dt_t * a_head, tri_ref[...])
        a_last = a_cs_t[:, L - 1:L]
        c_dec = jnp.exp(jnp.broadcast_to(a_last, a_cs_t.shape))
        w_end_t = dt_t * jnp.exp(a_last - a_cs_t)
        a_cs = jnp.concatenate([a_cs_t, jnp.zeros((L - SSD_HEADS, L), F32)], axis=0).T
        scalars.append((dt_t, a_cs_t, w_end_t, c_dec, a_cs))

    cr = shift_ref.shape[1]
    tail = tail_ref[...]
    row8 = lax.broadcasted_iota(jnp.int32, tail.shape, 0)
    xc_blocks = []
    for blk in range(rows // cr):
        sh = _dot(shift_ref[...], xbc_ref[blk * cr:(blk + 1) * cr, :])
        conv = cb_ref[...]
        for k in range(SSD_CONV):
            conv = conv + cw_ref[k:k + 1, :] * sh[k * cr:(k + 1) * cr, :]
        head_fix = jnp.zeros(tail.shape, F32)
        for k in range(SSD_CONV - 1):
            delay = SSD_CONV - 1 - k
            head_fix = head_fix + cw_ref[k:k + 1, :] * jnp.where(row8 < delay, pltpu.roll(tail, delay, 0), 0.0)
        conv = jnp.concatenate([conv[0:SUBLANES, :] + head_fix, conv[SUBLANES:, :]], axis=0)
        tail = sh[SSD_CONV * cr - SUBLANES:, :]
        xc_blocks.append(_silu(conv))
    tail_ref[...] = tail

    sq = (L, L)
    row_i = lax.broadcasted_iota(jnp.int32, sq, 0)
    col_i = lax.broadcasted_iota(jnp.int32, sq, 1)
    causal = row_i >= col_i
    left = col_i < HALF
    left_row = lax.broadcasted_iota(jnp.int32, (1, LANES), 1) < HALF

    states = [state_ref[pj] for pj in range(pairs)]
    for c in range(rows // L):
        r0 = c * L
        xc = xc_blocks[r0 // cr][r0 % cr:r0 % cr + L, :]
        xs = xc[:, 0:SSD_INNER]
        bm = xc[:, SSD_INNER:SSD_INNER + SSD_GROUPS * SSD_STATE]
        cm = xc[:, SSD_INNER + SSD_GROUPS * SSD_STATE:]
        dt_t, a_cs_t, w_end_t, c_dec, a_cs = scalars[c]

        y_pairs = []
        for g in range(SSD_GROUPS):
            b_g = bm[:, g * SSD_STATE:(g + 1) * SSD_STATE]
            c_g = cm[:, g * SSD_STATE:(g + 1) * SSD_STATE]
            b_gt = b_g.T
            cb = _dot(c_g.astype(BF16), b_gt.astype(BF16))
            for j in range(pairs_per_group):
                pj = g * pairs_per_group + j
                x_p = xs[:, pj * LANES:(pj + 1) * LANES]
                st = states[pj]
                y_p = x_p * dskip_ref[:, pj * LANES:(pj + 1) * LANES]
                new = None
                for hh in range(2):
                    hd = 2 * pj + hh
                    cs_l = jnp.broadcast_to(a_cs[:, hd:hd + 1], sq)
                    m = cb * jnp.where(causal, jnp.exp(cs_l - a_cs_t[hd:hd + 1, :]), 0.0) * dt_t[hd:hd + 1, :]
                    lhs = jnp.concatenate([m, c_g * jnp.exp(cs_l)], axis=1).astype(BF16)
                    keep = left if hh == 0 else jnp.logical_not(left)
                    x_h = jnp.where(keep, x_p, 0.0).astype(BF16)
                    rhs = jnp.concatenate([x_h, jnp.where(keep, st, 0.0).astype(BF16)], axis=0)
                    y_p = y_p + _dot(lhs, rhs)
                    part = _dot((b_gt * w_end_t[hd:hd + 1, :]).astype(BF16), x_h)
                    new = part if new is None else new + part
                dec = jnp.where(left_row, c_dec[2 * pj:2 * pj + 1, :], c_dec[2 * pj + 1:2 * pj + 2, :])
                states[pj] = st * dec + new
                y_pairs.append(y_p)

        gw = SSD_INNER // SSD_GROUPS
        zt = z_ref[r0:r0 + L, :].astype(F32)
        for g in range(SSD_GROUPS):
            y_g = jnp.concatenate(y_pairs[g * pairs_per_group:(g + 1) * pairs_per_group], axis=1)
            y_g = y_g * _silu(zt[:, g * gw:(g + 1) * gw])
            y_g = _rms_norm(y_g, ng_ref[:, g * gw:(g + 1) * gw])
            y_ref[r0:r0 + L, g * gw:(g + 1) * gw] = y_g.astype(BF16)
    for pj in range(pairs):
        state_ref[pj] = states[pj]


def _ssd(xbc, dt, z, conv_w, conv_b, dt_bias, a_log, d_skip, norm_g, batch, seq):
    rows = SSD_ROWS
    steps = seq // rows

    def tok(w):
        return pl.BlockSpec((rows, w), lambda b, c: (b * steps + c, 0))

    t_out = jnp.arange(SSD_CONV_ROWS)[:, None]
    t_in = jnp.arange(SSD_CONV_ROWS)[None, :]
    shift = jnp.concatenate([(t_out - t_in == SSD_CONV - 1 - k) for k in range(SSD_CONV)], axis=0).astype(BF16)
    tri = (jnp.arange(SSD_CHUNK)[:, None] <= jnp.arange(SSD_CHUNK)[None, :]).astype(BF16)

    pairs = SSD_HEADS // 2
    return pl.pallas_call(
        _ssd_kernel,
        out_shape=jax.ShapeDtypeStruct((batch * seq, SSD_INNER), BF16),
        grid=(batch, steps),
        in_specs=[tok(SSD_XBC), tok(LANES), tok(SSD_INNER), _full(shift.shape), _full(tri.shape),
                  _full(conv_w.shape), _full(conv_b.shape), _full(dt_bias.shape), _full(a_log.shape),
                  _full(d_skip.shape), _full(norm_g.shape)],
        out_specs=tok(SSD_INNER),
        scratch_shapes=[pltpu.VMEM((SUBLANES, SSD_XBC), F32),
                        pltpu.VMEM((pairs, SSD_STATE, LANES), F32)],
        compiler_params=_params("parallel", "arbitrary"),
        name="ssd",
    )(xbc, dt, z, shift, tri, conv_w, conv_b, dt_bias, a_log, d_skip, norm_g)


def _mla_attn_kernel(q_ref, k_ref, v_ref, o_ref, m_ref, acc_ref):
    seq = q_ref.shape[0]
    t = ATTN_TILE
    tq = ATTN_Q_SUB * t
    causal = [lax.broadcasted_iota(jnp.int32, (tq - d * t, t), 0) >= lax.broadcasted_iota(jnp.int32, (tq - d * t, t), 1)
              for d in range(ATTN_Q_SUB)]
    lane = lax.broadcasted_iota(jnp.int32, (tq, LANES), 1)
    heads = tuple(slice(hh * HEAD_PAD, (hh + 1) * HEAD_PAD) for hh in range(2))

    def tile_update(hh, r0, q, k0, width, mask, first=False):
        c = heads[hh]
        s = _dot_nt(q, k_ref[pl.ds(k0, width), c])
        if mask is not None:
            s = jnp.where(mask, s, -jnp.inf)
        m_new = jnp.max(s, axis=-1, keepdims=True)
        if first:
            m_new = jnp.broadcast_to(m_new, (s.shape[0], LANES))
        else:
            m = m_ref[hh, r0:, :]
            m_new = jnp.maximum(m, m_new)
        p = jnp.exp2(s - jnp.concatenate([m_new] * (width // LANES), axis=1)).astype(BF16)
        pv = _dot(p, v_ref[pl.ds(k0, width), c])
        acc_ref[hh, r0:, :] = pv if first else jnp.exp2(m - m_new) * acc_ref[hh, r0:, :] + pv
        m_ref[hh, r0:, :] = m_new

    def q_tile(qi, _):
        q0 = pl.multiple_of(qi * tq, tq)
        qs = [q_ref[pl.ds(q0, tq), c] for c in heads]
        for d in range(ATTN_Q_SUB):
            k0 = pl.multiple_of(q0 + d * t, t)
            for hh in range(2):
                tile_update(hh, d * t, qs[hh][d * t:], k0, t, causal[d], first=(d == 0))

        def full_tile(kb, _):
            k0 = pl.multiple_of(kb * tq, tq)
            for hh in range(2):
                tile_update(hh, 0, qs[hh], k0, tq, None)
            return 0

        lax.fori_loop(0, qi, full_tile, 0)
        acc_a, acc_b = acc_ref[0], acc_ref[1]
        l_a = acc_a[:, HALF:HALF + 1]
        l_b = acc_b[:, 0:1]
        out = jnp.where(lane < HALF, acc_a * (1.0 / l_a), acc_b * (1.0 / l_b))
        o_ref[pl.ds(q0, tq), :] = out.astype(BF16)
        return 0

    lax.fori_loop(0, seq // tq, q_tile, 0)


def _mla_attn(q, k, v, batch, seq):
    pairs = MLA_HEADS // 2
    pw = 2 * HEAD_PAD
    spec = pl.BlockSpec((seq, pw), lambda b, p: (b, p))
    return pl.pallas_call(
        _mla_attn_kernel,
        out_shape=jax.ShapeDtypeStruct((batch * seq, MLA_HEADS * MLA_V), BF16),
        grid=(batch, pairs),
        in_specs=[spec, spec, spec],
        out_specs=pl.BlockSpec((seq, LANES), lambda b, p: (b, p)),
        scratch_shapes=[pltpu.VMEM((2, ATTN_Q_SUB * ATTN_TILE, LANES), F32)] * 2,
        compiler_params=_params("parallel", "parallel"),
        name="mla_attn",
    )(q, k, v)


def _mem_kv_kernel(mem_ref, wk_ref, wv_ref, k_ref, v_ref):
    m = mem_ref[...].astype(BF16)
    k_ref[...] = _dot(m, wk_ref[...]).astype(BF16)
    v_ref[...] = _dot(m, wv_ref[...]).astype(BF16)


def _mem_kv(mem2, wk, wv, batch, mem_tokens):
    d = mem2.shape[1]
    spec = pl.BlockSpec((mem_tokens, d), lambda b: (b, 0))
    return pl.pallas_call(
        _mem_kv_kernel,
        out_shape=(jax.ShapeDtypeStruct(mem2.shape, BF16),) * 2,
        grid=(batch,),
        in_specs=[spec, _full(wk.shape), _full(wv.shape)],
        out_specs=(spec, spec),
        compiler_params=_params("parallel"),
        name="mem_kv",
    )(mem2, wk, wv)


def _mix_xattn_kernel(h_ref, y_ref, o_ref, km_ref, vm_ref, wout_ref, g1_ref, b1_ref,
                      wq_ref, wo_ref, pre2_ref):
    d = h_ref.shape[1]
    hd_w = d // MEM_HEADS
    scale = hd_w ** -0.5
    sub = h_ref.shape[0] // ROW_CHAINS
    chains = [slice(r * sub, (r + 1) * sub) for r in range(ROW_CHAINS)]
    mix = [_dot(y_ref[rows, :], wout_ref[0:SSD_INNER, :]) + _dot(o_ref[rows, :], wout_ref[SSD_INNER:, :])
           for rows in chains]
    h1 = [_layer_norm(DEEPNORM_ALPHA * h_ref[rows, :] + mix[r], g1_ref[...], b1_ref[...])
          for r, rows in enumerate(chains)]
    h1b = [v.astype(BF16) for v in h1]
    xa = [None] * ROW_CHAINS
    for hd in range(MEM_HEADS):
        cols = slice(hd * hd_w, (hd + 1) * hd_w)
        q = [(_dot(v, wq_ref[:, cols]) * scale).astype(BF16) for v in h1b]
        s = [_dot_nt(v, km_ref[:, cols]) for v in q]
        p = [jnp.exp(v - jnp.max(v, axis=-1, keepdims=True)) for v in s]
        p = [(v * (1.0 / jnp.sum(v, axis=-1, keepdims=True))).astype(BF16) for v in p]
        oh = [_dot(v, vm_ref[:, cols]).astype(BF16) for v in p]
        for r in range(ROW_CHAINS):
            part = _dot(oh[r], wo_ref[cols, :])
            xa[r] = part if xa[r] is None else xa[r] + part
    for r, rows in enumerate(chains):
        pre2_ref[rows, :] = DEEPNORM_ALPHA * h1[r] + xa[r]


def _mix_xattn(h, y, o, k_mem, v_mem, w_out, ln1_g, ln1_b, wq, wo, seq, mem_tokens):
    tokens, d = h.shape
    tm = TOKEN_TILE
    per_batch = seq // tm

    def row(w):
        return pl.BlockSpec((tm, w), lambda i: (i, 0))

    mem_spec = pl.BlockSpec((mem_tokens, d), lambda i: (i // per_batch, 0))
    vec = _full((1, d))
    return pl.pallas_call(
        _mix_xattn_kernel,
        out_shape=jax.ShapeDtypeStruct((tokens, d), F32),
        grid=(tokens // tm,),
        in_specs=[row(d), row(y.shape[1]), row(o.shape[1]), mem_spec, mem_spec,
                  _full(w_out.shape), vec, vec, _full(wq.shape), _full(wo.shape)],
        out_specs=row(d),
        compiler_params=_params("parallel"),
        name="mix_xattn",
    )(h, y, o, k_mem, v_mem, w_out, ln1_g, ln1_b, wq, wo)


def _mlp_kernel(pre2_ref, g2_ref, b2_ref, wup_ref, wdn_ref, g_ref, b_ref, out_ref):
    sub = pre2_ref.shape[0] // MLP_CHAINS
    chains = [slice(r * sub, (r + 1) * sub) for r in range(MLP_CHAINS)]
    h2 = [_layer_norm(pre2_ref[rows, :], g2_ref[...], b2_ref[...]) for rows in chains]
    hb = [v.astype(BF16) for v in h2]
    ff = [None] * MLP_CHAINS
    for c in range(wup_ref.shape[1] // FF_CHUNK):
        cols = slice(c * FF_CHUNK, (c + 1) * FF_CHUNK)
        for r in range(MLP_CHAINS):
            u = jnp.maximum(_dot(hb[r], wup_ref[:, cols]), 0.0)
            part = _dot((u * u).astype(BF16), wdn_ref[cols, :])
            ff[r] = part if ff[r] is None else ff[r] + part
    for r, rows in enumerate(chains):
        out_ref[rows, :] = _layer_norm(DEEPNORM_ALPHA * h2[r] + ff[r], g_ref[...], b_ref[...])


def _mlp(pre2, ln2_g, ln2_b, w_up, w_down, ln_g, ln_b):
    tokens, d = pre2.shape
    tm = TOKEN_TILE
    row = pl.BlockSpec((tm, d), lambda i: (i, 0))
    vec = _full((1, d))
    return pl.pallas_call(
        _mlp_kernel,
        out_shape=jax.ShapeDtypeStruct((tokens, d), F32),
        grid=(tokens // tm,),
        in_specs=[row, vec, vec, _full(w_up.shape), _full(w_down.shape), vec, vec],
        out_specs=row,
        compiler_params=_params("parallel"),
        name="mlp",
    )(pre2, ln2_g, ln2_b, w_up, w_down, ln_g, ln_b)


def _rot_cols(w):
    half = MLA_ROPE // 2
    return jnp.concatenate([-w[..., half:], w[..., :half]], axis=-1)


def _pack_in_proj(w_in):
    o = 0
    segs = {}
    for name, width in (("z", SSD_INNER), ("xbc", SSD_XBC), ("dt", SSD_HEADS), ("q", MLA_Q_RANK),
                        ("kv", MLA_KV_RANK), ("kr", MLA_ROPE)):
        segs[name] = w_in[:, o:o + width]
        o += width
    kr, krs = segs["kr"], _rot_cols(segs["kr"])
    dt = jnp.pad(segs["dt"], ((0, 0), (0, LANES - _KR_DT_LANE - SSD_HEADS)))
    return jnp.concatenate([segs["z"], segs["xbc"], segs["q"], segs["kv"], kr, krs, dt], axis=1).astype(BF16)


def _pack_q_up(w_q_up):
    r = w_q_up.shape[0]
    w = w_q_up.reshape(r, MLA_HEADS, MLA_QK)
    rope = w[:, :, MLA_NOPE:]
    return jnp.concatenate([w[:, :, :MLA_NOPE], rope, _rot_cols(rope)], axis=-1).reshape(
        r, MLA_HEADS * HEAD_PAD).astype(BF16)


def _pack_kv_up(w_kv_up):
    r = w_kv_up.shape[0]
    w = w_kv_up.reshape(r, MLA_HEADS, MLA_NOPE + MLA_V)
    zeros = jnp.zeros((r, MLA_HEADS, HEAD_PAD - MLA_NOPE), w.dtype)
    wk = jnp.concatenate([w[:, :, :MLA_NOPE], zeros], axis=-1)
    wv = w[:, :, MLA_NOPE:]
    zv = jnp.zeros_like(wv)
    even = (jnp.arange(MLA_HEADS) % 2 == 0)[None, :, None]
    wv = jnp.concatenate([jnp.where(even, wv, zv), jnp.where(even, zv, wv)], axis=-1)
    return jnp.concatenate([wk.reshape(r, -1), wv.reshape(r, -1)], axis=1).astype(BF16)


def kernel(x, mem, positions, ln_in_g, ln_in_b, w_in, conv_w, conv_b, dt_bias, a_log, d_skip, ssd_norm_g, q_norm_g, w_q_up, kv_norm_g, w_kv_up, w_mix_out, ln1_g, ln1_b, w_mem_q, w_mem_k, w_mem_v, w_mem_o, ln2_g, ln2_b, w_up, w_down, ln3_g, ln3_b):
    batch, seq, d = x.shape
    mem_tokens = mem.shape[1]
    tokens = batch * seq
    assert w_in.shape[0] == DEPTH == 1
    assert seq % (ATTN_TILE * ATTN_Q_SUB) == 0 and seq % TOKEN_TILE == 0 and seq % SSD_ROWS == 0

    def vec(v):
        return v.reshape(1, -1).astype(F32)

    def head_rows(v):
        return jnp.broadcast_to(v.astype(F32)[:, None], (v.size, LANES))

    x2 = x.reshape(tokens, d)

    half = MLA_ROPE // 2
    inv_freq = jnp.power(ROPE_THETA, -jnp.arange(half, dtype=F32) / half)
    invf = jnp.tile(inv_freq, LANES // half)[None, :]
    head_lane = jnp.arange(MLA_HEADS * HEAD_PAD) % HEAD_PAD
    head_odd = (jnp.arange(MLA_HEADS * HEAD_PAD) // HEAD_PAD) % 2
    vone = (head_lane == jnp.where(head_odd == 1, 0, HALF)).astype(F32)[None, :]

    h, z, xbc, dt, q, k, v = _in_proj(
        x2, positions.reshape(tokens, 1), invf, vone, vec(ln_in_g), vec(ln_in_b), _pack_in_proj(w_in[0]),
        vec(q_norm_g[0]), _pack_q_up(w_q_up[0]), vec(kv_norm_g[0]), _pack_kv_up(w_kv_up[0]))

    y = _ssd(xbc, dt, z, conv_w[0].astype(F32), vec(conv_b[0]), head_rows(dt_bias[0]), head_rows(a_log[0]),
             vec(jnp.repeat(d_skip[0], SSD_HEAD_DIM)), vec(ssd_norm_g[0]), batch, seq)
    o = _mla_attn(q, k, v, batch, seq)

    k_mem, v_mem = _mem_kv(mem.reshape(batch * mem_tokens, d), w_mem_k[0].astype(BF16),
                           w_mem_v[0].astype(BF16), batch, mem_tokens)
    pre2 = _mix_xattn(h, y, o, k_mem, v_mem, w_mix_out[0].astype(BF16),
                      vec(ln1_g[0]), vec(ln1_b[0]), w_mem_q[0].astype(BF16), w_mem_o[0].astype(BF16),
                      seq, mem_tokens)
    out = _mlp(pre2, vec(ln2_g[0]), vec(ln2_b[0]), w_up[0].astype(BF16), w_down[0].astype(BF16),
               vec(ln3_g[0]), vec(ln3_b[0]))
    return out.reshape(batch, seq, d)
```

```python
import functools

import jax
import jax.numpy as jnp
from jax import lax
from jax.experimental import pallas as pl
from jax.experimental.pallas import tpu as pltpu

F32 = jnp.float32
BF16 = jnp.bfloat16

SSD_HEADS = 8
SSD_HEAD_DIM = 64
SSD_INNER = SSD_HEADS * SSD_HEAD_DIM
SSD_GROUPS = 2
SSD_STATE = 128
SSD_CONV = 4
SSD_CHUNK = 128
SSD_XBC = SSD_INNER + 2 * SSD_GROUPS * SSD_STATE
MLA_HEADS = 8
MLA_NOPE = 64
MLA_ROPE = 32
MLA_QK = MLA_NOPE + MLA_ROPE
MLA_V = 64
MLA_Q_RANK = 384
MLA_KV_RANK = 256
ROPE_THETA = 10000.0
MEM_HEADS = 4
LN_EPS = 1e-5
RMS_EPS = 1e-6
DEPTH = 1
DEEPNORM_ALPHA = (2.0 * DEPTH) ** 0.25

LANES = 128
SUBLANES = 8
VMEM_LIMIT_BYTES = 56 * 1024 * 1024

TOKEN_TILE = 1024
ROW_CHAINS = 4
SSD_ROWS = 1024
SSD_CONV_ROWS = 256
ATTN_TILE = 512
ATTN_Q_SUB = 2
FF_CHUNK = 1024
MLP_CHAINS = 4

HEAD_PAD = LANES
HALF = LANES // 2


def _params(*semantics):
    return pltpu.CompilerParams(dimension_semantics=semantics, vmem_limit_bytes=VMEM_LIMIT_BYTES)


def _full(shape):
    zeros = (0,) * len(shape)
    return pl.BlockSpec(shape, lambda *_: zeros, pipeline_mode=pl.Buffered(1))


def _layer_norm(x, g, b):
    mu = jnp.mean(x, axis=-1, keepdims=True)
    xc = x - mu
    var = jnp.mean(xc * xc, axis=-1, keepdims=True)
    return xc * lax.rsqrt(var + LN_EPS) * g + b


def _rms_norm(x, g):
    ms = jnp.mean(x * x, axis=-1, keepdims=True)
    return x * lax.rsqrt(ms + RMS_EPS) * g


def _silu(x):
    hx = 0.5 * x
    return hx + hx * jnp.tanh(hx)


def _dot(a, b):
    return jnp.dot(a, b, preferred_element_type=F32)


def _dot_nt(a, b):
    return lax.dot_general(a, b, (((1,), (1,)), ((), ())), preferred_element_type=F32)


_C_Z = 0
_C_XBC = _C_Z + SSD_INNER
_C_QLAT = _C_XBC + SSD_XBC
_C_KVLAT = _C_QLAT + MLA_Q_RANK
_C_KR = _C_KVLAT + MLA_KV_RANK
_C_END = _C_KR + LANES
_KR_DT_LANE = 2 * MLA_ROPE

LOG2_E = 1.4426950408889634
MLA_Q_SCALE = MLA_QK ** -0.5 * LOG2_E


def _in_proj_kernel(x_ref, pos_ref, invf_ref, vone_ref, g_ref, b_ref, w1_ref, qg_ref, wq_ref, kvg_ref, wkv_ref,
                    h_ref, z_ref, xbc_ref, dt_ref, q_ref, k_ref, v_ref):
    sub = x_ref.shape[0] // ROW_CHAINS
    chains = [slice(r * sub, (r + 1) * sub) for r in range(ROW_CHAINS)]
    lane = lax.broadcasted_iota(jnp.int32, (sub, LANES), 1)
    kw = MLA_HEADS * HEAD_PAD

    hs = []
    for rows in chains:
        hf = _layer_norm(x_ref[rows, :], g_ref[...], b_ref[...])
        h_ref[rows, :] = hf
        hs.append(hf.astype(BF16))
    q_lat, kv_lat, kr = [], [], []
    for h, rows in zip(hs, chains):
        lat = _dot(h, w1_ref[:, _C_QLAT:_C_END])
        q_lat.append(lat[:, :MLA_Q_RANK])
        kv_lat.append(lat[:, MLA_Q_RANK:MLA_Q_RANK + MLA_KV_RANK])
        kr.append(lat[:, MLA_Q_RANK + MLA_KV_RANK:])
        dt_ref[rows, :] = kr[-1]
    csq, csk = [], []
    for h, rows in zip(hs, chains):
        z_ref[rows, :] = _dot(h, w1_ref[:, _C_Z:_C_XBC]).astype(BF16)
        xbc_ref[rows, :] = _dot(h, w1_ref[:, _C_XBC:_C_QLAT]).astype(BF16)
        ang = pos_ref[rows, :].astype(F32) * invf_ref[...]
        cos = jnp.cos(ang)
        sin = jnp.sin(ang)
        csq.append(MLA_Q_SCALE * jnp.where(lane < MLA_NOPE, 1.0, jnp.where(lane < MLA_NOPE + MLA_ROPE, cos, sin)))
        csk.append(jnp.where(lane < MLA_ROPE, cos, jnp.where(lane < 2 * MLA_ROPE, sin, 0.0)))
    qn = [_rms_norm(v, qg_ref[...]).astype(BF16) for v in q_lat]
    kvn = [_rms_norm(v, kvg_ref[...]).astype(BF16) for v in kv_lat]

    for r, rows in enumerate(chains):
        q_all = _dot(qn[r], wq_ref[...])
        for hd in range(MLA_HEADS):
            cols = slice(hd * HEAD_PAD, (hd + 1) * HEAD_PAD)
            q_ref[rows, cols] = (q_all[:, cols] * csq[r]).astype(BF16)
        t = kr[r] * csk[r]
        rk = jnp.where((lane >= MLA_ROPE) & (lane < 2 * MLA_ROPE), t + pltpu.roll(t, MLA_ROPE, 1), 0.0)
        kpe = pltpu.roll(rk, MLA_ROPE, 1) + pltpu.roll(rk, 2 * MLA_ROPE, 1)
        kv_all = _dot(kvn[r], wkv_ref[...])
        for hd in range(MLA_HEADS):
            cols = slice(hd * HEAD_PAD, (hd + 1) * HEAD_PAD)
            k_ref[rows, cols] = (kv_all[:, cols] + kpe).astype(BF16)
        v_ref[rows, :] = (kv_all[:, kw:] + vone_ref[...]).astype(BF16)


def _in_proj(x2, pos, invf, vone, ln_g, ln_b, w1, q_norm_g, wq, kv_norm_g, wkv):
    tokens, d = x2.shape
    tm = TOKEN_TILE
    hw = MLA_HEADS * HEAD_PAD

    def row(w):
        return pl.BlockSpec((tm, w), lambda i: (i, 0))

    out_shape = (
        jax.ShapeDtypeStruct((tokens, d), F32),
        jax.ShapeDtypeStruct((tokens, SSD_INNER), BF16),
        jax.ShapeDtypeStruct((tokens, SSD_XBC), BF16),
        jax.ShapeDtypeStruct((tokens, LANES), F32),
        jax.ShapeDtypeStruct((tokens, hw), BF16),
        jax.ShapeDtypeStruct((tokens, hw), BF16),
        jax.ShapeDtypeStruct((tokens, hw), BF16),
    )
    return pl.pallas_call(
        _in_proj_kernel,
        out_shape=out_shape,
        grid=(tokens // tm,),
        in_specs=[row(d), row(1), _full((1, LANES)), _full((1, hw)), _full((1, d)), _full((1, d)),
                  _full(w1.shape), _full((1, MLA_Q_RANK)), _full(wq.shape), _full((1, MLA_KV_RANK)),
                  _full(wkv.shape)],
        out_specs=(row(d), row(SSD_INNER), row(SSD_XBC), row(LANES), row(hw), row(hw), row(hw)),
        compiler_params=_params("parallel"),
        name="in_proj",
    )(x2, pos, invf, vone, ln_g, ln_b, w1, q_norm_g, wq, kv_norm_g, wkv)


def _cumsum_lanes(x, tri):
    x1 = x.astype(BF16).astype(F32)
    r1 = x - x1
    x2 = r1.astype(BF16).astype(F32)
    x3 = r1 - x2
    parts = jnp.concatenate([x1, x2, x3, jnp.zeros_like(x)], axis=0).astype(BF16)
    sums = _dot(parts, tri)
    n = x.shape[0]
    return sums[0:n] + sums[n:2 * n] + sums[2 * n:3 * n]


def _ssd_kernel(xbc_ref, dt_ref, z_ref, shift_ref, tri_ref, cw_ref, cb_ref, dtb_ref, alog_ref, dskip_ref, ng_ref,
                y_ref, tail_ref, state_ref):
    rows = xbc_ref.shape[0]
    L = SSD_CHUNK
    pairs = SSD_HEADS // 2
    pairs_per_group = pairs // SSD_GROUPS

    @pl.when(pl.program_id(1) == 0)
    def _():
        tail_ref[...] = jnp.zeros(tail_ref.shape, F32)
        state_ref[...] = jnp.zeros(state_ref.shape, F32)

    a_head = -jnp.exp(alog_ref[...])
    scalars = []
    for c in range(rows // L):
        dt_in = dt_ref[c * L:(c + 1) * L, :].T[_KR_DT_LANE:_KR_DT_LANE + SSD_HEADS, :] + dtb_ref[...]
        dt_t = jnp.maximum(dt_in, 0.0) + jnp.log1p(jnp.exp(-jnp.abs(dt_in)))
        a_cs_t = _cumsum_lanes(dt_t * a_head, tri_ref[...])
        a_last = a_cs_t[:, L - 1:L]
        c_dec = jnp.exp(jnp.broadcast_to(a_last, a_cs_t.shape))
        w_end_t = dt_t * jnp.exp(a_last - a_cs_t)
        a_cs = jnp.concatenate([a_cs_t, jnp.zeros((L - SSD_HEADS, L), F32)], axis=0).T
        scalars.append((dt_t, a_cs_t, w_end_t, c_dec, a_cs))

    cr = shift_ref.shape[1]
    tail = tail_ref[...]
    row8 = lax.broadcasted_iota(jnp.int32, tail.shape, 0)
    xc_blocks = []
    for blk in range(rows // cr):
        sh = _dot(shift_ref[...], xbc_ref[blk * cr:(blk + 1) * cr, :])
        conv = cb_ref[...]
        for k in range(SSD_CONV):
            conv = conv + cw_ref[k:k + 1, :] * sh[k * cr:(k + 1) * cr, :]
        head_fix = jnp.zeros(tail.shape, F32)
        for k in range(SSD_CONV - 1):
            delay = SSD_CONV - 1 - k
            head_fix = head_fix + cw_ref[k:k + 1, :] * jnp.where(row8 < delay, pltpu.roll(tail, delay, 0), 0.0)
        conv = jnp.concatenate([conv[0:SUBLANES, :] + head_fix, conv[SUBLANES:, :]], axis=0)
        tail = sh[SSD_CONV * cr - SUBLANES:, :]
        xc_blocks.append(_silu(conv))
    tail_ref[...] = tail

    sq = (L, L)
    row_i = lax.broadcasted_iota(jnp.int32, sq, 0)
    col_i = lax.broadcasted_iota(jnp.int32, sq, 1)
    causal = row_i >= col_i
    left = col_i < HALF
    left_row = lax.broadcasted_iota(jnp.int32, (1, LANES), 1) < HALF

    states = [state_ref[pj] for pj in range(pairs)]
    for c in range(rows // L):
        r0 = c * L
        xc = xc_blocks[r0 // cr][r0 % cr:r0 % cr + L, :]
        xs = xc[:, 0:SSD_INNER]
        bm = xc[:, SSD_INNER:SSD_INNER + SSD_GROUPS * SSD_STATE]
        cm = xc[:, SSD_INNER + SSD_GROUPS * SSD_STATE:]
        dt_t, a_cs_t, w_end_t, c_dec, a_cs = scalars[c]

        y_pairs = []
        for g in range(SSD_GROUPS):
            b_g = bm[:, g * SSD_STATE:(g + 1) * SSD_STATE]
            c_g = cm[:, g * SSD_STATE:(g + 1) * SSD_STATE]
            b_gt = b_g.T
            cb = _dot(c_g.astype(BF16), b_gt.astype(BF16))
            for j in range(pairs_per_group):
                pj = g * pairs_per_group + j
                x_p = xs[:, pj * LANES:(pj + 1) * LANES]
                st = states[pj]
                y_p = x_p * dskip_ref[:, pj * LANES:(pj + 1) * LANES]
                new = None
                for hh in range(2):
                    hd = 2 * pj + hh
                    cs_l = jnp.broadcast_to(a_cs[:, hd:hd + 1], sq)
                    m = cb * jnp.where(causal, jnp.exp(cs_l - a_cs_t[hd:hd + 1, :]), 0.0) * dt_t[hd:hd + 1, :]
                    lhs = jnp.concatenate([m, c_g * jnp.exp(cs_l)], axis=1).astype(BF16)
                    keep = left if hh == 0 else jnp.logical_not(left)
                    x_h = jnp.where(keep, x_p, 0.0).astype(BF16)
                    rhs = jnp.concatenate([x_h, jnp.where(keep, st, 0.0).astype(BF16)], axis=0)
                    y_p = y_p + _dot(lhs, rhs)
                    part = _dot((b_gt * w_end_t[hd:hd + 1, :]).astype(BF16), x_h)
                    new = part if new is None else new + part
                dec = jnp.where(left_row, c_dec[2 * pj:2 * pj + 1, :], c_dec[2 * pj + 1:2 * pj + 2, :])
                states[pj] = st * dec + new
                y_pairs.append(y_p)

        gw = SSD_INNER // SSD_GROUPS
        zt = z_ref[r0:r0 + L, :].astype(F32)
        for g in range(SSD_GROUPS):
            y_g = jnp.concatenate(y_pairs[g * pairs_per_group:(g + 1) * pairs_per_group], axis=1)
            y_g = y_g * _silu(zt[:, g * gw:(g + 1) * gw])
            y_g = _rms_norm(y_g, ng_ref[:, g * gw:(g + 1) * gw])
            y_ref[r0:r0 + L, g * gw:(g + 1) * gw] = y_g.astype(BF16)
    for pj in range(pairs):
        state_ref[pj] = states[pj]


def _ssd(xbc, dt, z, conv_w, conv_b, dt_bias, a_log, d_skip, norm_g, batch, seq):
    rows = SSD_ROWS
    steps = seq // rows

    def tok(w):
        return pl.BlockSpec((rows, w), lambda b, c: (b * steps + c, 0))

    t_out = jnp.arange(SSD_CONV_ROWS)[:, None]
    t_in = jnp.arange(SSD_CONV_ROWS)[None, :]
    shift = jnp.concatenate([(t_out - t_in == SSD_CONV - 1 - k) for k in range(SSD_CONV)], axis=0).astype(BF16)
    tri = (jnp.arange(SSD_CHUNK)[:, None] <= jnp.arange(SSD_CHUNK)[None, :]).astype(BF16)

    pairs = SSD_HEADS // 2
    return pl.pallas_call(
        _ssd_kernel,
        out_shape=jax.ShapeDtypeStruct((batch * seq, SSD_INNER), BF16),
        grid=(batch, steps),
        in_specs=[tok(SSD_XBC), tok(LANES), tok(SSD_INNER), _full(shift.shape), _full(tri.shape),
                  _full(conv_w.shape), _full(conv_b.shape), _full(dt_bias.shape), _full(a_log.shape),
                  _full(d_skip.shape), _full(norm_g.shape)],
        out_specs=tok(SSD_INNER),
        scratch_shapes=[pltpu.VMEM((SUBLANES, SSD_XBC), F32),
                        pltpu.VMEM((pairs, SSD_STATE, LANES), F32)],
        compiler_params=_params("parallel", "arbitrary"),
        name="ssd",
    )(xbc, dt, z, shift, tri, conv_w, conv_b, dt_bias, a_log, d_skip, norm_g)


def _mla_attn_kernel(q_ref, k_ref, v_ref, o_ref, m_ref, acc_ref):
    seq = q_ref.shape[0]
    t = ATTN_TILE
    tq = ATTN_Q_SUB * t
    causal = [lax.broadcasted_iota(jnp.int32, (tq - d * t, t), 0) >= lax.broadcasted_iota(jnp.int32, (tq - d * t, t), 1)
              for d in range(ATTN_Q_SUB)]
    lane = lax.broadcasted_iota(jnp.int32, (tq, LANES), 1)
    heads = tuple(slice(hh * HEAD_PAD, (hh + 1) * HEAD_PAD) for hh in range(2))

    def tile_update(qi, hh, r0, q, k0, width, mask, first=False):
        c = heads[hh]
        s = _dot_nt(q, k_ref[k0:k0 + width, c])
        if mask is not None:
            s = jnp.where(mask, s, -jnp.inf)
        m_new = jnp.max(s, axis=-1, keepdims=True)
        if first:
            m_new = jnp.broadcast_to(m_new, (s.shape[0], LANES))
        else:
            m = m_ref[qi, hh, r0:, :]
            m_new = jnp.maximum(m, m_new)
        p = jnp.exp2(s - jnp.concatenate([m_new] * (width // LANES), axis=1)).astype(BF16)
        pv = _dot(p, v_ref[k0:k0 + width, c])
        acc_ref[qi, hh, r0:, :] = pv if first else jnp.exp2(m - m_new) * acc_ref[qi, hh, r0:, :] + pv
        m_ref[qi, hh, r0:, :] = m_new

    for qi in range(seq // tq):
        q0 = qi * tq
        qs = [q_ref[q0:q0 + tq, c] for c in heads]
        for d in range(ATTN_Q_SUB):
            for hh in range(2):
                tile_update(qi, hh, d * t, qs[hh][d * t:], q0 + d * t, t, causal[d], first=(d == 0))
        for kb in range(qi):
            for hh in range(2):
                tile_update(qi, hh, 0, qs[hh], kb * tq, tq, None)
        acc_a, acc_b = acc_ref[qi, 0], acc_ref[qi, 1]
        l_a = acc_a[:, HALF:HALF + 1]
        l_b = acc_b[:, 0:1]
        out = jnp.where(lane < HALF, acc_a * (1.0 / l_a), acc_b * (1.0 / l_b))
        o_ref[q0:q0 + tq, :] = out.astype(BF16)


def _mla_attn(q, k, v, batch, seq):
    pairs = MLA_HEADS // 2
    pw = 2 * HEAD_PAD
    spec = pl.BlockSpec((seq, pw), lambda b, p: (b, p))
    return pl.pallas_call(
        _mla_attn_kernel,
        out_shape=jax.ShapeDtypeStruct((batch * seq, MLA_HEADS * MLA_V), BF16),
        grid=(batch, pairs),
        in_specs=[spec, spec, spec],
        out_specs=pl.BlockSpec((seq, LANES), lambda b, p: (b, p)),
        scratch_shapes=[pltpu.VMEM((seq // (ATTN_Q_SUB * ATTN_TILE), 2, ATTN_Q_SUB * ATTN_TILE, LANES), F32)] * 2,
        compiler_params=_params("parallel", "parallel"),
        name="mla_attn",
    )(q, k, v)


def _mem_kv_kernel(mem_ref, wk_ref, wv_ref, k_ref, v_ref):
    m = mem_ref[...].astype(BF16)
    k_ref[...] = _dot(m, wk_ref[...]).astype(BF16)
    v_ref[...] = _dot(m, wv_ref[...]).astype(BF16)


def _mem_kv(mem2, wk, wv, batch, mem_tokens):
    d = mem2.shape[1]
    spec = pl.BlockSpec((mem_tokens, d), lambda b: (b, 0))
    return pl.pallas_call(
        _mem_kv_kernel,
        out_shape=(jax.ShapeDtypeStruct(mem2.shape, BF16),) * 2,
        grid=(batch,),
        in_specs=[spec, _full(wk.shape), _full(wv.shape)],
        out_specs=(spec, spec),
        compiler_params=_params("parallel"),
        name="mem_kv",
    )(mem2, wk, wv)


def _mix_xattn_kernel(h_ref, y_ref, o_ref, km_ref, vm_ref, wout_ref, g1_ref, b1_ref,
                      wq_ref, wo_ref, pre2_ref):
    d = h_ref.shape[1]
    hd_w = d // MEM_HEADS
    scale = hd_w ** -0.5
    sub = h_ref.shape[0] // ROW_CHAINS
    chains = [slice(r * sub, (r + 1) * sub) for r in range(ROW_CHAINS)]
    mix = [_dot(y_ref[rows, :], wout_ref[0:SSD_INNER, :]) + _dot(o_ref[rows, :], wout_ref[SSD_INNER:, :])
           for rows in chains]
    h1 = [_layer_norm(DEEPNORM_ALPHA * h_ref[rows, :] + mix[r], g1_ref[...], b1_ref[...])
          for r, rows in enumerate(chains)]
    h1b = [v.astype(BF16) for v in h1]
    xa = [None] * ROW_CHAINS
    for hd in range(MEM_HEADS):
        cols = slice(hd * hd_w, (hd + 1) * hd_w)
        q = [(_dot(v, wq_ref[:, cols]) * scale).astype(BF16) for v in h1b]
        s = [_dot_nt(v, km_ref[:, cols]) for v in q]
        p = [jnp.exp(v - jnp.max(v, axis=-1, keepdims=True)) for v in s]
        p = [(v * (1.0 / jnp.sum(v, axis=-1, keepdims=True))).astype(BF16) for v in p]
        oh = [_dot(v, vm_ref[:, cols]).astype(BF16) for v in p]
        for r in range(ROW_CHAINS):
            part = _dot(oh[r], wo_ref[cols, :])
            xa[r] = part if xa[r] is None else xa[r] + part
    for r, rows in enumerate(chains):
        pre2_ref[rows, :] = DEEPNORM_ALPHA * h1[r] + xa[r]


def _mix_xattn(h, y, o, k_mem, v_mem, w_out, ln1_g, ln1_b, wq, wo, seq, mem_tokens):
    tokens, d = h.shape
    tm = TOKEN_TILE
    per_batch = seq // tm

    def row(w):
        return pl.BlockSpec((tm, w), lambda i: (i, 0))

    mem_spec = pl.BlockSpec((mem_tokens, d), lambda i: (i // per_batch, 0))
    vec = _full((1, d))
    return pl.pallas_call(
        _mix_xattn_kernel,
        out_shape=jax.ShapeDtypeStruct((tokens, d), F32),
        grid=(tokens // tm,),
        in_specs=[row(d), row(y.shape[1]), row(o.shape[1]), mem_spec, mem_spec,
                  _full(w_out.shape), vec, vec, _full(wq.shape), _full(wo.shape)],
        out_specs=row(d),
        compiler_params=_params("parallel"),
        name="mix_xattn",
    )(h, y, o, k_mem, v_mem, w_out, ln1_g, ln1_b, wq, wo)


def _mlp_kernel(pre2_ref, g2_ref, b2_ref, wup_ref, wdn_ref, g_ref, b_ref, out_ref):
    sub = pre2_ref.shape[0] // MLP_CHAINS
    chains = [slice(r * sub, (r + 1) * sub) for r in range(MLP_CHAINS)]
    h2 = [_layer_norm(pre2_ref[rows, :], g2_ref[...], b2_ref[...]) for rows in chains]
    hb = [v.astype(BF16) for v in h2]
    ff = [None] * MLP_CHAINS
    for c in range(wup_ref.shape[1] // FF_CHUNK):
        cols = slice(c * FF_CHUNK, (c + 1) * FF_CHUNK)
        for r in range(MLP_CHAINS):
            u = jnp.maximum(_dot(hb[r], wup_ref[:, cols]), 0.0)
            part = _dot((u * u).astype(BF16), wdn_ref[cols, :])
            ff[r] = part if ff[r] is None else ff[r] + part
    for r, rows in enumerate(chains):
        out_ref[rows, :] = _layer_norm(DEEPNORM_ALPHA * h2[r] + ff[r], g_ref[...], b_ref[...])


def _mlp(pre2, ln2_g, ln2_b, w_up, w_down, ln_g, ln_b):
    tokens, d = pre2.shape
    tm = TOKEN_TILE
    row = pl.BlockSpec((tm, d), lambda i: (i, 0))
    vec = _full((1, d))
    return pl.pallas_call(
        _mlp_kernel,
        out_shape=jax.ShapeDtypeStruct((tokens, d), F32),
        grid=(tokens // tm,),
        in_specs=[row, vec, vec, _full(w_up.shape), _full(w_down.shape), vec, vec],
        out_specs=row,
        compiler_params=_params("parallel"),
        name="mlp",
    )(pre2, ln2_g, ln2_b, w_up, w_down, ln_g, ln_b)


def _rot_cols(w):
    half = MLA_ROPE // 2
    return jnp.concatenate([-w[..., half:], w[..., :half]], axis=-1)


def _pack_in_proj(w_in):
    o = 0
    segs = {}
    for name, width in (("z", SSD_INNER), ("xbc", SSD_XBC), ("dt", SSD_HEADS), ("q", MLA_Q_RANK),
                        ("kv", MLA_KV_RANK), ("kr", MLA_ROPE)):
        segs[name] = w_in[:, o:o + width]
        o += width
    kr, krs = segs["kr"], _rot_cols(segs["kr"])
    dt = jnp.pad(segs["dt"], ((0, 0), (0, LANES - _KR_DT_LANE - SSD_HEADS)))
    return jnp.concatenate([segs["z"], segs["xbc"], segs["q"], segs["kv"], kr, krs, dt], axis=1).astype(BF16)


def _pack_q_up(w_q_up):
    r = w_q_up.shape[0]
    w = w_q_up.reshape(r, MLA_HEADS, MLA_QK)
    rope = w[:, :, MLA_NOPE:]
    return jnp.concatenate([w[:, :, :MLA_NOPE], rope, _rot_cols(rope)], axis=-1).reshape(
        r, MLA_HEADS * HEAD_PAD).astype(BF16)


def _pack_kv_up(w_kv_up):
    r = w_kv_up.shape[0]
    w = w_kv_up.reshape(r, MLA_HEADS, MLA_NOPE + MLA_V)
    zeros = jnp.zeros((r, MLA_HEADS, HEAD_PAD - MLA_NOPE), w.dtype)
    wk = jnp.concatenate([w[:, :, :MLA_NOPE], zeros], axis=-1)
    wv = w[:, :, MLA_NOPE:]
    zv = jnp.zeros_like(wv)
    even = (jnp.arange(MLA_HEADS) % 2 == 0)[None, :, None]
    wv = jnp.concatenate([jnp.where(even, wv, zv), jnp.where(even, zv, wv)], axis=-1)
    return jnp.concatenate([wk.reshape(r, -1), wv.reshape(r, -1)], axis=1).astype(BF16)


def kernel(x, mem, positions, ln_in_g, ln_in_b, w_in, conv_w, conv_b, dt_bias, a_log, d_skip, ssd_norm_g, q_norm_g, w_q_up, kv_norm_g, w_kv_up, w_mix_out, ln1_g, ln1_b, w_mem_q, w_mem_k, w_mem_v, w_mem_o, ln2_g, ln2_b, w_up, w_down, ln3_g, ln3_b):
    batch, seq, d = x.shape
    mem_tokens = mem.shape[1]
    tokens = batch * seq
    assert w_in.shape[0] == DEPTH == 1
    assert seq % (ATTN_TILE * ATTN_Q_SUB) == 0 and seq % TOKEN_TILE == 0 and seq % SSD_ROWS == 0

    def vec(v):
        return v.reshape(1, -1).astype(F32)

    def head_rows(v):
        return jnp.broadcast_to(v.astype(F32)[:, None], (v.size, LANES))

    x2 = x.reshape(tokens, d)

    half = MLA_ROPE // 2
    inv_freq = jnp.power(ROPE_THETA, -jnp.arange(half, dtype=F32) / half)
    invf = jnp.tile(inv_freq, LANES // half)[None, :]
    head_lane = jnp.arange(MLA_HEADS * HEAD_PAD) % HEAD_PAD
    head_odd = (jnp.arange(MLA_HEADS * HEAD_PAD) // HEAD_PAD) % 2
    vone = (head_lane == jnp.where(head_odd == 1, 0, HALF)).astype(F32)[None, :]

    h, z, xbc, dt, q, k, v = _in_proj(
        x2, positions.reshape(tokens, 1), invf, vone, vec(ln_in_g), vec(ln_in_b), _pack_in_proj(w_in[0]),
        vec(q_norm_g[0]), _pack_q_up(w_q_up[0]), vec(kv_norm_g[0]), _pack_kv_up(w_kv_up[0]))

    y = _ssd(xbc, dt, z, conv_w[0].astype(F32), vec(conv_b[0]), head_rows(dt_bias[0]), head_rows(a_log[0]),
             vec(jnp.repeat(d_skip[0], SSD_HEAD_DIM)), vec(ssd_norm_g[0]), batch, seq)
    o = _mla_attn(q, k, v, batch, seq)

    k_mem, v_mem = _mem_kv(mem.reshape(batch * mem_tokens, d), w_mem_k[0].astype(BF16),
                           w_mem_v[0].astype(BF16), batch, mem_tokens)
    pre2 = _mix_xattn(h, y, o, k_mem, v_mem, w_mix_out[0].astype(BF16),
                      vec(ln1_g[0]), vec(ln1_b[0]), w_mem_q[0].astype(BF16), w_mem_o[0].astype(BF16),
                      seq, mem_tokens)
    out = _mlp(pre2, vec(ln2_g[0]), vec(ln2_b[0]), w_up[0].astype(BF16), w_down[0].astype(BF16),
               vec(ln3_g[0]), vec(ln3_b[0]))
    return out.reshape(batch, seq, d)
```

```python
import functools

import jax
import jax.numpy as jnp
from jax import lax
from jax.experimental import pallas as pl
from jax.experimental.pallas import tpu as pltpu

F32 = jnp.float32
BF16 = jnp.bfloat16

SSD_HEADS = 8
SSD_HEAD_DIM = 64
SSD_INNER = SSD_HEADS * SSD_HEAD_DIM
SSD_GROUPS = 2
SSD_STATE = 128
SSD_CONV = 4
SSD_CHUNK = 128
SSD_XBC = SSD_INNER + 2 * SSD_GROUPS * SSD_STATE
MLA_HEADS = 8
MLA_NOPE = 64
MLA_ROPE = 32
MLA_QK = MLA_NOPE + MLA_ROPE
MLA_V = 64
MLA_Q_RANK = 384
MLA_KV_RANK = 256
ROPE_THETA = 10000.0
MEM_HEADS = 4
LN_EPS = 1e-5
RMS_EPS = 1e-6
DEPTH = 1
DEEPNORM_ALPHA = (2.0 * DEPTH) ** 0.25

LANES = 128
SUBLANES = 8
VMEM_LIMIT_BYTES = 56 * 1024 * 1024

TOKEN_TILE = 1024
ROW_CHAINS = 4
SSD_ROWS = 1024
SSD_CONV_ROWS = 256
ATTN_TILE = 512
ATTN_Q_SUB = 2
FF_CHUNK = 1024
MLP_CHAINS = 4

HEAD_PAD = LANES
HALF = LANES // 2


def _params(*semantics):
    return pltpu.CompilerParams(dimension_semantics=semantics, vmem_limit_bytes=VMEM_LIMIT_BYTES)


def _full(shape):
    zeros = (0,) * len(shape)
    return pl.BlockSpec(shape, lambda *_: zeros, pipeline_mode=pl.Buffered(1))


def _layer_norm(x, g, b):
    mu = jnp.mean(x, axis=-1, keepdims=True)
    xc = x - mu
    var = jnp.mean(xc * xc, axis=-1, keepdims=True)
    return xc * lax.rsqrt(var + LN_EPS) * g + b


def _rms_norm(x, g):
    ms = jnp.mean(x * x, axis=-1, keepdims=True)
    return x * lax.rsqrt(ms + RMS_EPS) * g


def _silu(x):
    hx = 0.5 * x
    return hx + hx * jnp.tanh(hx)


def _dot(a, b):
    return jnp.dot(a, b, preferred_element_type=F32)


def _dot_nt(a, b):
    return lax.dot_general(a, b, (((1,), (1,)), ((), ())), preferred_element_type=F32)


_C_Z = 0
_C_XBC = _C_Z + SSD_INNER
_C_QLAT = _C_XBC + SSD_XBC
_C_KVLAT = _C_QLAT + MLA_Q_RANK
_C_KR = _C_KVLAT + MLA_KV_RANK
_C_END = _C_KR + LANES
_KR_DT_LANE = 2 * MLA_ROPE

LOG2_E = 1.4426950408889634
MLA_Q_SCALE = MLA_QK ** -0.5 * LOG2_E


def _in_proj_kernel(x_ref, g_ref, b_ref, w1_ref, h_ref, z_ref, xbc_ref, lat_ref):
    sub = x_ref.shape[0] // ROW_CHAINS
    for r in range(ROW_CHAINS):
        rows = slice(r * sub, (r + 1) * sub)
        hf = _layer_norm(x_ref[rows, :], g_ref[...], b_ref[...])
        h_ref[rows, :] = hf
        h = hf.astype(BF16)
        z_ref[rows, :] = _dot(h, w1_ref[:, _C_Z:_C_XBC]).astype(BF16)
        xbc_ref[rows, :] = _dot(h, w1_ref[:, _C_XBC:_C_QLAT]).astype(BF16)
        lat_ref[rows, :] = _dot(h, w1_ref[:, _C_QLAT:_C_END])


def _mla_prep_kernel(lat_ref, pos_ref, invf_ref, vone_ref, qg_ref, wq_ref, kvg_ref, wkv_ref, q_ref, k_ref, v_ref):
    sub = lat_ref.shape[0] // ROW_CHAINS
    chains = [slice(r * sub, (r + 1) * sub) for r in range(ROW_CHAINS)]
    lane = lax.broadcasted_iota(jnp.int32, (sub, LANES), 1)
    kw = MLA_HEADS * HEAD_PAD
    c_kv, c_kr = MLA_Q_RANK, MLA_Q_RANK + MLA_KV_RANK

    qn = [_rms_norm(lat_ref[rows, :c_kv], qg_ref[...]).astype(BF16) for rows in chains]
    kvn = [_rms_norm(lat_ref[rows, c_kv:c_kr], kvg_ref[...]).astype(BF16) for rows in chains]
    kr = [lat_ref[rows, c_kr:] for rows in chains]
    csq, csk = [], []
    for rows in chains:
        ang = pos_ref[rows, :].astype(F32) * invf_ref[...]
        cos = jnp.cos(ang)
        sin = jnp.sin(ang)
        csq.append(MLA_Q_SCALE * jnp.where(lane < MLA_NOPE, 1.0, jnp.where(lane < MLA_NOPE + MLA_ROPE, cos, sin)))
        csk.append(jnp.where(lane < MLA_ROPE, cos, jnp.where(lane < 2 * MLA_ROPE, sin, 0.0)))

    for r, rows in enumerate(chains):
        q_all = _dot(qn[r], wq_ref[...])
        for hd in range(MLA_HEADS):
            cols = slice(hd * HEAD_PAD, (hd + 1) * HEAD_PAD)
            q_ref[rows, cols] = (q_all[:, cols] * csq[r]).astype(BF16)
        t = kr[r] * csk[r]
        rk = jnp.where((lane >= MLA_ROPE) & (lane < 2 * MLA_ROPE), t + pltpu.roll(t, MLA_ROPE, 1), 0.0)
        kpe = pltpu.roll(rk, MLA_ROPE, 1) + pltpu.roll(rk, 2 * MLA_ROPE, 1)
        kv_all = _dot(kvn[r], wkv_ref[...])
        for hd in range(MLA_HEADS):
            cols = slice(hd * HEAD_PAD, (hd + 1) * HEAD_PAD)
            k_ref[rows, cols] = (kv_all[:, cols] + kpe).astype(BF16)
        v_ref[rows, :] = (kv_all[:, kw:] + vone_ref[...]).astype(BF16)


def _in_proj(x2, ln_g, ln_b, w1):
    tokens, d = x2.shape
    tm = TOKEN_TILE
    lw = _C_END - _C_QLAT

    def row(w):
        return pl.BlockSpec((tm, w), lambda i: (i, 0))

    out_shape = (
        jax.ShapeDtypeStruct((tokens, d), F32),
        jax.ShapeDtypeStruct((tokens, SSD_INNER), BF16),
        jax.ShapeDtypeStruct((tokens, SSD_XBC), BF16),
        jax.ShapeDtypeStruct((tokens, lw), F32),
    )
    return pl.pallas_call(
        _in_proj_kernel,
        out_shape=out_shape,
        grid=(tokens // tm,),
        in_specs=[row(d), _full((1, d)), _full((1, d)), _full(w1.shape)],
        out_specs=(row(d), row(SSD_INNER), row(SSD_XBC), row(lw)),
        compiler_params=_params("parallel"),
        name="in_proj",
    )(x2, ln_g, ln_b, w1)


def _mla_prep(lat, pos, invf, vone, q_norm_g, wq, kv_norm_g, wkv):
    tokens, lw = lat.shape
    tm = TOKEN_TILE
    hw = MLA_HEADS * HEAD_PAD

    def row(w):
        return pl.BlockSpec((tm, w), lambda i: (i, 0))

    return pl.pallas_call(
        _mla_prep_kernel,
        out_shape=(jax.ShapeDtypeStruct((tokens, hw), BF16),) * 3,
        grid=(tokens // tm,),
        in_specs=[row(lw), row(1), _full((1, LANES)), _full((1, hw)), _full((1, MLA_Q_RANK)), _full(wq.shape),
                  _full((1, MLA_KV_RANK)), _full(wkv.shape)],
        out_specs=(row(hw), row(hw), row(hw)),
        compiler_params=_params("parallel"),
        name="mla_prep",
    )(lat, pos, invf, vone, q_norm_g, wq, kv_norm_g, wkv)


def _cumsum_lanes(x, tri):
    x1 = x.astype(BF16).astype(F32)
    r1 = x - x1
    x2 = r1.astype(BF16).astype(F32)
    x3 = r1 - x2
    parts = jnp.concatenate([x1, x2, x3, jnp.zeros_like(x)], axis=0).astype(BF16)
    sums = _dot(parts, tri)
    n = x.shape[0]
    return sums[0:n] + sums[n:2 * n] + sums[2 * n:3 * n]


def _ssd_kernel(xbc_ref, dt_ref, z_ref, shift_ref, tri_ref, cw_ref, cb_ref, dtb_ref, alog_ref, dskip_ref, ng_ref,
                y_ref, tail_ref, state_ref):
    rows = xbc_ref.shape[0]
    L = SSD_CHUNK
    pairs = SSD_HEADS // 2
    pairs_per_group = pairs // SSD_GROUPS

    @pl.when(pl.program_id(1) == 0)
    def _():
        tail_ref[...] = jnp.zeros(tail_ref.shape, F32)
        state_ref[...] = jnp.zeros(state_ref.shape, F32)

    a_head = -jnp.exp(alog_ref[...])
    scalars = []
    for c in range(rows // L):
        dt_in = dt_ref[c * L:(c + 1) * L, :].T[_KR_DT_LANE:_KR_DT_LANE + SSD_HEADS, :] + dtb_ref[...]
        dt_t = jnp.maximum(dt_in, 0.0) + jnp.log1p(jnp.exp(-jnp.abs(dt_in)))
        a_cs_t = _cumsum_lanes(dt_t * a_head, tri_ref[...])
        a_last = a_cs_t[:, L - 1:L]
        c_dec = jnp.exp(jnp.broadcast_to(a_last, a_cs_t.shape))
        w_end_t = dt_t * jnp.exp(a_last - a_cs_t)
        a_cs = jnp.concatenate([a_cs_t, jnp.zeros((L - SSD_HEADS, L), F32)], axis=0).T
        scalars.append((dt_t, a_cs_t, w_end_t, c_dec, a_cs))

    cr = shift_ref.shape[1]
    tail = tail_ref[...]
    row8 = lax.broadcasted_iota(jnp.int32, tail.shape, 0)
    xc_blocks = []
    for blk in range(rows // cr):
        sh = _dot(shift_ref[...], xbc_ref[blk * cr:(blk + 1) * cr, :])
        conv = cb_ref[...]
        for k in range(SSD_CONV):
            conv = conv + cw_ref[k:k + 1, :] * sh[k * cr:(k + 1) * cr, :]
        head_fix = jnp.zeros(tail.shape, F32)
        for k in range(SSD_CONV - 1):
            delay = SSD_CONV - 1 - k
            head_fix = head_fix + cw_ref[k:k + 1, :] * jnp.where(row8 < delay, pltpu.roll(tail, delay, 0), 0.0)
        conv = jnp.concatenate([conv[0:SUBLANES, :] + head_fix, conv[SUBLANES:, :]], axis=0)
        tail = sh[SSD_CONV * cr - SUBLANES:, :]
        xc_blocks.append(_silu(conv))
    tail_ref[...] = tail

    sq = (L, L)
    row_i = lax.broadcasted_iota(jnp.int32, sq, 0)
    col_i = lax.broadcasted_iota(jnp.int32, sq, 1)
    causal = row_i >= col_i
    left = col_i < HALF
    left_row = lax.broadcasted_iota(jnp.int32, (1, LANES), 1) < HALF

    states = [state_ref[pj] for pj in range(pairs)]
    for c in range(rows // L):
        r0 = c * L
        xc = xc_blocks[r0 // cr][r0 % cr:r0 % cr + L, :]
        xs = xc[:, 0:SSD_INNER]
        bm = xc[:, SSD_INNER:SSD_INNER + SSD_GROUPS * SSD_STATE]
        cm = xc[:, SSD_INNER + SSD_GROUPS * SSD_STATE:]
        dt_t, a_cs_t, w_end_t, c_dec, a_cs = scalars[c]

        y_pairs = []
        for g in range(SSD_GROUPS):
            b_g = bm[:, g * SSD_STATE:(g + 1) * SSD_STATE]
            c_g = cm[:, g * SSD_STATE:(g + 1) * SSD_STATE]
            b_gt = b_g.T
            cb = _dot(c_g.astype(BF16), b_gt.astype(BF16))
            for j in range(pairs_per_group):
                pj = g * pairs_per_group + j
                x_p = xs[:, pj * LANES:(pj + 1) * LANES]
                st = states[pj]
                y_p = x_p * dskip_ref[:, pj * LANES:(pj + 1) * LANES]
                new = None
                for hh in range(2):
                    hd = 2 * pj + hh
                    cs_l = jnp.broadcast_to(a_cs[:, hd:hd + 1], sq)
                    m = cb * jnp.where(causal, jnp.exp(cs_l - a_cs_t[hd:hd + 1, :]), 0.0) * dt_t[hd:hd + 1, :]
                    lhs = jnp.concatenate([m, c_g * jnp.exp(cs_l)], axis=1).astype(BF16)
                    keep = left if hh == 0 else jnp.logical_not(left)
                    x_h = jnp.where(keep, x_p, 0.0).astype(BF16)
                    rhs = jnp.concatenate([x_h, jnp.where(keep, st, 0.0).astype(BF16)], axis=0)
                    y_p = y_p + _dot(lhs, rhs)
                    part = _dot((b_gt * w_end_t[hd:hd + 1, :]).astype(BF16), x_h)
                    new = part if new is None else new + part
                dec = jnp.where(left_row, c_dec[2 * pj:2 * pj + 1, :], c_dec[2 * pj + 1:2 * pj + 2, :])
                states[pj] = st * dec + new
                y_pairs.append(y_p)

        gw = SSD_INNER // SSD_GROUPS
        zt = z_ref[r0:r0 + L, :].astype(F32)
        for g in range(SSD_GROUPS):
            y_g = jnp.concatenate(y_pairs[g * pairs_per_group:(g + 1) * pairs_per_group], axis=1)
            y_g = y_g * _silu(zt[:, g * gw:(g + 1) * gw])
            y_g = _rms_norm(y_g, ng_ref[:, g * gw:(g + 1) * gw])
            y_ref[r0:r0 + L, g * gw:(g + 1) * gw] = y_g.astype(BF16)
    for pj in range(pairs):
        state_ref[pj] = states[pj]


def _ssd(xbc, dt, z, conv_w, conv_b, dt_bias, a_log, d_skip, norm_g, batch, seq):
    rows = SSD_ROWS
    steps = seq // rows

    def tok(w):
        return pl.BlockSpec((rows, w), lambda b, c: (b * steps + c, 0))

    t_out = jnp.arange(SSD_CONV_ROWS)[:, None]
    t_in = jnp.arange(SSD_CONV_ROWS)[None, :]
    shift = jnp.concatenate([(t_out - t_in == SSD_CONV - 1 - k) for k in range(SSD_CONV)], axis=0).astype(BF16)
    tri = (jnp.arange(SSD_CHUNK)[:, None] <= jnp.arange(SSD_CHUNK)[None, :]).astype(BF16)

    dt_block = dt.shape[1] // LANES - 1
    pairs = SSD_HEADS // 2
    return pl.pallas_call(
        _ssd_kernel,
        out_shape=jax.ShapeDtypeStruct((batch * seq, SSD_INNER), BF16),
        grid=(batch, steps),
        in_specs=[tok(SSD_XBC), pl.BlockSpec((rows, LANES), lambda b, c: (b * steps + c, dt_block)),
                  tok(SSD_INNER), _full(shift.shape), _full(tri.shape),
                  _full(conv_w.shape), _full(conv_b.shape), _full(dt_bias.shape), _full(a_log.shape),
                  _full(d_skip.shape), _full(norm_g.shape)],
        out_specs=tok(SSD_INNER),
        scratch_shapes=[pltpu.VMEM((SUBLANES, SSD_XBC), F32),
                        pltpu.VMEM((pairs, SSD_STATE, LANES), F32)],
        compiler_params=_params("parallel", "arbitrary"),
        name="ssd",
    )(xbc, dt, z, shift, tri, conv_w, conv_b, dt_bias, a_log, d_skip, norm_g)


def _mla_attn_kernel(q_ref, k_ref, v_ref, o_ref, m_ref, acc_ref):
    seq = q_ref.shape[0]
    t = ATTN_TILE
    tq = ATTN_Q_SUB * t
    causal = [lax.broadcasted_iota(jnp.int32, (tq - d * t, t), 0) >= lax.broadcasted_iota(jnp.int32, (tq - d * t, t), 1)
              for d in range(ATTN_Q_SUB)]
    lane = lax.broadcasted_iota(jnp.int32, (tq, LANES), 1)
    heads = tuple(slice(hh * HEAD_PAD, (hh + 1) * HEAD_PAD) for hh in range(2))

    def tile_update(qi, hh, r0, q, k0, width, mask, first=False):
        c = heads[hh]
        s = _dot_nt(q, k_ref[k0:k0 + width, c])
        if mask is not None:
            s = jnp.where(mask, s, -jnp.inf)
        m_new = jnp.max(s, axis=-1, keepdims=True)
        if first:
            m_new = jnp.broadcast_to(m_new, (s.shape[0], LANES))
        else:
            m = m_ref[qi, hh, r0:, :]
            m_new = jnp.maximum(m, m_new)
        p = jnp.exp2(s - jnp.concatenate([m_new] * (width // LANES), axis=1)).astype(BF16)
        pv = _dot(p, v_ref[k0:k0 + width, c])
        acc_ref[qi, hh, r0:, :] = pv if first else jnp.exp2(m - m_new) * acc_ref[qi, hh, r0:, :] + pv
        m_ref[qi, hh, r0:, :] = m_new

    for qi in range(seq // tq):
        q0 = qi * tq
        qs = [q_ref[q0:q0 + tq, c] for c in heads]
        for d in range(ATTN_Q_SUB):
            for hh in range(2):
                tile_update(qi, hh, d * t, qs[hh][d * t:], q0 + d * t, t, causal[d], first=(d == 0))
        for kb in range(qi):
            for hh in range(2):
                tile_update(qi, hh, 0, qs[hh], kb * tq, tq, None)
        acc_a, acc_b = acc_ref[qi, 0], acc_ref[qi, 1]
        l_a = acc_a[:, HALF:HALF + 1]
        l_b = acc_b[:, 0:1]
        out = jnp.where(lane < HALF, acc_a * (1.0 / l_a), acc_b * (1.0 / l_b))
        o_ref[q0:q0 + tq, :] = out.astype(BF16)


def _mla_attn(q, k, v, batch, seq):
    pairs = MLA_HEADS // 2
    pw = 2 * HEAD_PAD
    spec = pl.BlockSpec((seq, pw), lambda b, p: (b, p))
    return pl.pallas_call(
        _mla_attn_kernel,
        out_shape=jax.ShapeDtypeStruct((batch * seq, MLA_HEADS * MLA_V), BF16),
        grid=(batch, pairs),
        in_specs=[spec, spec, spec],
        out_specs=pl.BlockSpec((seq, LANES), lambda b, p: (b, p)),
        scratch_shapes=[pltpu.VMEM((seq // (ATTN_Q_SUB * ATTN_TILE), 2, ATTN_Q_SUB * ATTN_TILE, LANES), F32)] * 2,
        compiler_params=_params("parallel", "parallel"),
        name="mla_attn",
    )(q, k, v)


def _mem_kv_kernel(mem_ref, wk_ref, wv_ref, k_ref, v_ref):
    m = mem_ref[...].astype(BF16)
    k_ref[...] = _dot(m, wk_ref[...]).astype(BF16)
    v_ref[...] = _dot(m, wv_ref[...]).astype(BF16)


def _mem_kv(mem2, wk, wv, batch, mem_tokens):
    d = mem2.shape[1]
    spec = pl.BlockSpec((mem_tokens, d), lambda b: (b, 0))
    return pl.pallas_call(
        _mem_kv_kernel,
        out_shape=(jax.ShapeDtypeStruct(mem2.shape, BF16),) * 2,
        grid=(batch,),
        in_specs=[spec, _full(wk.shape), _full(wv.shape)],
        out_specs=(spec, spec),
        compiler_params=_params("parallel"),
        name="mem_kv",
    )(mem2, wk, wv)


def _mix_xattn_kernel(h_ref, y_ref, o_ref, km_ref, vm_ref, wout_ref, g1_ref, b1_ref,
                      wq_ref, wo_ref, pre2_ref):
    d = h_ref.shape[1]
    hd_w = d // MEM_HEADS
    scale = hd_w ** -0.5
    sub = h_ref.shape[0] // ROW_CHAINS
    chains = [slice(r * sub, (r + 1) * sub) for r in range(ROW_CHAINS)]
    mix = [_dot(y_ref[rows, :], wout_ref[0:SSD_INNER, :]) + _dot(o_ref[rows, :], wout_ref[SSD_INNER:, :])
           for rows in chains]
    h1 = [_layer_norm(DEEPNORM_ALPHA * h_ref[rows, :] + mix[r], g1_ref[...], b1_ref[...])
          for r, rows in enumerate(chains)]
    h1b = [v.astype(BF16) for v in h1]
    xa = [None] * ROW_CHAINS
    for hd in range(MEM_HEADS):
        cols = slice(hd * hd_w, (hd + 1) * hd_w)
        q = [(_dot(v, wq_ref[:, cols]) * scale).astype(BF16) for v in h1b]
        s = [_dot_nt(v, km_ref[:, cols]) for v in q]
        p = [jnp.exp(v - jnp.max(v, axis=-1, keepdims=True)) for v in s]
        p = [(v * (1.0 / jnp.sum(v, axis=-1, keepdims=True))).astype(BF16) for v in p]
        oh = [_dot(v, vm_ref[:, cols]).astype(BF16) for v in p]
        for r in range(ROW_CHAINS):
            part = _dot(oh[r], wo_ref[cols, :])
            xa[r] = part if xa[r] is None else xa[r] + part
    for r, rows in enumerate(chains):
        pre2_ref[rows, :] = DEEPNORM_ALPHA * h1[r] + xa[r]


def _mix_xattn(h, y, o, k_mem, v_mem, w_out, ln1_g, ln1_b, wq, wo, seq, mem_tokens):
    tokens, d = h.shape
    tm = TOKEN_TILE
    per_batch = seq // tm

    def row(w):
        return pl.BlockSpec((tm, w), lambda i: (i, 0))

    mem_spec = pl.BlockSpec((mem_tokens, d), lambda i: (i // per_batch, 0))
    vec = _full((1, d))
    return pl.pallas_call(
        _mix_xattn_kernel,
        out_shape=jax.ShapeDtypeStruct((tokens, d), F32),
        grid=(tokens // tm,),
        in_specs=[row(d), row(y.shape[1]), row(o.shape[1]), mem_spec, mem_spec,
                  _full(w_out.shape), vec, vec, _full(wq.shape), _full(wo.shape)],
        out_specs=row(d),
        compiler_params=_params("parallel"),
        name="mix_xattn",
    )(h, y, o, k_mem, v_mem, w_out, ln1_g, ln1_b, wq, wo)


def _mlp_kernel(pre2_ref, g2_ref, b2_ref, wup_ref, wdn_ref, g_ref, b_ref, out_ref):
    sub = pre2_ref.shape[0] // MLP_CHAINS
    chains = [slice(r * sub, (r + 1) * sub) for r in range(MLP_CHAINS)]
    h2 = [_layer_norm(pre2_ref[rows, :], g2_ref[...], b2_ref[...]) for rows in chains]
    hb = [v.astype(BF16) for v in h2]
    ff = [None] * MLP_CHAINS
    for c in range(wup_ref.shape[1] // FF_CHUNK):
        cols = slice(c * FF_CHUNK, (c + 1) * FF_CHUNK)
        for r in range(MLP_CHAINS):
            u = jnp.maximum(_dot(hb[r], wup_ref[:, cols]), 0.0)
            part = _dot((u * u).astype(BF16), wdn_ref[cols, :])
            ff[r] = part if ff[r] is None else ff[r] + part
    for r, rows in enumerate(chains):
        out_ref[rows, :] = _layer_norm(DEEPNORM_ALPHA * h2[r] + ff[r], g_ref[...], b_ref[...])


def _mlp(pre2, ln2_g, ln2_b, w_up, w_down, ln_g, ln_b):
    tokens, d = pre2.shape
    tm = TOKEN_TILE
    row = pl.BlockSpec((tm, d), lambda i: (i, 0))
    vec = _full((1, d))
    return pl.pallas_call(
        _mlp_kernel,
        out_shape=jax.ShapeDtypeStruct((tokens, d), F32),
        grid=(tokens // tm,),
        in_specs=[row, vec, vec, _full(w_up.shape), _full(w_down.shape), vec, vec],
        out_specs=row,
        compiler_params=_params("parallel"),
        name="mlp",
    )(pre2, ln2_g, ln2_b, w_up, w_down, ln_g, ln_b)


def _rot_cols(w):
    half = MLA_ROPE // 2
    return jnp.concatenate([-w[..., half:], w[..., :half]], axis=-1)


def _pack_in_proj(w_in):
    o = 0
    segs = {}
    for name, width in (("z", SSD_INNER), ("xbc", SSD_XBC), ("dt", SSD_HEADS), ("q", MLA_Q_RANK),
                        ("kv", MLA_KV_RANK), ("kr", MLA_ROPE)):
        segs[name] = w_in[:, o:o + width]
        o += width
    kr, krs = segs["kr"], _rot_cols(segs["kr"])
    dt = jnp.pad(segs["dt"], ((0, 0), (0, LANES - _KR_DT_LANE - SSD_HEADS)))
    return jnp.concatenate([segs["z"], segs["xbc"], segs["q"], segs["kv"], kr, krs, dt], axis=1).astype(BF16)


def _pack_q_up(w_q_up):
    r = w_q_up.shape[0]
    w = w_q_up.reshape(r, MLA_HEADS, MLA_QK)
    rope = w[:, :, MLA_NOPE:]
    return jnp.concatenate([w[:, :, :MLA_NOPE], rope, _rot_cols(rope)], axis=-1).reshape(
        r, MLA_HEADS * HEAD_PAD).astype(BF16)


def _pack_kv_up(w_kv_up):
    r = w_kv_up.shape[0]
    w = w_kv_up.reshape(r, MLA_HEADS, MLA_NOPE + MLA_V)
    zeros = jnp.zeros((r, MLA_HEADS, HEAD_PAD - MLA_NOPE), w.dtype)
    wk = jnp.concatenate([w[:, :, :MLA_NOPE], zeros], axis=-1)
    wv = w[:, :, MLA_NOPE:]
    zv = jnp.zeros_like(wv)
    even = (jnp.arange(MLA_HEADS) % 2 == 0)[None, :, None]
    wv = jnp.concatenate([jnp.where(even, wv, zv), jnp.where(even, zv, wv)], axis=-1)
    return jnp.concatenate([wk.reshape(r, -1), wv.reshape(r, -1)], axis=1).astype(BF16)


def kernel(x, mem, positions, ln_in_g, ln_in_b, w_in, conv_w, conv_b, dt_bias, a_log, d_skip, ssd_norm_g, q_norm_g, w_q_up, kv_norm_g, w_kv_up, w_mix_out, ln1_g, ln1_b, w_mem_q, w_mem_k, w_mem_v, w_mem_o, ln2_g, ln2_b, w_up, w_down, ln3_g, ln3_b):
    batch, seq, d = x.shape
    mem_tokens = mem.shape[1]
    tokens = batch * seq
    assert w_in.shape[0] == DEPTH == 1
    assert seq % (ATTN_TILE * ATTN_Q_SUB) == 0 and seq % TOKEN_TILE == 0 and seq % SSD_ROWS == 0

    def vec(v):
        return v.reshape(1, -1).astype(F32)

    def head_rows(v):
        return jnp.broadcast_to(v.astype(F32)[:, None], (v.size, LANES))

    x2 = x.reshape(tokens, d)

    half = MLA_ROPE // 2
    inv_freq = jnp.power(ROPE_THETA, -jnp.arange(half, dtype=F32) / half)
    invf = jnp.tile(inv_freq, LANES // half)[None, :]
    head_lane = jnp.arange(MLA_HEADS * HEAD_PAD) % HEAD_PAD
    head_odd = (jnp.arange(MLA_HEADS * HEAD_PAD) // HEAD_PAD) % 2
    vone = (head_lane == jnp.where(head_odd == 1, 0, HALF)).astype(F32)[None, :]

    h, z, xbc, lat = _in_proj(x2, vec(ln_in_g), vec(ln_in_b), _pack_in_proj(w_in[0]))
    q, k, v = _mla_prep(lat, positions.reshape(tokens, 1), invf, vone, vec(q_norm_g[0]), _pack_q_up(w_q_up[0]),
                        vec(kv_norm_g[0]), _pack_kv_up(w_kv_up[0]))

    y = _ssd(xbc, lat, z, conv_w[0].astype(F32), vec(conv_b[0]), head_rows(dt_bias[0]), head_rows(a_log[0]),
             vec(jnp.repeat(d_skip[0], SSD_HEAD_DIM)), vec(ssd_norm_g[0]), batch, seq)
    o = _mla_attn(q, k, v, batch, seq)

    k_mem, v_mem = _mem_kv(mem.reshape(batch * mem_tokens, d), w_mem_k[0].astype(BF16),
                           w_mem_v[0].astype(BF16), batch, mem_tokens)
    pre2 = _mix_xattn(h, y, o, k_mem, v_mem, w_mix_out[0].astype(BF16),
                      vec(ln1_g[0]), vec(ln1_b[0]), w_mem_q[0].astype(BF16), w_mem_o[0].astype(BF16),
                      seq, mem_tokens)
    out = _mlp(pre2, vec(ln2_g[0]), vec(ln2_b[0]), w_up[0].astype(BF16), w_down[0].astype(BF16),
               vec(ln3_g[0]), vec(ln3_b[0]))
    return out.reshape(batch, seq, d)
```

```python
import functools

import jax
import jax.numpy as jnp
from jax import lax
from jax.experimental import pallas as pl
from jax.experimental.pallas import tpu as pltpu

F32 = jnp.float32
BF16 = jnp.bfloat16

SSD_HEADS = 8
SSD_HEAD_DIM = 64
SSD_INNER = SSD_HEADS * SSD_HEAD_DIM
SSD_GROUPS = 2
SSD_STATE = 128
SSD_CONV = 4
SSD_CHUNK = 128
SSD_XBC = SSD_INNER + 2 * SSD_GROUPS * SSD_STATE
MLA_HEADS = 8
MLA_NOPE = 64
MLA_ROPE = 32
MLA_QK = MLA_NOPE + MLA_ROPE
MLA_V = 64
MLA_Q_RANK = 384
MLA_KV_RANK = 256
ROPE_THETA = 10000.0
ROPE_FREQS = MLA_ROPE // 2
MEM_HEADS = 4
LN_EPS = 1e-5
RMS_EPS = 1e-6
DEPTH = 1
DEEPNORM_ALPHA = (2.0 * DEPTH) ** 0.25

LANES = 128
SUBLANES = 8
VMEM_LIMIT_BYTES = 56 * 1024 * 1024

TOKEN_TILE = 1024
ROW_CHAINS = 4
SSD_ROWS = 1024
SSD_CONV_ROWS = 256
ATTN_TILE = 512
ATTN_Q_SUB = 2
FF_CHUNK = 1024
MLP_CHAINS = 4

HEAD_PAD = LANES
HALF = LANES // 2


def _params(*semantics):
    return pltpu.CompilerParams(dimension_semantics=semantics, vmem_limit_bytes=VMEM_LIMIT_BYTES)


def _full(shape):
    zeros = (0,) * len(shape)
    return pl.BlockSpec(shape, lambda *_: zeros, pipeline_mode=pl.Buffered(1))


def _layer_norm(x, g, b):
    mu = jnp.mean(x, axis=-1, keepdims=True)
    xc = x - mu
    var = jnp.mean(xc * xc, axis=-1, keepdims=True)
    return xc * lax.rsqrt(var + LN_EPS) * g + b


def _rms_norm(x, g):
    ms = jnp.mean(x * x, axis=-1, keepdims=True)
    return x * lax.rsqrt(ms + RMS_EPS) * g


def _silu(x):
    hx = 0.5 * x
    return hx + hx * jnp.tanh(hx)


def _dot(a, b):
    return jnp.dot(a, b, preferred_element_type=F32)


def _dot_nt(a, b):
    return lax.dot_general(a, b, (((1,), (1,)), ((), ())), preferred_element_type=F32)


_C_Z = 0
_C_XBC = _C_Z + SSD_INNER
_C_QLAT = _C_XBC + SSD_XBC
_C_KVLAT = _C_QLAT + MLA_Q_RANK
_C_KR = _C_KVLAT + MLA_KV_RANK
_C_END = _C_KR + LANES
_KR_DT_LANE = 2 * MLA_ROPE

LOG2_E = 1.4426950408889634
MLA_Q_SCALE = MLA_QK ** -0.5 * LOG2_E


def _in_proj_kernel(x_ref, g_ref, b_ref, w1_ref, h_ref, z_ref, xbc_ref, lat_ref):
    sub = x_ref.shape[0] // ROW_CHAINS
    for r in range(ROW_CHAINS):
        rows = slice(r * sub, (r + 1) * sub)
        hf = _layer_norm(x_ref[rows, :], g_ref[...], b_ref[...])
        h_ref[rows, :] = hf
        h = hf.astype(BF16)
        z_ref[rows, :] = _dot(h, w1_ref[:, _C_Z:_C_XBC]).astype(BF16)
        xbc_ref[rows, :] = _dot(h, w1_ref[:, _C_XBC:_C_QLAT]).astype(BF16)
        lat_ref[rows, :] = _dot(h, w1_ref[:, _C_QLAT:_C_END])


def _split3(x):
    x1 = x.astype(BF16)
    r1 = x - x1.astype(F32)
    x2 = r1.astype(BF16)
    x3 = (r1 - x2.astype(F32)).astype(BF16)
    return [x1, x2, x3]


def _mla_prep_kernel(lat_ref, pos_ref, invf_ref, rep_ref, fold_ref, vone_ref, qg_ref, wq_ref, kvg_ref, wkv_ref,
                     q_ref, k_ref, v_ref):
    tm = lat_ref.shape[0]
    sub = tm // ROW_CHAINS
    chains = [slice(r * sub, (r + 1) * sub) for r in range(ROW_CHAINS)]
    lane = lax.broadcasted_iota(jnp.int32, (sub, LANES), 1)
    kw = MLA_HEADS * HEAD_PAD
    c_kv, c_kr = MLA_Q_RANK, MLA_Q_RANK + MLA_KV_RANK

    ang = pos_ref[...].astype(F32) * invf_ref[...]
    thirds = jnp.concatenate(_split3(jnp.cos(ang)) + _split3(jnp.sin(ang)), axis=1)
    spread = _dot(rep_ref[...], thirds)
    own = (lax.broadcasted_iota(jnp.int32, (tm, LANES), 1) // ROPE_FREQS
           == lax.broadcasted_iota(jnp.int32, (tm, LANES), 0) % (LANES // ROPE_FREQS))
    own_only = jnp.concatenate([jnp.where(own, spread[:, b * LANES:(b + 1) * LANES], 0.0) for b in range(6)],
                               axis=1).astype(BF16)
    cos_sin = _dot(own_only, fold_ref[...])

    qn = [_rms_norm(lat_ref[rows, :c_kv], qg_ref[...]).astype(BF16) for rows in chains]
    kvn = [_rms_norm(lat_ref[rows, c_kv:c_kr], kvg_ref[...]).astype(BF16) for rows in chains]
    kr = [lat_ref[rows, c_kr:] for rows in chains]
    csq, csk = [], []
    for rows in chains:
        cos = cos_sin[rows, :LANES]
        sin = cos_sin[rows, LANES:]
        csq.append(MLA_Q_SCALE * jnp.where(lane < MLA_NOPE, 1.0, jnp.where(lane < MLA_NOPE + MLA_ROPE, cos, sin)))
        csk.append(jnp.where(lane < MLA_ROPE, cos, jnp.where(lane < 2 * MLA_ROPE, sin, 0.0)))

    for r, rows in enumerate(chains):
        q_all = _dot(qn[r], wq_ref[...])
        for hd in range(MLA_HEADS):
            cols = slice(hd * HEAD_PAD, (hd + 1) * HEAD_PAD)
            q_ref[rows, cols] = (q_all[:, cols] * csq[r]).astype(BF16)
        t = kr[r] * csk[r]
        rk = jnp.where((lane >= MLA_ROPE) & (lane < 2 * MLA_ROPE), t + pltpu.roll(t, MLA_ROPE, 1), 0.0)
        kpe = pltpu.roll(rk, MLA_ROPE, 1) + pltpu.roll(rk, 2 * MLA_ROPE, 1)
        kv_all = _dot(kvn[r], wkv_ref[...])
        for hd in range(MLA_HEADS):
            cols = slice(hd * HEAD_PAD, (hd + 1) * HEAD_PAD)
            k_ref[rows, cols] = (kv_all[:, cols] + kpe).astype(BF16)
        v_ref[rows, :] = (kv_all[:, kw:] + vone_ref[...]).astype(BF16)


def _in_proj(x2, ln_g, ln_b, w1):
    tokens, d = x2.shape
    tm = TOKEN_TILE
    lw = _C_END - _C_QLAT

    def row(w):
        return pl.BlockSpec((tm, w), lambda i: (i, 0))

    out_shape = (
        jax.ShapeDtypeStruct((tokens, d), F32),
        jax.ShapeDtypeStruct((tokens, SSD_INNER), BF16),
        jax.ShapeDtypeStruct((tokens, SSD_XBC), BF16),
        jax.ShapeDtypeStruct((tokens, lw), F32),
    )
    return pl.pallas_call(
        _in_proj_kernel,
        out_shape=out_shape,
        grid=(tokens // tm,),
        in_specs=[row(d), _full((1, d)), _full((1, d)), _full(w1.shape)],
        out_specs=(row(d), row(SSD_INNER), row(SSD_XBC), row(lw)),
        compiler_params=_params("parallel"),
        name="in_proj",
    )(x2, ln_g, ln_b, w1)


def _mla_prep(lat, positions, inv_freq, vone, q_norm_g, wq, kv_norm_g, wkv):
    tokens, lw = lat.shape
    tm = TOKEN_TILE
    hw = MLA_HEADS * HEAD_PAD
    per_row = LANES // ROPE_FREQS

    def row(w):
        return pl.BlockSpec((tm, w), lambda i: (i, 0))

    pos_c = jnp.repeat(positions.reshape(tokens // per_row, per_row), ROPE_FREQS, axis=1)
    invf = jnp.tile(inv_freq, per_row)[None, :]
    rep = (jnp.arange(tm)[:, None] // per_row == jnp.arange(tm // per_row)[None, :]).astype(BF16)
    src = jnp.arange(6 * LANES)[:, None]
    dst = jnp.arange(2 * LANES)[None, :]
    fold = ((src // (3 * LANES) == dst // LANES) & (src % ROPE_FREQS == dst % ROPE_FREQS)).astype(BF16)

    return pl.pallas_call(
        _mla_prep_kernel,
        out_shape=(jax.ShapeDtypeStruct((tokens, hw), BF16),) * 3,
        grid=(tokens // tm,),
        in_specs=[row(lw), pl.BlockSpec((tm // per_row, LANES), lambda i: (i, 0)), _full((1, LANES)),
                  _full(rep.shape), _full(fold.shape), _full((1, hw)), _full((1, MLA_Q_RANK)), _full(wq.shape),
                  _full((1, MLA_KV_RANK)), _full(wkv.shape)],
        out_specs=(row(hw), row(hw), row(hw)),
        compiler_params=_params("parallel"),
        name="mla_prep",
    )(lat, pos_c, invf, rep, fold, vone, q_norm_g, wq, kv_norm_g, wkv)


def _cumsum_lanes(x, tri):
    x1 = x.astype(BF16).astype(F32)
    r1 = x - x1
    x2 = r1.astype(BF16).astype(F32)
    x3 = r1 - x2
    parts = jnp.concatenate([x1, x2, x3, jnp.zeros_like(x)], axis=0).astype(BF16)
    sums = _dot(parts, tri)
    n = x.shape[0]
    return sums[0:n] + sums[n:2 * n] + sums[2 * n:3 * n]


def _ssd_kernel(xbc_ref, dt_ref, z_ref, shift_ref, tri_ref, cw_ref, cb_ref, dtb_ref, alog_ref, dskip_ref, ng_ref,
                y_ref, tail_ref, state_ref):
    rows = xbc_ref.shape[0]
    L = SSD_CHUNK
    pairs = SSD_HEADS // 2
    pairs_per_group = pairs // SSD_GROUPS

    @pl.when(pl.program_id(1) == 0)
    def _():
        tail_ref[...] = jnp.zeros(tail_ref.shape, F32)
        state_ref[...] = jnp.zeros(state_ref.shape, F32)

    a_head = -jnp.exp(alog_ref[...])
    scalars = []
    for c in range(rows // L):
        dt_in = dt_ref[c * L:(c + 1) * L, :].T[_KR_DT_LANE:_KR_DT_LANE + SSD_HEADS, :] + dtb_ref[...]
        dt_t = jnp.maximum(dt_in, 0.0) + jnp.log1p(jnp.exp(-jnp.abs(dt_in)))
        a_cs_t = _cumsum_lanes(dt_t * a_head, tri_ref[...])
        a_last = a_cs_t[:, L - 1:L]
        c_dec = jnp.exp(jnp.broadcast_to(a_last, a_cs_t.shape))
        w_end_t = dt_t * jnp.exp(a_last - a_cs_t)
        a_cs = jnp.concatenate([a_cs_t, jnp.zeros((L - SSD_HEADS, L), F32)], axis=0).T
        scalars.append((dt_t, a_cs_t, w_end_t, c_dec, a_cs))

    cr = shift_ref.shape[1]
    tail = tail_ref[...]
    row8 = lax.broadcasted_iota(jnp.int32, tail.shape, 0)
    xc_blocks = []
    for blk in range(rows // cr):
        sh = _dot(shift_ref[...], xbc_ref[blk * cr:(blk + 1) * cr, :])
        conv = cb_ref[...]
        for k in range(SSD_CONV):
            conv = conv + cw_ref[k:k + 1, :] * sh[k * cr:(k + 1) * cr, :]
        head_fix = jnp.zeros(tail.shape, F32)
        for k in range(SSD_CONV - 1):
            delay = SSD_CONV - 1 - k
            head_fix = head_fix + cw_ref[k:k + 1, :] * jnp.where(row8 < delay, pltpu.roll(tail, delay, 0), 0.0)
        conv = jnp.concatenate([conv[0:SUBLANES, :] + head_fix, conv[SUBLANES:, :]], axis=0)
        tail = sh[SSD_CONV * cr - SUBLANES:, :]
        xc_blocks.append(_silu(conv))
    tail_ref[...] = tail

    sq = (L, L)
    row_i = lax.broadcasted_iota(jnp.int32, sq, 0)
    col_i = lax.broadcasted_iota(jnp.int32, sq, 1)
    causal = row_i >= col_i
    left = col_i < HALF
    left_row = lax.broadcasted_iota(jnp.int32, (1, LANES), 1) < HALF

    states = [state_ref[pj] for pj in range(pairs)]
    for c in range(rows // L):
        r0 = c * L
        xc = xc_blocks[r0 // cr][r0 % cr:r0 % cr + L, :]
        xs = xc[:, 0:SSD_INNER]
        bm = xc[:, SSD_INNER:SSD_INNER + SSD_GROUPS * SSD_STATE]
        cm = xc[:, SSD_INNER + SSD_GROUPS * SSD_STATE:]
        dt_t, a_cs_t, w_end_t, c_dec, a_cs = scalars[c]

        y_pairs = []
        for g in range(SSD_GROUPS):
            b_g = bm[:, g * SSD_STATE:(g + 1) * SSD_STATE]
            c_g = cm[:, g * SSD_STATE:(g + 1) * SSD_STATE]
            b_gt = b_g.T
            cb = _dot(c_g.astype(BF16), b_gt.astype(BF16))
            for j in range(pairs_per_group):
                pj = g * pairs_per_group + j
                x_p = xs[:, pj * LANES:(pj + 1) * LANES]
                st = states[pj]
                y_p = x_p * dskip_ref[:, pj * LANES:(pj + 1) * LANES]
                new = None
                for hh in range(2):
                    hd = 2 * pj + hh
                    cs_l = jnp.broadcast_to(a_cs[:, hd:hd + 1], sq)
                    m = cb * jnp.where(causal, jnp.exp(cs_l - a_cs_t[hd:hd + 1, :]), 0.0) * dt_t[hd:hd + 1, :]
                    lhs = jnp.concatenate([m, c_g * jnp.exp(cs_l)], axis=1).astype(BF16)
                    keep = left if hh == 0 else jnp.logical_not(left)
                    x_h = jnp.where(keep, x_p, 0.0).astype(BF16)
                    rhs = jnp.concatenate([x_h, jnp.where(keep, st, 0.0).astype(BF16)], axis=0)
                    y_p = y_p + _dot(lhs, rhs)
                    part = _dot((b_gt * w_end_t[hd:hd + 1, :]).astype(BF16), x_h)
                    new = part if new is None else new + part
                dec = jnp.where(left_row, c_dec[2 * pj:2 * pj + 1, :], c_dec[2 * pj + 1:2 * pj + 2, :])
                states[pj] = st * dec + new
                y_pairs.append(y_p)

        gw = SSD_INNER // SSD_GROUPS
        zt = z_ref[r0:r0 + L, :].astype(F32)
        for g in range(SSD_GROUPS):
            y_g = jnp.concatenate(y_pairs[g * pairs_per_group:(g + 1) * pairs_per_group], axis=1)
            y_g = y_g * _silu(zt[:, g * gw:(g + 1) * gw])
            y_g = _rms_norm(y_g, ng_ref[:, g * gw:(g + 1) * gw])
            y_ref[r0:r0 + L, g * gw:(g + 1) * gw] = y_g.astype(BF16)
    for pj in range(pairs):
        state_ref[pj] = states[pj]


def _ssd(xbc, dt, z, conv_w, conv_b, dt_bias, a_log, d_skip, norm_g, batch, seq):
    rows = SSD_ROWS
    steps = seq // rows

    def tok(w):
        return pl.BlockSpec((rows, w), lambda b, c: (b * steps + c, 0))

    t_out = jnp.arange(SSD_CONV_ROWS)[:, None]
    t_in = jnp.arange(SSD_CONV_ROWS)[None, :]
    shift = jnp.concatenate([(t_out - t_in == SSD_CONV - 1 - k) for k in range(SSD_CONV)], axis=0).astype(BF16)
    tri = (jnp.arange(SSD_CHUNK)[:, None] <= jnp.arange(SSD_CHUNK)[None, :]).astype(BF16)

    dt_block = dt.shape[1] // LANES - 1
    pairs = SSD_HEADS // 2
    return pl.pallas_call(
        _ssd_kernel,
        out_shape=jax.ShapeDtypeStruct((batch * seq, SSD_INNER), BF16),
        grid=(batch, steps),
        in_specs=[tok(SSD_XBC), pl.BlockSpec((rows, LANES), lambda b, c: (b * steps + c, dt_block)),
                  tok(SSD_INNER), _full(shift.shape), _full(tri.shape),
                  _full(conv_w.shape), _full(conv_b.shape), _full(dt_bias.shape), _full(a_log.shape),
                  _full(d_skip.shape), _full(norm_g.shape)],
        out_specs=tok(SSD_INNER),
        scratch_shapes=[pltpu.VMEM((SUBLANES, SSD_XBC), F32),
                        pltpu.VMEM((pairs, SSD_STATE, LANES), F32)],
        compiler_params=_params("parallel", "arbitrary"),
        name="ssd",
    )(xbc, dt, z, shift, tri, conv_w, conv_b, dt_bias, a_log, d_skip, norm_g)


def _mla_attn_kernel(q_ref, k_ref, v_ref, o_ref, m_ref, acc_ref):
    seq = q_ref.shape[0]
    t = ATTN_TILE
    tq = ATTN_Q_SUB * t
    causal = [lax.broadcasted_iota(jnp.int32, (tq - d * t, t), 0) >= lax.broadcasted_iota(jnp.int32, (tq - d * t, t), 1)
              for d in range(ATTN_Q_SUB)]
    lane = lax.broadcasted_iota(jnp.int32, (tq, LANES), 1)
    heads = tuple(slice(hh * HEAD_PAD, (hh + 1) * HEAD_PAD) for hh in range(2))

    def tile_update(qi, hh, r0, q, k0, width, mask, first=False):
        c = heads[hh]
        s = _dot_nt(q, k_ref[k0:k0 + width, c])
        if mask is not None:
            s = jnp.where(mask, s, -jnp.inf)
        m_new = jnp.max(s, axis=-1, keepdims=True)
        if first:
            m_new = jnp.broadcast_to(m_new, (s.shape[0], LANES))
        else:
            m = m_ref[qi, hh, r0:, :]
            m_new = jnp.maximum(m, m_new)
        p = jnp.exp2(s - jnp.concatenate([m_new] * (width // LANES), axis=1)).astype(BF16)
        pv = _dot(p, v_ref[k0:k0 + width, c])
        acc_ref[qi, hh, r0:, :] = pv if first else jnp.exp2(m - m_new) * acc_ref[qi, hh, r0:, :] + pv
        m_ref[qi, hh, r0:, :] = m_new

    for qi in range(seq // tq):
        q0 = qi * tq
        qs = [q_ref[q0:q0 + tq, c] for c in heads]
        for d in range(ATTN_Q_SUB):
            for hh in range(2):
                tile_update(qi, hh, d * t, qs[hh][d * t:], q0 + d * t, t, causal[d], first=(d == 0))
        for kb in range(qi):
            for hh in range(2):
                tile_update(qi, hh, 0, qs[hh], kb * tq, tq, None)
        acc_a, acc_b = acc_ref[qi, 0], acc_ref[qi, 1]
        l_a = acc_a[:, HALF:HALF + 1]
        l_b = acc_b[:, 0:1]
        out = jnp.where(lane < HALF, acc_a * (1.0 / l_a), acc_b * (1.0 / l_b))
        o_ref[q0:q0 + tq, :] = out.astype(BF16)


def _mla_attn(q, k, v, batch, seq):
    pairs = MLA_HEADS // 2
    pw = 2 * HEAD_PAD
    spec = pl.BlockSpec((seq, pw), lambda b, p: (b, p))
    return pl.pallas_call(
        _mla_attn_kernel,
        out_shape=jax.ShapeDtypeStruct((batch * seq, MLA_HEADS * MLA_V), BF16),
        grid=(batch, pairs),
        in_specs=[spec, spec, spec],
        out_specs=pl.BlockSpec((seq, LANES), lambda b, p: (b, p)),
        scratch_shapes=[pltpu.VMEM((seq // (ATTN_Q_SUB * ATTN_TILE), 2, ATTN_Q_SUB * ATTN_TILE, LANES), F32)] * 2,
        compiler_params=_params("parallel", "parallel"),
        name="mla_attn",
    )(q, k, v)


def _mem_kv_kernel(mem_ref, wk_ref, wv_ref, k_ref, v_ref):
    m = mem_ref[...].astype(BF16)
    k_ref[...] = _dot(m, wk_ref[...]).astype(BF16)
    v_ref[...] = _dot(m, wv_ref[...]).astype(BF16)


def _mem_kv(mem2, wk, wv, batch, mem_tokens):
    d = mem2.shape[1]
    spec = pl.BlockSpec((mem_tokens, d), lambda b: (b, 0))
    return pl.pallas_call(
        _mem_kv_kernel,
        out_shape=(jax.ShapeDtypeStruct(mem2.shape, BF16),) * 2,
        grid=(batch,),
        in_specs=[spec, _full(wk.shape), _full(wv.shape)],
        out_specs=(spec, spec),
        compiler_params=_params("parallel"),
        name="mem_kv",
    )(mem2, wk, wv)


def _mix_xattn_kernel(h_ref, y_ref, o_ref, km_ref, vm_ref, wout_ref, g1_ref, b1_ref,
                      wq_ref, wo_ref, pre2_ref):
    d = h_ref.shape[1]
    hd_w = d // MEM_HEADS
    scale = hd_w ** -0.5
    sub = h_ref.shape[0] // ROW_CHAINS
    chains = [slice(r * sub, (r + 1) * sub) for r in range(ROW_CHAINS)]
    mix = [_dot(y_ref[rows, :], wout_ref[0:SSD_INNER, :]) + _dot(o_ref[rows, :], wout_ref[SSD_INNER:, :])
           for rows in chains]
    h1 = [_layer_norm(DEEPNORM_ALPHA * h_ref[rows, :] + mix[r], g1_ref[...], b1_ref[...])
          for r, rows in enumerate(chains)]
    h1b = [v.astype(BF16) for v in h1]
    xa = [None] * ROW_CHAINS
    for hd in range(MEM_HEADS):
        cols = slice(hd * hd_w, (hd + 1) * hd_w)
        q = [(_dot(v, wq_ref[:, cols]) * scale).astype(BF16) for v in h1b]
        s = [_dot_nt(v, km_ref[:, cols]) for v in q]
        p = [jnp.exp(v - jnp.max(v, axis=-1, keepdims=True)) for v in s]
        p = [(v * (1.0 / jnp.sum(v, axis=-1, keepdims=True))).astype(BF16) for v in p]
        oh = [_dot(v, vm_ref[:, cols]).astype(BF16) for v in p]
        for r in range(ROW_CHAINS):
            part = _dot(oh[r], wo_ref[cols, :])
            xa[r] = part if xa[r] is None else xa[r] + part
    for r, rows in enumerate(chains):
        pre2_ref[rows, :] = DEEPNORM_ALPHA * h1[r] + xa[r]


def _mix_xattn(h, y, o, k_mem, v_mem, w_out, ln1_g, ln1_b, wq, wo, seq, mem_tokens):
    tokens, d = h.shape
    tm = TOKEN_TILE
    per_batch = seq // tm

    def row(w):
        return pl.BlockSpec((tm, w), lambda i: (i, 0))

    mem_spec = pl.BlockSpec((mem_tokens, d), lambda i: (i // per_batch, 0))
    vec = _full((1, d))
    return pl.pallas_call(
        _mix_xattn_kernel,
        out_shape=jax.ShapeDtypeStruct((tokens, d), F32),
        grid=(tokens // tm,),
        in_specs=[row(d), row(y.shape[1]), row(o.shape[1]), mem_spec, mem_spec,
                  _full(w_out.shape), vec, vec, _full(wq.shape), _full(wo.shape)],
        out_specs=row(d),
        compiler_params=_params("parallel"),
        name="mix_xattn",
    )(h, y, o, k_mem, v_mem, w_out, ln1_g, ln1_b, wq, wo)


def _mlp_kernel(pre2_ref, g2_ref, b2_ref, wup_ref, wdn_ref, g_ref, b_ref, out_ref):
    sub = pre2_ref.shape[0] // MLP_CHAINS
    chains = [slice(r * sub, (r + 1) * sub) for r in range(MLP_CHAINS)]
    h2 = [_layer_norm(pre2_ref[rows, :], g2_ref[...], b2_ref[...]) for rows in chains]
    hb = [v.astype(BF16) for v in h2]
    ff = [None] * MLP_CHAINS
    for c in range(wup_ref.shape[1] // FF_CHUNK):
        cols = slice(c * FF_CHUNK, (c + 1) * FF_CHUNK)
        for r in range(MLP_CHAINS):
            u = jnp.maximum(_dot(hb[r], wup_ref[:, cols]), 0.0)
            part = _dot((u * u).astype(BF16), wdn_ref[cols, :])
            ff[r] = part if ff[r] is None else ff[r] + part
    for r, rows in enumerate(chains):
        out_ref[rows, :] = _layer_norm(DEEPNORM_ALPHA * h2[r] + ff[r], g_ref[...], b_ref[...])


def _mlp(pre2, ln2_g, ln2_b, w_up, w_down, ln_g, ln_b):
    tokens, d = pre2.shape
    tm = TOKEN_TILE
    row = pl.BlockSpec((tm, d), lambda i: (i, 0))
    vec = _full((1, d))
    return pl.pallas_call(
        _mlp_kernel,
        out_shape=jax.ShapeDtypeStruct((tokens, d), F32),
        grid=(tokens // tm,),
        in_specs=[row, vec, vec, _full(w_up.shape), _full(w_down.shape), vec, vec],
        out_specs=row,
        compiler_params=_params("parallel"),
        name="mlp",
    )(pre2, ln2_g, ln2_b, w_up, w_down, ln_g, ln_b)


def _rot_cols(w):
    half = MLA_ROPE // 2
    return jnp.concatenate([-w[..., half:], w[..., :half]], axis=-1)


def _pack_in_proj(w_in):
    o = 0
    segs = {}
    for name, width in (("z", SSD_INNER), ("xbc", SSD_XBC), ("dt", SSD_HEADS), ("q", MLA_Q_RANK),
                        ("kv", MLA_KV_RANK), ("kr", MLA_ROPE)):
        segs[name] = w_in[:, o:o + width]
        o += width
    kr, krs = segs["kr"], _rot_cols(segs["kr"])
    dt = jnp.pad(segs["dt"], ((0, 0), (0, LANES - _KR_DT_LANE - SSD_HEADS)))
    return jnp.concatenate([segs["z"], segs["xbc"], segs["q"], segs["kv"], kr, krs, dt], axis=1).astype(BF16)


def _pack_q_up(w_q_up):
    r = w_q_up.shape[0]
    w = w_q_up.reshape(r, MLA_HEADS, MLA_QK)
    rope = w[:, :, MLA_NOPE:]
    return jnp.concatenate([w[:, :, :MLA_NOPE], rope, _rot_cols(rope)], axis=-1).reshape(
        r, MLA_HEADS * HEAD_PAD).astype(BF16)


def _pack_kv_up(w_kv_up):
    r = w_kv_up.shape[0]
    w = w_kv_up.reshape(r, MLA_HEADS, MLA_NOPE + MLA_V)
    zeros = jnp.zeros((r, MLA_HEADS, HEAD_PAD - MLA_NOPE), w.dtype)
    wk = jnp.concatenate([w[:, :, :MLA_NOPE], zeros], axis=-1)
    wv = w[:, :, MLA_NOPE:]
    zv = jnp.zeros_like(wv)
    even = (jnp.arange(MLA_HEADS) % 2 == 0)[None, :, None]
    wv = jnp.concatenate([jnp.where(even, wv, zv), jnp.where(even, zv, wv)], axis=-1)
    return jnp.concatenate([wk.reshape(r, -1), wv.reshape(r, -1)], axis=1).astype(BF16)


def kernel(x, mem, positions, ln_in_g, ln_in_b, w_in, conv_w, conv_b, dt_bias, a_log, d_skip, ssd_norm_g, q_norm_g, w_q_up, kv_norm_g, w_kv_up, w_mix_out, ln1_g, ln1_b, w_mem_q, w_mem_k, w_mem_v, w_mem_o, ln2_g, ln2_b, w_up, w_down, ln3_g, ln3_b):
    batch, seq, d = x.shape
    mem_tokens = mem.shape[1]
    tokens = batch * seq
    assert w_in.shape[0] == DEPTH == 1
    assert seq % (ATTN_TILE * ATTN_Q_SUB) == 0 and seq % TOKEN_TILE == 0 and seq % SSD_ROWS == 0

    def vec(v):
        return v.reshape(1, -1).astype(F32)

    def head_rows(v):
        return jnp.broadcast_to(v.astype(F32)[:, None], (v.size, LANES))

    x2 = x.reshape(tokens, d)

    inv_freq = jnp.power(ROPE_THETA, -jnp.arange(ROPE_FREQS, dtype=F32) / ROPE_FREQS)
    head_lane = jnp.arange(MLA_HEADS * HEAD_PAD) % HEAD_PAD
    head_odd = (jnp.arange(MLA_HEADS * HEAD_PAD) // HEAD_PAD) % 2
    vone = (head_lane == jnp.where(head_odd == 1, 0, HALF)).astype(F32)[None, :]

    h, z, xbc, lat = _in_proj(x2, vec(ln_in_g), vec(ln_in_b), _pack_in_proj(w_in[0]))
    q, k, v = _mla_prep(lat, positions, inv_freq, vone, vec(q_norm_g[0]), _pack_q_up(w_q_up[0]),
                        vec(kv_norm_g[0]), _pack_kv_up(w_kv_up[0]))

    y = _ssd(xbc, lat, z, conv_w[0].astype(F32), vec(conv_b[0]), head_rows(dt_bias[0]), head_rows(a_log[0]),
             vec(jnp.repeat(d_skip[0], SSD_HEAD_DIM)), vec(ssd_norm_g[0]), batch, seq)
    o = _mla_attn(q, k, v, batch, seq)

    k_mem, v_mem = _mem_kv(mem.reshape(batch * mem_tokens, d), w_mem_k[0].astype(BF16),
                           w_mem_v[0].astype(BF16), batch, mem_tokens)
    pre2 = _mix_xattn(h, y, o, k_mem, v_mem, w_mix_out[0].astype(BF16),
                      vec(ln1_g[0]), vec(ln1_b[0]), w_mem_q[0].astype(BF16), w_mem_o[0].astype(BF16),
                      seq, mem_tokens)
    out = _mlp(pre2, vec(ln2_g[0]), vec(ln2_b[0]), w_up[0].astype(BF16), w_down[0].astype(BF16),
               vec(ln3_g[0]), vec(ln3_b[0]))
    return out.reshape(batch, seq, d)
```

```python
import jax
import jax.numpy as jnp
from jax import lax
from jax.experimental import pallas as pl
from jax.experimental.pallas import tpu as pltpu

F32 = jnp.float32
BF16 = jnp.bfloat16

SSD_HEADS = 8
SSD_HEAD_DIM = 64
SSD_INNER = SSD_HEADS * SSD_HEAD_DIM
SSD_GROUPS = 2
SSD_STATE = 128
SSD_CONV = 4
SSD_CHUNK = 128
SSD_XBC = SSD_INNER + 2 * SSD_GROUPS * SSD_STATE
MLA_HEADS = 8
MLA_NOPE = 64
MLA_ROPE = 32
MLA_QK = MLA_NOPE + MLA_ROPE
MLA_V = 64
MLA_Q_RANK = 384
MLA_KV_RANK = 256
ROPE_THETA = 10000.0
ROPE_FREQS = MLA_ROPE // 2
MEM_HEADS = 4
LN_EPS = 1e-5
RMS_EPS = 1e-6
DEPTH = 1
DEEPNORM_ALPHA = (2.0 * DEPTH) ** 0.25

LANES = 128
SUBLANES = 8
VMEM_LIMIT_BYTES = 56 * 1024 * 1024

TOKEN_TILE = 1024
ROW_CHAINS = 4
SSD_ROWS = 1024
SSD_CONV_ROWS = 256
ATTN_TILE = 512
ATTN_Q_SUB = 2
FF_CHUNK = 1024
MLP_CHAINS = 4

HEAD_PAD = LANES
HALF = LANES // 2


def _params(*semantics):
    return pltpu.CompilerParams(dimension_semantics=semantics, vmem_limit_bytes=VMEM_LIMIT_BYTES)


def _full(shape):
    zeros = (0,) * len(shape)
    return pl.BlockSpec(shape, lambda *_: zeros, pipeline_mode=pl.Buffered(1))


def _layer_norm(x, g, b):
    mu = jnp.mean(x, axis=-1, keepdims=True)
    xc = x - mu
    var = jnp.mean(xc * xc, axis=-1, keepdims=True)
    return xc * lax.rsqrt(var + LN_EPS) * g + b


def _rms_norm(x, g):
    ms = jnp.mean(x * x, axis=-1, keepdims=True)
    return x * lax.rsqrt(ms + RMS_EPS) * g


def _silu(x):
    hx = 0.5 * x
    return hx + hx * jnp.tanh(hx)


def _dot(a, b):
    return jnp.dot(a, b, preferred_element_type=F32)


def _dot_nt(a, b):
    return lax.dot_general(a, b, (((1,), (1,)), ((), ())), preferred_element_type=F32)


_C_Z = 0
_C_XBC = _C_Z + SSD_INNER
_C_QLAT = _C_XBC + SSD_XBC
_C_KVLAT = _C_QLAT + MLA_Q_RANK
_C_KR = _C_KVLAT + MLA_KV_RANK
_C_END = _C_KR + LANES
_KR_DT_LANE = 2 * MLA_ROPE

LOG2_E = 1.4426950408889634
MLA_Q_SCALE = MLA_QK ** -0.5 * LOG2_E


def _in_proj_kernel(x_ref, g_ref, b_ref, w1_ref, h_ref, z_ref, xbc_ref, lat_ref):
    sub = x_ref.shape[0] // ROW_CHAINS
    for r in range(ROW_CHAINS):
        rows = slice(r * sub, (r + 1) * sub)
        hf = _layer_norm(x_ref[rows, :], g_ref[...], b_ref[...])
        h_ref[rows, :] = hf
        h = hf.astype(BF16)
        z_ref[rows, :] = _dot(h, w1_ref[:, _C_Z:_C_XBC]).astype(BF16)
        xbc_ref[rows, :] = _dot(h, w1_ref[:, _C_XBC:_C_QLAT]).astype(BF16)
        lat_ref[rows, :] = _dot(h, w1_ref[:, _C_QLAT:_C_END])


def _split3(x):
    x1 = x.astype(BF16)
    r1 = x - x1.astype(F32)
    x2 = r1.astype(BF16)
    x3 = (r1 - x2.astype(F32)).astype(BF16)
    return [x1, x2, x3]


def _mla_prep_kernel(lat_ref, pos_ref, invf_ref, rep_ref, fold_ref, vone_ref, qg_ref, wq_ref, kvg_ref, wkv_ref,
                     q_ref, k_ref, v_ref):
    tm = lat_ref.shape[0]
    sub = tm // ROW_CHAINS
    chains = [slice(r * sub, (r + 1) * sub) for r in range(ROW_CHAINS)]
    lane = lax.broadcasted_iota(jnp.int32, (sub, LANES), 1)
    kw = MLA_HEADS * HEAD_PAD
    c_kv, c_kr = MLA_Q_RANK, MLA_Q_RANK + MLA_KV_RANK

    ang = pos_ref[...].astype(F32) * invf_ref[...]
    thirds = jnp.concatenate(_split3(jnp.cos(ang)) + _split3(jnp.sin(ang)), axis=1)
    spread = _dot(rep_ref[...], thirds)
    own = (lax.broadcasted_iota(jnp.int32, (tm, LANES), 1) // ROPE_FREQS
           == lax.broadcasted_iota(jnp.int32, (tm, LANES), 0) % (LANES // ROPE_FREQS))
    own_only = jnp.concatenate([jnp.where(own, spread[:, b * LANES:(b + 1) * LANES], 0.0) for b in range(6)],
                               axis=1).astype(BF16)
    cos_sin = _dot(own_only, fold_ref[...])

    qn = [_rms_norm(lat_ref[rows, :c_kv], qg_ref[...]).astype(BF16) for rows in chains]
    kvn = [_rms_norm(lat_ref[rows, c_kv:c_kr], kvg_ref[...]).astype(BF16) for rows in chains]
    kr = [lat_ref[rows, c_kr:] for rows in chains]
    csq, csk = [], []
    for rows in chains:
        cos = cos_sin[rows, :LANES]
        sin = cos_sin[rows, LANES:]
        csq.append(MLA_Q_SCALE * jnp.where(lane < MLA_NOPE, 1.0, jnp.where(lane < MLA_NOPE + MLA_ROPE, cos, sin)))
        csk.append(jnp.where(lane < MLA_ROPE, cos, jnp.where(lane < 2 * MLA_ROPE, sin, 0.0)))

    for r, rows in enumerate(chains):
        q_all = _dot(qn[r], wq_ref[...])
        for hd in range(MLA_HEADS):
            cols = slice(hd * HEAD_PAD, (hd + 1) * HEAD_PAD)
            q_ref[rows, cols] = (q_all[:, cols] * csq[r]).astype(BF16)
        t = kr[r] * csk[r]
        rk = jnp.where((lane >= MLA_ROPE) & (lane < 2 * MLA_ROPE), t + pltpu.roll(t, MLA_ROPE, 1), 0.0)
        kpe = pltpu.roll(rk, MLA_ROPE, 1) + pltpu.roll(rk, 2 * MLA_ROPE, 1)
        kv_all = _dot(kvn[r], wkv_ref[...])
        for hd in range(MLA_HEADS):
            cols = slice(hd * HEAD_PAD, (hd + 1) * HEAD_PAD)
            k_ref[rows, cols] = (kv_all[:, cols] + kpe).astype(BF16)
        v_ref[rows, :] = (kv_all[:, kw:] + vone_ref[...]).astype(BF16)


def _in_proj(x2, ln_g, ln_b, w1):
    tokens, d = x2.shape
    tm = TOKEN_TILE
    lw = _C_END - _C_QLAT

    def row(w):
        return pl.BlockSpec((tm, w), lambda i: (i, 0))

    out_shape = (
        jax.ShapeDtypeStruct((tokens, d), F32),
        jax.ShapeDtypeStruct((tokens, SSD_INNER), BF16),
        jax.ShapeDtypeStruct((tokens, SSD_XBC), BF16),
        jax.ShapeDtypeStruct((tokens, lw), F32),
    )
    return pl.pallas_call(
        _in_proj_kernel,
        out_shape=out_shape,
        grid=(tokens // tm,),
        in_specs=[row(d), _full((1, d)), _full((1, d)), _full(w1.shape)],
        out_specs=(row(d), row(SSD_INNER), row(SSD_XBC), row(lw)),
        compiler_params=_params("parallel"),
        name="in_proj",
    )(x2, ln_g, ln_b, w1)


def _mla_prep(lat, positions, inv_freq, vone, q_norm_g, wq, kv_norm_g, wkv):
    tokens, lw = lat.shape
    tm = TOKEN_TILE
    hw = MLA_HEADS * HEAD_PAD
    per_row = LANES // ROPE_FREQS

    def row(w):
        return pl.BlockSpec((tm, w), lambda i: (i, 0))

    pos_c = jnp.repeat(positions.reshape(tokens // per_row, per_row), ROPE_FREQS, axis=1)
    invf = jnp.tile(inv_freq, per_row)[None, :]
    rep = (jnp.arange(tm)[:, None] // per_row == jnp.arange(tm // per_row)[None, :]).astype(BF16)
    src = jnp.arange(6 * LANES)[:, None]
    dst = jnp.arange(2 * LANES)[None, :]
    fold = ((src // (3 * LANES) == dst // LANES) & (src % ROPE_FREQS == dst % ROPE_FREQS)).astype(BF16)

    return pl.pallas_call(
        _mla_prep_kernel,
        out_shape=(jax.ShapeDtypeStruct((tokens, hw), BF16),) * 3,
        grid=(tokens // tm,),
        in_specs=[row(lw), pl.BlockSpec((tm // per_row, LANES), lambda i: (i, 0)), _full((1, LANES)),
                  _full(rep.shape), _full(fold.shape), _full((1, hw)), _full((1, MLA_Q_RANK)), _full(wq.shape),
                  _full((1, MLA_KV_RANK)), _full(wkv.shape)],
        out_specs=(row(hw), row(hw), row(hw)),
        compiler_params=_params("parallel"),
        name="mla_prep",
    )(lat, pos_c, invf, rep, fold, vone, q_norm_g, wq, kv_norm_g, wkv)


def _cumsum_lanes(x, tri):
    x1 = x.astype(BF16).astype(F32)
    r1 = x - x1
    x2 = r1.astype(BF16).astype(F32)
    x3 = r1 - x2
    parts = jnp.concatenate([x1, x2, x3, jnp.zeros_like(x)], axis=0).astype(BF16)
    sums = _dot(parts, tri)
    n = x.shape[0]
    return sums[0:n] + sums[n:2 * n] + sums[2 * n:3 * n]


def _ssd_kernel(xbc_ref, dt_ref, z_ref, shift_ref, tri_ref, cw_ref, cb_ref, dtb_ref, alog_ref, dskip_ref, ng_ref,
                y_ref, tail_ref, state_ref):
    rows = xbc_ref.shape[0]
    L = SSD_CHUNK
    pairs = SSD_HEADS // 2
    pairs_per_group = pairs // SSD_GROUPS

    @pl.when(pl.program_id(1) == 0)
    def _():
        tail_ref[...] = jnp.zeros(tail_ref.shape, F32)
        state_ref[...] = jnp.zeros(state_ref.shape, F32)

    a_head = -jnp.exp(alog_ref[...])
    scalars = []
    for c in range(rows // L):
        dt_in = dt_ref[c * L:(c + 1) * L, :].T[_KR_DT_LANE:_KR_DT_LANE + SSD_HEADS, :] + dtb_ref[...]
        dt_t = jnp.maximum(dt_in, 0.0) + jnp.log1p(jnp.exp(-jnp.abs(dt_in)))
        a_cs_t = _cumsum_lanes(dt_t * a_head, tri_ref[...])
        a_last = a_cs_t[:, L - 1:L]
        c_dec = jnp.exp(jnp.broadcast_to(a_last, a_cs_t.shape))
        w_end_t = dt_t * jnp.exp(a_last - a_cs_t)
        a_cs = jnp.concatenate([a_cs_t, jnp.zeros((L - SSD_HEADS, L), F32)], axis=0).T
        scalars.append((dt_t, a_cs_t, w_end_t, c_dec, a_cs))

    cr = shift_ref.shape[1]
    tail = tail_ref[...]
    row8 = lax.broadcasted_iota(jnp.int32, tail.shape, 0)
    xc_blocks = []
    for blk in range(rows // cr):
        u = xbc_ref[blk * cr:(blk + 1) * cr, :]
        u32 = u.astype(F32)
        sh = _dot(shift_ref[...], u)
        conv = cb_ref[...] + cw_ref[SSD_CONV - 1:SSD_CONV, :] * u32
        for k in range(SSD_CONV - 1):
            conv = conv + cw_ref[k:k + 1, :] * sh[k * cr:(k + 1) * cr, :]
        head_fix = jnp.zeros(tail.shape, F32)
        for k in range(SSD_CONV - 1):
            delay = SSD_CONV - 1 - k
            head_fix = head_fix + cw_ref[k:k + 1, :] * jnp.where(row8 < delay, pltpu.roll(tail, delay, 0), 0.0)
        conv = jnp.concatenate([conv[0:SUBLANES, :] + head_fix, conv[SUBLANES:, :]], axis=0)
        tail = u32[cr - SUBLANES:, :]
        xc_blocks.append(_silu(conv))
    tail_ref[...] = tail

    sq = (L, L)
    row_i = lax.broadcasted_iota(jnp.int32, sq, 0)
    col_i = lax.broadcasted_iota(jnp.int32, sq, 1)
    causal = row_i >= col_i
    left = col_i < HALF
    left_row = lax.broadcasted_iota(jnp.int32, (1, LANES), 1) < HALF

    states = [state_ref[pj] for pj in range(pairs)]
    for c in range(rows // L):
        r0 = c * L
        xc = xc_blocks[r0 // cr][r0 % cr:r0 % cr + L, :]
        xs = xc[:, 0:SSD_INNER]
        bm = xc[:, SSD_INNER:SSD_INNER + SSD_GROUPS * SSD_STATE]
        cm = xc[:, SSD_INNER + SSD_GROUPS * SSD_STATE:]
        dt_t, a_cs_t, w_end_t, c_dec, a_cs = scalars[c]

        y_pairs = []
        for g in range(SSD_GROUPS):
            b_g = bm[:, g * SSD_STATE:(g + 1) * SSD_STATE]
            c_g = cm[:, g * SSD_STATE:(g + 1) * SSD_STATE]
            b_gt = b_g.T
            cb = _dot(c_g.astype(BF16), b_gt.astype(BF16))
            for j in range(pairs_per_group):
                pj = g * pairs_per_group + j
                x_p = xs[:, pj * LANES:(pj + 1) * LANES]
                st = states[pj]
                y_p = x_p * dskip_ref[:, pj * LANES:(pj + 1) * LANES]
                new = None
                for hh in range(2):
                    hd = 2 * pj + hh
                    cs_l = jnp.broadcast_to(a_cs[:, hd:hd + 1], sq)
                    m = cb * jnp.where(causal, jnp.exp(cs_l - a_cs_t[hd:hd + 1, :]), 0.0) * dt_t[hd:hd + 1, :]
                    lhs = jnp.concatenate([m, c_g * jnp.exp(cs_l)], axis=1).astype(BF16)
                    keep = left if hh == 0 else jnp.logical_not(left)
                    x_h = jnp.where(keep, x_p, 0.0).astype(BF16)
                    rhs = jnp.concatenate([x_h, jnp.where(keep, st, 0.0).astype(BF16)], axis=0)
                    y_p = y_p + _dot(lhs, rhs)
                    part = _dot((b_gt * w_end_t[hd:hd + 1, :]).astype(BF16), x_h)
                    new = part if new is None else new + part
                dec = jnp.where(left_row, c_dec[2 * pj:2 * pj + 1, :], c_dec[2 * pj + 1:2 * pj + 2, :])
                states[pj] = st * dec + new
                y_pairs.append(y_p)

        gw = SSD_INNER // SSD_GROUPS
        zt = z_ref[r0:r0 + L, :].astype(F32)
        for g in range(SSD_GROUPS):
            y_g = jnp.concatenate(y_pairs[g * pairs_per_group:(g + 1) * pairs_per_group], axis=1)
            y_g = y_g * _silu(zt[:, g * gw:(g + 1) * gw])
            y_g = _rms_norm(y_g, ng_ref[:, g * gw:(g + 1) * gw])
            y_ref[r0:r0 + L, g * gw:(g + 1) * gw] = y_g.astype(BF16)
    for pj in range(pairs):
        state_ref[pj] = states[pj]


def _ssd(xbc, dt, z, conv_w, conv_b, dt_bias, a_log, d_skip, norm_g, batch, seq):
    rows = SSD_ROWS
    steps = seq // rows

    def tok(w):
        return pl.BlockSpec((rows, w), lambda b, c: (b * steps + c, 0))

    t_out = jnp.arange(SSD_CONV_ROWS)[:, None]
    t_in = jnp.arange(SSD_CONV_ROWS)[None, :]
    shift = jnp.concatenate([(t_out - t_in == SSD_CONV - 1 - k) for k in range(SSD_CONV - 1)], axis=0).astype(BF16)
    tri = (jnp.arange(SSD_CHUNK)[:, None] <= jnp.arange(SSD_CHUNK)[None, :]).astype(BF16)

    dt_block = dt.shape[1] // LANES - 1
    pairs = SSD_HEADS // 2
    return pl.pallas_call(
        _ssd_kernel,
        out_shape=jax.ShapeDtypeStruct((batch * seq, SSD_INNER), BF16),
        grid=(batch, steps),
        in_specs=[tok(SSD_XBC), pl.BlockSpec((rows, LANES), lambda b, c: (b * steps + c, dt_block)),
                  tok(SSD_INNER), _full(shift.shape), _full(tri.shape),
                  _full(conv_w.shape), _full(conv_b.shape), _full(dt_bias.shape), _full(a_log.shape),
                  _full(d_skip.shape), _full(norm_g.shape)],
        out_specs=tok(SSD_INNER),
        scratch_shapes=[pltpu.VMEM((SUBLANES, SSD_XBC), F32),
                        pltpu.VMEM((pairs, SSD_STATE, LANES), F32)],
        compiler_params=_params("parallel", "arbitrary"),
        name="ssd",
    )(xbc, dt, z, shift, tri, conv_w, conv_b, dt_bias, a_log, d_skip, norm_g)


def _mla_attn_kernel(q_ref, k_ref, v_ref, o_ref, m_ref, acc_ref):
    seq = q_ref.shape[0]
    t = ATTN_TILE
    tq = ATTN_Q_SUB * t
    causal = [lax.broadcasted_iota(jnp.int32, (tq - d * t, t), 0) >= lax.broadcasted_iota(jnp.int32, (tq - d * t, t), 1)
              for d in range(ATTN_Q_SUB)]
    lane = lax.broadcasted_iota(jnp.int32, (tq, LANES), 1)
    heads = tuple(slice(hh * HEAD_PAD, (hh + 1) * HEAD_PAD) for hh in range(2))

    def tile_update(qi, hh, r0, q, k0, width, mask, first=False):
        c = heads[hh]
        s = _dot_nt(q, k_ref[k0:k0 + width, c])
        if mask is not None:
            s = jnp.where(mask, s, -jnp.inf)
        m_new = jnp.max(s, axis=-1, keepdims=True)
        if first:
            m_new = jnp.broadcast_to(m_new, (s.shape[0], LANES))
        else:
            m = m_ref[qi, hh, r0:, :]
            m_new = jnp.maximum(m, m_new)
        p = jnp.exp2(s - jnp.concatenate([m_new] * (width // LANES), axis=1)).astype(BF16)
        pv = _dot(p, v_ref[k0:k0 + width, c])
        acc_ref[qi, hh, r0:, :] = pv if first else jnp.exp2(m - m_new) * acc_ref[qi, hh, r0:, :] + pv
        m_ref[qi, hh, r0:, :] = m_new

    for qi in range(seq // tq):
        q0 = qi * tq
        qs = [q_ref[q0:q0 + tq, c] for c in heads]
        for d in range(ATTN_Q_SUB):
            for hh in range(2):
                tile_update(qi, hh, d * t, qs[hh][d * t:], q0 + d * t, t, causal[d], first=(d == 0))
        for kb in range(qi):
            for hh in range(2):
                tile_update(qi, hh, 0, qs[hh], kb * tq, tq, None)
        acc_a, acc_b = acc_ref[qi, 0], acc_ref[qi, 1]
        l_a = acc_a[:, HALF:HALF + 1]
        l_b = acc_b[:, 0:1]
        out = jnp.where(lane < HALF, acc_a * (1.0 / l_a), acc_b * (1.0 / l_b))
        o_ref[q0:q0 + tq, :] = out.astype(BF16)


def _mla_attn(q, k, v, batch, seq):
    pairs = MLA_HEADS // 2
    pw = 2 * HEAD_PAD
    spec = pl.BlockSpec((seq, pw), lambda b, p: (b, p))
    return pl.pallas_call(
        _mla_attn_kernel,
        out_shape=jax.ShapeDtypeStruct((batch * seq, MLA_HEADS * MLA_V), BF16),
        grid=(batch, pairs),
        in_specs=[spec, spec, spec],
        out_specs=pl.BlockSpec((seq, LANES), lambda b, p: (b, p)),
        scratch_shapes=[pltpu.VMEM((seq // (ATTN_Q_SUB * ATTN_TILE), 2, ATTN_Q_SUB * ATTN_TILE, LANES), F32)] * 2,
        compiler_params=_params("parallel", "parallel"),
        name="mla_attn",
    )(q, k, v)


def _mem_kv_kernel(mem_ref, wk_ref, wv_ref, k_ref, v_ref):
    m = mem_ref[...].astype(BF16)
    k_ref[...] = _dot(m, wk_ref[...]).astype(BF16)
    v_ref[...] = _dot(m, wv_ref[...]).astype(BF16)


def _mem_kv(mem2, wk, wv, batch, mem_tokens):
    d = mem2.shape[1]
    spec = pl.BlockSpec((mem_tokens, d), lambda b: (b, 0))
    return pl.pallas_call(
        _mem_kv_kernel,
        out_shape=(jax.ShapeDtypeStruct(mem2.shape, BF16),) * 2,
        grid=(batch,),
        in_specs=[spec, _full(wk.shape), _full(wv.shape)],
        out_specs=(spec, spec),
        compiler_params=_params("parallel"),
        name="mem_kv",
    )(mem2, wk, wv)


def _mix_xattn_kernel(h_ref, y_ref, o_ref, km_ref, vm_ref, wout_ref, g1_ref, b1_ref,
                      wq_ref, wo_ref, pre2_ref):
    d = h_ref.shape[1]
    hd_w = d // MEM_HEADS
    scale = hd_w ** -0.5
    sub = h_ref.shape[0] // ROW_CHAINS
    chains = [slice(r * sub, (r + 1) * sub) for r in range(ROW_CHAINS)]
    mix = [_dot(y_ref[rows, :], wout_ref[0:SSD_INNER, :]) + _dot(o_ref[rows, :], wout_ref[SSD_INNER:, :])
           for rows in chains]
    h1 = [_layer_norm(DEEPNORM_ALPHA * h_ref[rows, :] + mix[r], g1_ref[...], b1_ref[...])
          for r, rows in enumerate(chains)]
    h1b = [v.astype(BF16) for v in h1]
    xa = [None] * ROW_CHAINS
    for hd in range(MEM_HEADS):
        cols = slice(hd * hd_w, (hd + 1) * hd_w)
        q = [(_dot(v, wq_ref[:, cols]) * scale).astype(BF16) for v in h1b]
        s = [_dot_nt(v, km_ref[:, cols]) for v in q]
        p = [jnp.exp(v - jnp.max(v, axis=-1, keepdims=True)) for v in s]
        p = [(v * (1.0 / jnp.sum(v, axis=-1, keepdims=True))).astype(BF16) for v in p]
        oh = [_dot(v, vm_ref[:, cols]).astype(BF16) for v in p]
        for r in range(ROW_CHAINS):
            part = _dot(oh[r], wo_ref[cols, :])
            xa[r] = part if xa[r] is None else xa[r] + part
    for r, rows in enumerate(chains):
        pre2_ref[rows, :] = DEEPNORM_ALPHA * h1[r] + xa[r]


def _mix_xattn(h, y, o, k_mem, v_mem, w_out, ln1_g, ln1_b, wq, wo, seq, mem_tokens):
    tokens, d = h.shape
    tm = TOKEN_TILE
    per_batch = seq // tm

    def row(w):
        return pl.BlockSpec((tm, w), lambda i: (i, 0))

    mem_spec = pl.BlockSpec((mem_tokens, d), lambda i: (i // per_batch, 0))
    vec = _full((1, d))
    return pl.pallas_call(
        _mix_xattn_kernel,
        out_shape=jax.ShapeDtypeStruct((tokens, d), F32),
        grid=(tokens // tm,),
        in_specs=[row(d), row(y.shape[1]), row(o.shape[1]), mem_spec, mem_spec,
                  _full(w_out.shape), vec, vec, _full(wq.shape), _full(wo.shape)],
        out_specs=row(d),
        compiler_params=_params("parallel"),
        name="mix_xattn",
    )(h, y, o, k_mem, v_mem, w_out, ln1_g, ln1_b, wq, wo)


def _mlp_kernel(pre2_ref, g2_ref, b2_ref, wup_ref, wdn_ref, g_ref, b_ref, out_ref):
    sub = pre2_ref.shape[0] // MLP_CHAINS
    chains = [slice(r * sub, (r + 1) * sub) for r in range(MLP_CHAINS)]
    h2 = [_layer_norm(pre2_ref[rows, :], g2_ref[...], b2_ref[...]) for rows in chains]
    hb = [v.astype(BF16) for v in h2]
    ff = [None] * MLP_CHAINS
    for c in range(wup_ref.shape[1] // FF_CHUNK):
        cols = slice(c * FF_CHUNK, (c + 1) * FF_CHUNK)
        for r in range(MLP_CHAINS):
            u = jnp.maximum(_dot(hb[r], wup_ref[:, cols]), 0.0)
            part = _dot((u * u).astype(BF16), wdn_ref[cols, :])
            ff[r] = part if ff[r] is None else ff[r] + part
    for r, rows in enumerate(chains):
        out_ref[rows, :] = _layer_norm(DEEPNORM_ALPHA * h2[r] + ff[r], g_ref[...], b_ref[...])


def _mlp(pre2, ln2_g, ln2_b, w_up, w_down, ln_g, ln_b):
    tokens, d = pre2.shape
    tm = TOKEN_TILE
    row = pl.BlockSpec((tm, d), lambda i: (i, 0))
    vec = _full((1, d))
    return pl.pallas_call(
        _mlp_kernel,
        out_shape=jax.ShapeDtypeStruct((tokens, d), F32),
        grid=(tokens // tm,),
        in_specs=[row, vec, vec, _full(w_up.shape), _full(w_down.shape), vec, vec],
        out_specs=row,
        compiler_params=_params("parallel"),
        name="mlp",
    )(pre2, ln2_g, ln2_b, w_up, w_down, ln_g, ln_b)


def _rot_cols(w):
    half = MLA_ROPE // 2
    return jnp.concatenate([-w[..., half:], w[..., :half]], axis=-1)


def _pack_in_proj(w_in):
    o = 0
    segs = {}
    for name, width in (("z", SSD_INNER), ("xbc", SSD_XBC), ("dt", SSD_HEADS), ("q", MLA_Q_RANK),
                        ("kv", MLA_KV_RANK), ("kr", MLA_ROPE)):
        segs[name] = w_in[:, o:o + width]
        o += width
    kr, krs = segs["kr"], _rot_cols(segs["kr"])
    dt = jnp.pad(segs["dt"], ((0, 0), (0, LANES - _KR_DT_LANE - SSD_HEADS)))
    return jnp.concatenate([segs["z"], segs["xbc"], segs["q"], segs["kv"], kr, krs, dt], axis=1).astype(BF16)


def _pack_q_up(w_q_up):
    r = w_q_up.shape[0]
    w = w_q_up.reshape(r, MLA_HEADS, MLA_QK)
    rope = w[:, :, MLA_NOPE:]
    return jnp.concatenate([w[:, :, :MLA_NOPE], rope, _rot_cols(rope)], axis=-1).reshape(
        r, MLA_HEADS * HEAD_PAD).astype(BF16)


def _pack_kv_up(w_kv_up):
    r = w_kv_up.shape[0]
    w = w_kv_up.reshape(r, MLA_HEADS, MLA_NOPE + MLA_V)
    zeros = jnp.zeros((r, MLA_HEADS, HEAD_PAD - MLA_NOPE), w.dtype)
    wk = jnp.concatenate([w[:, :, :MLA_NOPE], zeros], axis=-1)
    wv = w[:, :, MLA_NOPE:]
    zv = jnp.zeros_like(wv)
    even = (jnp.arange(MLA_HEADS) % 2 == 0)[None, :, None]
    wv = jnp.concatenate([jnp.where(even, wv, zv), jnp.where(even, zv, wv)], axis=-1)
    return jnp.concatenate([wk.reshape(r, -1), wv.reshape(r, -1)], axis=1).astype(BF16)


def kernel(x, mem, positions, ln_in_g, ln_in_b, w_in, conv_w, conv_b, dt_bias, a_log, d_skip, ssd_norm_g, q_norm_g, w_q_up, kv_norm_g, w_kv_up, w_mix_out, ln1_g, ln1_b, w_mem_q, w_mem_k, w_mem_v, w_mem_o, ln2_g, ln2_b, w_up, w_down, ln3_g, ln3_b):
    batch, seq, d = x.shape
    mem_tokens = mem.shape[1]
    tokens = batch * seq
    assert w_in.shape[0] == DEPTH == 1
    assert seq % (ATTN_TILE * ATTN_Q_SUB) == 0 and seq % TOKEN_TILE == 0 and seq % SSD_ROWS == 0

    def vec(v):
        return v.reshape(1, -1).astype(F32)

    def head_rows(v):
        return jnp.broadcast_to(v.astype(F32)[:, None], (v.size, LANES))

    x2 = x.reshape(tokens, d)

    inv_freq = jnp.power(ROPE_THETA, -jnp.arange(ROPE_FREQS, dtype=F32) / ROPE_FREQS)
    head_lane = jnp.arange(MLA_HEADS * HEAD_PAD) % HEAD_PAD
    head_odd = (jnp.arange(MLA_HEADS * HEAD_PAD) // HEAD_PAD) % 2
    vone = (head_lane == jnp.where(head_odd == 1, 0, HALF)).astype(F32)[None, :]

    h, z, xbc, lat = _in_proj(x2, vec(ln_in_g), vec(ln_in_b), _pack_in_proj(w_in[0]))
    q, k, v = _mla_prep(lat, positions, inv_freq, vone, vec(q_norm_g[0]), _pack_q_up(w_q_up[0]),
                        vec(kv_norm_g[0]), _pack_kv_up(w_kv_up[0]))

    y = _ssd(xbc, lat, z, conv_w[0].astype(F32), vec(conv_b[0]), head_rows(dt_bias[0]), head_rows(a_log[0]),
             vec(jnp.repeat(d_skip[0], SSD_HEAD_DIM)), vec(ssd_norm_g[0]), batch, seq)
    o = _mla_attn(q, k, v, batch, seq)

    k_mem, v_mem = _mem_kv(mem.reshape(batch * mem_tokens, d), w_mem_k[0].astype(BF16),
                           w_mem_v[0].astype(BF16), batch, mem_tokens)
    pre2 = _mix_xattn(h, y, o, k_mem, v_mem, w_mix_out[0].astype(BF16),
                      vec(ln1_g[0]), vec(ln1_b[0]), w_mem_q[0].astype(BF16), w_mem_o[0].astype(BF16),
                      seq, mem_tokens)
    out = _mlp(pre2, vec(ln2_g[0]), vec(ln2_b[0]), w_up[0].astype(BF16), w_down[0].astype(BF16),
               vec(ln3_g[0]), vec(ln3_b[0]))
    return out.reshape(batch, seq, d)
```

```python
import jax
import jax.numpy as jnp
from jax import lax
from jax.experimental import pallas as pl
from jax.experimental.pallas import tpu as pltpu

F32 = jnp.float32
BF16 = jnp.bfloat16

SSD_HEADS = 8
SSD_HEAD_DIM = 64
SSD_INNER = SSD_HEADS * SSD_HEAD_DIM
SSD_GROUPS = 2
SSD_STATE = 128
SSD_CONV = 4
SSD_CHUNK = 128
SSD_XBC = SSD_INNER + 2 * SSD_GROUPS * SSD_STATE
MLA_HEADS = 8
MLA_NOPE = 64
MLA_ROPE = 32
MLA_QK = MLA_NOPE + MLA_ROPE
MLA_V = 64
MLA_Q_RANK = 384
MLA_KV_RANK = 256
ROPE_THETA = 10000.0
ROPE_FREQS = MLA_ROPE // 2
MEM_HEADS = 4
LN_EPS = 1e-5
RMS_EPS = 1e-6
DEPTH = 1
DEEPNORM_ALPHA = (2.0 * DEPTH) ** 0.25

LANES = 128
SUBLANES = 8
VMEM_LIMIT_BYTES = 56 * 1024 * 1024

TOKEN_TILE = 1024
ROW_CHAINS = 4
MLA_PREP_TILE = 2048
SSD_ROWS = 2048
SSD_CONV_ROWS = 256
ATTN_TILE = 512
ATTN_Q_SUB = 2
FF_CHUNK = 1024
MLP_CHAINS = 4

HEAD_PAD = LANES
HALF = LANES // 2


def _params(*semantics):
    return pltpu.CompilerParams(dimension_semantics=semantics, vmem_limit_bytes=VMEM_LIMIT_BYTES)


def _full(shape):
    zeros = (0,) * len(shape)
    return pl.BlockSpec(shape, lambda *_: zeros, pipeline_mode=pl.Buffered(1))


def _layer_norm(x, g, b):
    mu = jnp.mean(x, axis=-1, keepdims=True)
    xc = x - mu
    var = jnp.mean(xc * xc, axis=-1, keepdims=True)
    return xc * lax.rsqrt(var + LN_EPS) * g + b


def _rms_norm(x, g):
    ms = jnp.mean(x * x, axis=-1, keepdims=True)
    return x * lax.rsqrt(ms + RMS_EPS) * g


def _silu(x):
    hx = 0.5 * x
    return hx + hx * jnp.tanh(hx)


def _dot(a, b):
    return jnp.dot(a, b, preferred_element_type=F32)


def _dot_nt(a, b):
    return lax.dot_general(a, b, (((1,), (1,)), ((), ())), preferred_element_type=F32)


_C_Z = 0
_C_XBC = _C_Z + SSD_INNER
_C_QLAT = _C_XBC + SSD_XBC
_C_KVLAT = _C_QLAT + MLA_Q_RANK
_C_KR = _C_KVLAT + MLA_KV_RANK
_C_END = _C_KR + LANES
_KR_DT_LANE = 2 * MLA_ROPE

LOG2_E = 1.4426950408889634
MLA_Q_SCALE = MLA_QK ** -0.5 * LOG2_E


def _in_proj_kernel(x_ref, g_ref, b_ref, w1_ref, h_ref, z_ref, xbc_ref, lat_ref):
    sub = x_ref.shape[0] // ROW_CHAINS
    for r in range(ROW_CHAINS):
        rows = slice(r * sub, (r + 1) * sub)
        hf = _layer_norm(x_ref[rows, :], g_ref[...], b_ref[...])
        h_ref[rows, :] = hf
        h = hf.astype(BF16)
        z_ref[rows, :] = _dot(h, w1_ref[:, _C_Z:_C_XBC]).astype(BF16)
        xbc_ref[rows, :] = _dot(h, w1_ref[:, _C_XBC:_C_QLAT]).astype(BF16)
        lat_ref[rows, :] = _dot(h, w1_ref[:, _C_QLAT:_C_END])


def _split3(x):
    x1 = x.astype(BF16)
    r1 = x - x1.astype(F32)
    x2 = r1.astype(BF16)
    x3 = (r1 - x2.astype(F32)).astype(BF16)
    return [x1, x2, x3]


def _mla_prep_kernel(lat_ref, pos_ref, invf_ref, rep_ref, fold_ref, vone_ref, qg_ref, wq_ref, kvg_ref, wkv_ref,
                     q_ref, k_ref, v_ref):
    tm = lat_ref.shape[0]
    sub = tm // ROW_CHAINS
    chains = [slice(r * sub, (r + 1) * sub) for r in range(ROW_CHAINS)]
    lane = lax.broadcasted_iota(jnp.int32, (sub, LANES), 1)
    kw = MLA_HEADS * HEAD_PAD
    c_kv, c_kr = MLA_Q_RANK, MLA_Q_RANK + MLA_KV_RANK

    ang = pos_ref[...].astype(F32) * invf_ref[...]
    thirds = jnp.concatenate(_split3(jnp.cos(ang)) + _split3(jnp.sin(ang)), axis=1)
    spread = _dot(rep_ref[...], thirds)
    own = (lax.broadcasted_iota(jnp.int32, (tm, LANES), 1) // ROPE_FREQS
           == lax.broadcasted_iota(jnp.int32, (tm, LANES), 0) % (LANES // ROPE_FREQS))
    own_only = jnp.concatenate([jnp.where(own, spread[:, b * LANES:(b + 1) * LANES], 0.0) for b in range(6)],
                               axis=1).astype(BF16)
    cos_sin = _dot(own_only, fold_ref[...])

    qn = [_rms_norm(lat_ref[rows, :c_kv], qg_ref[...]).astype(BF16) for rows in chains]
    kvn = [_rms_norm(lat_ref[rows, c_kv:c_kr], kvg_ref[...]).astype(BF16) for rows in chains]
    kr = [lat_ref[rows, c_kr:] for rows in chains]
    csq, csk = [], []
    for rows in chains:
        cos = cos_sin[rows, :LANES]
        sin = cos_sin[rows, LANES:]
        csq.append(MLA_Q_SCALE * jnp.where(lane < MLA_NOPE, 1.0, jnp.where(lane < MLA_NOPE + MLA_ROPE, cos, sin)))
        csk.append(jnp.where(lane < MLA_ROPE, cos, jnp.where(lane < 2 * MLA_ROPE, sin, 0.0)))

    for r, rows in enumerate(chains):
        q_all = _dot(qn[r], wq_ref[...])
        for hd in range(MLA_HEADS):
            cols = slice(hd * HEAD_PAD, (hd + 1) * HEAD_PAD)
            q_ref[rows, cols] = (q_all[:, cols] * csq[r]).astype(BF16)
        t = kr[r] * csk[r]
        rk = jnp.where((lane >= MLA_ROPE) & (lane < 2 * MLA_ROPE), t + pltpu.roll(t, MLA_ROPE, 1), 0.0)
        kpe = pltpu.roll(rk, MLA_ROPE, 1) + pltpu.roll(rk, 2 * MLA_ROPE, 1)
        kv_all = _dot(kvn[r], wkv_ref[...])
        for hd in range(MLA_HEADS):
            cols = slice(hd * HEAD_PAD, (hd + 1) * HEAD_PAD)
            k_ref[rows, cols] = (kv_all[:, cols] + kpe).astype(BF16)
        v_ref[rows, :] = (kv_all[:, kw:] + vone_ref[...]).astype(BF16)


def _in_proj(x2, ln_g, ln_b, w1):
    tokens, d = x2.shape
    tm = TOKEN_TILE
    lw = _C_END - _C_QLAT

    def row(w):
        return pl.BlockSpec((tm, w), lambda i: (i, 0))

    out_shape = (
        jax.ShapeDtypeStruct((tokens, d), F32),
        jax.ShapeDtypeStruct((tokens, SSD_INNER), BF16),
        jax.ShapeDtypeStruct((tokens, SSD_XBC), BF16),
        jax.ShapeDtypeStruct((tokens, lw), F32),
    )
    return pl.pallas_call(
        _in_proj_kernel,
        out_shape=out_shape,
        grid=(tokens // tm,),
        in_specs=[row(d), _full((1, d)), _full((1, d)), _full(w1.shape)],
        out_specs=(row(d), row(SSD_INNER), row(SSD_XBC), row(lw)),
        compiler_params=_params("parallel"),
        name="in_proj",
    )(x2, ln_g, ln_b, w1)


def _mla_prep(lat, positions, inv_freq, vone, q_norm_g, wq, kv_norm_g, wkv):
    tokens, lw = lat.shape
    tm = MLA_PREP_TILE
    hw = MLA_HEADS * HEAD_PAD
    per_row = LANES // ROPE_FREQS

    def row(w):
        return pl.BlockSpec((tm, w), lambda i: (i, 0))

    pos_c = jnp.repeat(positions.reshape(tokens // per_row, per_row), ROPE_FREQS, axis=1)
    invf = jnp.tile(inv_freq, per_row)[None, :]
    rep = (jnp.arange(tm)[:, None] // per_row == jnp.arange(tm // per_row)[None, :]).astype(BF16)
    src = jnp.arange(6 * LANES)[:, None]
    dst = jnp.arange(2 * LANES)[None, :]
    fold = ((src // (3 * LANES) == dst // LANES) & (src % ROPE_FREQS == dst % ROPE_FREQS)).astype(BF16)

    return pl.pallas_call(
        _mla_prep_kernel,
        out_shape=(jax.ShapeDtypeStruct((tokens, hw), BF16),) * 3,
        grid=(tokens // tm,),
        in_specs=[row(lw), pl.BlockSpec((tm // per_row, LANES), lambda i: (i, 0)), _full((1, LANES)),
                  _full(rep.shape), _full(fold.shape), _full((1, hw)), _full((1, MLA_Q_RANK)), _full(wq.shape),
                  _full((1, MLA_KV_RANK)), _full(wkv.shape)],
        out_specs=(row(hw), row(hw), row(hw)),
        compiler_params=_params("parallel"),
        name="mla_prep",
    )(lat, pos_c, invf, rep, fold, vone, q_norm_g, wq, kv_norm_g, wkv)


def _cumsum_lanes(x, tri):
    x1 = x.astype(BF16).astype(F32)
    r1 = x - x1
    x2 = r1.astype(BF16).astype(F32)
    x3 = r1 - x2
    parts = jnp.concatenate([x1, x2, x3, jnp.zeros_like(x)], axis=0).astype(BF16)
    sums = _dot(parts, tri)
    n = x.shape[0]
    return sums[0:n] + sums[n:2 * n] + sums[2 * n:3 * n]


def _ssd_kernel(xbc_ref, dt_ref, z_ref, shift_ref, tri_ref, cw_ref, cb_ref, dtb_ref, alog_ref, dskip_ref, ng_ref,
                y_ref, tail_ref, state_ref):
    rows = xbc_ref.shape[0]
    L = SSD_CHUNK
    pairs = SSD_HEADS // 2
    pairs_per_group = pairs // SSD_GROUPS

    @pl.when(pl.program_id(1) == 0)
    def _():
        tail_ref[...] = jnp.zeros(tail_ref.shape, F32)
        state_ref[...] = jnp.zeros(state_ref.shape, F32)

    a_head = -jnp.exp(alog_ref[...])
    scalars = []
    for c in range(rows // L):
        dt_in = dt_ref[c * L:(c + 1) * L, :].T[_KR_DT_LANE:_KR_DT_LANE + SSD_HEADS, :] + dtb_ref[...]
        dt_t = jnp.maximum(dt_in, 0.0) + jnp.log1p(jnp.exp(-jnp.abs(dt_in)))
        a_cs_t = _cumsum_lanes(dt_t * a_head, tri_ref[...])
        a_last = a_cs_t[:, L - 1:L]
        c_dec = jnp.exp(jnp.broadcast_to(a_last, a_cs_t.shape))
        w_end_t = dt_t * jnp.exp(a_last - a_cs_t)
        a_cs = jnp.concatenate([a_cs_t, jnp.zeros((L - SSD_HEADS, L), F32)], axis=0).T
        scalars.append((dt_t, a_cs_t, w_end_t, c_dec, a_cs))

    cr = shift_ref.shape[1]
    tail = tail_ref[...]
    row8 = lax.broadcasted_iota(jnp.int32, tail.shape, 0)
    xc_blocks = []
    for blk in range(rows // cr):
        u = xbc_ref[blk * cr:(blk + 1) * cr, :]
        u32 = u.astype(F32)
        sh = _dot(shift_ref[...], u)
        conv = cb_ref[...] + cw_ref[SSD_CONV - 1:SSD_CONV, :] * u32
        for k in range(SSD_CONV - 1):
            conv = conv + cw_ref[k:k + 1, :] * sh[k * cr:(k + 1) * cr, :]
        head_fix = jnp.zeros(tail.shape, F32)
        for k in range(SSD_CONV - 1):
            delay = SSD_CONV - 1 - k
            head_fix = head_fix + cw_ref[k:k + 1, :] * jnp.where(row8 < delay, pltpu.roll(tail, delay, 0), 0.0)
        conv = jnp.concatenate([conv[0:SUBLANES, :] + head_fix, conv[SUBLANES:, :]], axis=0)
        tail = u32[cr - SUBLANES:, :]
        xc_blocks.append(_silu(conv))
    tail_ref[...] = tail

    sq = (L, L)
    row_i = lax.broadcasted_iota(jnp.int32, sq, 0)
    col_i = lax.broadcasted_iota(jnp.int32, sq, 1)
    causal = row_i >= col_i
    left = col_i < HALF
    left_row = lax.broadcasted_iota(jnp.int32, (1, LANES), 1) < HALF

    states = [state_ref[pj] for pj in range(pairs)]
    for c in range(rows // L):
        r0 = c * L
        xc = xc_blocks[r0 // cr][r0 % cr:r0 % cr + L, :]
        xs = xc[:, 0:SSD_INNER]
        bm = xc[:, SSD_INNER:SSD_INNER + SSD_GROUPS * SSD_STATE]
        cm = xc[:, SSD_INNER + SSD_GROUPS * SSD_STATE:]
        dt_t, a_cs_t, w_end_t, c_dec, a_cs = scalars[c]

        y_pairs = []
        for g in range(SSD_GROUPS):
            b_g = bm[:, g * SSD_STATE:(g + 1) * SSD_STATE]
            c_g = cm[:, g * SSD_STATE:(g + 1) * SSD_STATE]
            b_gt = b_g.T
            cb = _dot(c_g.astype(BF16), b_gt.astype(BF16))
            for j in range(pairs_per_group):
                pj = g * pairs_per_group + j
                x_p = xs[:, pj * LANES:(pj + 1) * LANES]
                st = states[pj]
                y_p = x_p * dskip_ref[:, pj * LANES:(pj + 1) * LANES]
                new = None
                for hh in range(2):
                    hd = 2 * pj + hh
                    cs_l = jnp.broadcast_to(a_cs[:, hd:hd + 1], sq)
                    m = cb * jnp.where(causal, jnp.exp(cs_l - a_cs_t[hd:hd + 1, :]), 0.0) * dt_t[hd:hd + 1, :]
                    lhs = jnp.concatenate([m, c_g * jnp.exp(cs_l)], axis=1).astype(BF16)
                    keep = left if hh == 0 else jnp.logical_not(left)
                    x_h = jnp.where(keep, x_p, 0.0).astype(BF16)
                    rhs = jnp.concatenate([x_h, jnp.where(keep, st, 0.0).astype(BF16)], axis=0)
                    y_p = y_p + _dot(lhs, rhs)
                    part = _dot((b_gt * w_end_t[hd:hd + 1, :]).astype(BF16), x_h)
                    new = part if new is None else new + part
                dec = jnp.where(left_row, c_dec[2 * pj:2 * pj + 1, :], c_dec[2 * pj + 1:2 * pj + 2, :])
                states[pj] = st * dec + new
                y_pairs.append(y_p)

        gw = SSD_INNER // SSD_GROUPS
        zt = z_ref[r0:r0 + L, :].astype(F32)
        for g in range(SSD_GROUPS):
            y_g = jnp.concatenate(y_pairs[g * pairs_per_group:(g + 1) * pairs_per_group], axis=1)
            y_g = y_g * _silu(zt[:, g * gw:(g + 1) * gw])
            y_g = _rms_norm(y_g, ng_ref[:, g * gw:(g + 1) * gw])
            y_ref[r0:r0 + L, g * gw:(g + 1) * gw] = y_g.astype(BF16)
    for pj in range(pairs):
        state_ref[pj] = states[pj]


def _ssd(xbc, dt, z, conv_w, conv_b, dt_bias, a_log, d_skip, norm_g, batch, seq):
    rows = SSD_ROWS
    steps = seq // rows

    def tok(w):
        return pl.BlockSpec((rows, w), lambda b, c: (b * steps + c, 0))

    t_out = jnp.arange(SSD_CONV_ROWS)[:, None]
    t_in = jnp.arange(SSD_CONV_ROWS)[None, :]
    shift = jnp.concatenate([(t_out - t_in == SSD_CONV - 1 - k) for k in range(SSD_CONV - 1)], axis=0).astype(BF16)
    tri = (jnp.arange(SSD_CHUNK)[:, None] <= jnp.arange(SSD_CHUNK)[None, :]).astype(BF16)

    dt_block = dt.shape[1] // LANES - 1
    pairs = SSD_HEADS // 2
    return pl.pallas_call(
        _ssd_kernel,
        out_shape=jax.ShapeDtypeStruct((batch * seq, SSD_INNER), BF16),
        grid=(batch, steps),
        in_specs=[tok(SSD_XBC), pl.BlockSpec((rows, LANES), lambda b, c: (b * steps + c, dt_block)),
                  tok(SSD_INNER), _full(shift.shape), _full(tri.shape),
                  _full(conv_w.shape), _full(conv_b.shape), _full(dt_bias.shape), _full(a_log.shape),
                  _full(d_skip.shape), _full(norm_g.shape)],
        out_specs=tok(SSD_INNER),
        scratch_shapes=[pltpu.VMEM((SUBLANES, SSD_XBC), F32),
                        pltpu.VMEM((pairs, SSD_STATE, LANES), F32)],
        compiler_params=_params("parallel", "arbitrary"),
        name="ssd",
    )(xbc, dt, z, shift, tri, conv_w, conv_b, dt_bias, a_log, d_skip, norm_g)


def _mla_attn_kernel(q_ref, k_ref, v_ref, o_ref, m_ref, acc_ref):
    seq = q_ref.shape[0]
    t = ATTN_TILE
    tq = ATTN_Q_SUB * t
    causal = [lax.broadcasted_iota(jnp.int32, (tq - d * t, t), 0) >= lax.broadcasted_iota(jnp.int32, (tq - d * t, t), 1)
              for d in range(ATTN_Q_SUB)]
    lane = lax.broadcasted_iota(jnp.int32, (tq, LANES), 1)
    heads = tuple(slice(hh * HEAD_PAD, (hh + 1) * HEAD_PAD) for hh in range(2))

    def tile_update(qi, hh, r0, q, k0, width, mask, first=False):
        c = heads[hh]
        s = _dot_nt(q, k_ref[k0:k0 + width, c])
        if mask is not None:
            s = jnp.where(mask, s, -jnp.inf)
        m_new = jnp.max(s, axis=-1, keepdims=True)
        if first:
            m_new = jnp.broadcast_to(m_new, (s.shape[0], LANES))
        else:
            m = m_ref[qi, hh, r0:, :]
            m_new = jnp.maximum(m, m_new)
        p = jnp.exp2(s - jnp.concatenate([m_new] * (width // LANES), axis=1)).astype(BF16)
        pv = _dot(p, v_ref[k0:k0 + width, c])
        acc_ref[qi, hh, r0:, :] = pv if first else jnp.exp2(m - m_new) * acc_ref[qi, hh, r0:, :] + pv
        m_ref[qi, hh, r0:, :] = m_new

    for qi in range(seq // tq):
        q0 = qi * tq
        qs = [q_ref[q0:q0 + tq, c] for c in heads]
        for d in range(ATTN_Q_SUB):
            for hh in range(2):
                tile_update(qi, hh, d * t, qs[hh][d * t:], q0 + d * t, t, causal[d], first=(d == 0))
        for kb in range(qi):
            for hh in range(2):
                tile_update(qi, hh, 0, qs[hh], kb * tq, tq, None)
        acc_a, acc_b = acc_ref[qi, 0], acc_ref[qi, 1]
        l_a = acc_a[:, HALF:HALF + 1]
        l_b = acc_b[:, 0:1]
        out = jnp.where(lane < HALF, acc_a * (1.0 / l_a), acc_b * (1.0 / l_b))
        o_ref[q0:q0 + tq, :] = out.astype(BF16)


def _mla_attn(q, k, v, batch, seq):
    pairs = MLA_HEADS // 2
    pw = 2 * HEAD_PAD
    spec = pl.BlockSpec((seq, pw), lambda b, p: (b, p))
    return pl.pallas_call(
        _mla_attn_kernel,
        out_shape=jax.ShapeDtypeStruct((batch * seq, MLA_HEADS * MLA_V), BF16),
        grid=(batch, pairs),
        in_specs=[spec, spec, spec],
        out_specs=pl.BlockSpec((seq, LANES), lambda b, p: (b, p)),
        scratch_shapes=[pltpu.VMEM((seq // (ATTN_Q_SUB * ATTN_TILE), 2, ATTN_Q_SUB * ATTN_TILE, LANES), F32)] * 2,
        compiler_params=_params("parallel", "parallel"),
        name="mla_attn",
    )(q, k, v)


def _mem_kv_kernel(mem_ref, wk_ref, wv_ref, k_ref, v_ref):
    m = mem_ref[...].astype(BF16)
    k_ref[...] = _dot(m, wk_ref[...]).astype(BF16)
    v_ref[...] = _dot(m, wv_ref[...]).astype(BF16)


def _mem_kv(mem2, wk, wv, batch, mem_tokens):
    d = mem2.shape[1]
    spec = pl.BlockSpec((mem_tokens, d), lambda b: (b, 0))
    return pl.pallas_call(
        _mem_kv_kernel,
        out_shape=(jax.ShapeDtypeStruct(mem2.shape, BF16),) * 2,
        grid=(batch,),
        in_specs=[spec, _full(wk.shape), _full(wv.shape)],
        out_specs=(spec, spec),
        compiler_params=_params("parallel"),
        name="mem_kv",
    )(mem2, wk, wv)


def _mix_xattn_kernel(h_ref, y_ref, o_ref, km_ref, vm_ref, wout_ref, g1_ref, b1_ref,
                      wq_ref, wo_ref, pre2_ref):
    d = h_ref.shape[1]
    hd_w = d // MEM_HEADS
    scale = hd_w ** -0.5
    sub = h_ref.shape[0] // ROW_CHAINS
    chains = [slice(r * sub, (r + 1) * sub) for r in range(ROW_CHAINS)]
    mix = [_dot(y_ref[rows, :], wout_ref[0:SSD_INNER, :]) + _dot(o_ref[rows, :], wout_ref[SSD_INNER:, :])
           for rows in chains]
    h1 = [_layer_norm(DEEPNORM_ALPHA * h_ref[rows, :] + mix[r], g1_ref[...], b1_ref[...])
          for r, rows in enumerate(chains)]
    h1b = [v.astype(BF16) for v in h1]
    xa = [None] * ROW_CHAINS
    for hd in range(MEM_HEADS):
        cols = slice(hd * hd_w, (hd + 1) * hd_w)
        q = [(_dot(v, wq_ref[:, cols]) * scale).astype(BF16) for v in h1b]
        s = [_dot_nt(v, km_ref[:, cols]) for v in q]
        p = [jnp.exp(v - jnp.max(v, axis=-1, keepdims=True)) for v in s]
        p = [(v * (1.0 / jnp.sum(v, axis=-1, keepdims=True))).astype(BF16) for v in p]
        oh = [_dot(v, vm_ref[:, cols]).astype(BF16) for v in p]
        for r in range(ROW_CHAINS):
            part = _dot(oh[r], wo_ref[cols, :])
            xa[r] = part if xa[r] is None else xa[r] + part
    for r, rows in enumerate(chains):
        pre2_ref[rows, :] = DEEPNORM_ALPHA * h1[r] + xa[r]


def _mix_xattn(h, y, o, k_mem, v_mem, w_out, ln1_g, ln1_b, wq, wo, seq, mem_tokens):
    tokens, d = h.shape
    tm = TOKEN_TILE
    per_batch = seq // tm

    def row(w):
        return pl.BlockSpec((tm, w), lambda i: (i, 0))

    mem_spec = pl.BlockSpec((mem_tokens, d), lambda i: (i // per_batch, 0))
    vec = _full((1, d))
    return pl.pallas_call(
        _mix_xattn_kernel,
        out_shape=jax.ShapeDtypeStruct((tokens, d), F32),
        grid=(tokens // tm,),
        in_specs=[row(d), row(y.shape[1]), row(o.shape[1]), mem_spec, mem_spec,
                  _full(w_out.shape), vec, vec, _full(wq.shape), _full(wo.shape)],
        out_specs=row(d),
        compiler_params=_params("parallel"),
        name="mix_xattn",
    )(h, y, o, k_mem, v_mem, w_out, ln1_g, ln1_b, wq, wo)


def _mlp_kernel(pre2_ref, g2_ref, b2_ref, wup_ref, wdn_ref, g_ref, b_ref, out_ref):
    sub = pre2_ref.shape[0] // MLP_CHAINS
    chains = [slice(r * sub, (r + 1) * sub) for r in range(MLP_CHAINS)]
    h2 = [_layer_norm(pre2_ref[rows, :], g2_ref[...], b2_ref[...]) for rows in chains]
    hb = [v.astype(BF16) for v in h2]
    ff = [None] * MLP_CHAINS
    for c in range(wup_ref.shape[1] // FF_CHUNK):
        cols = slice(c * FF_CHUNK, (c + 1) * FF_CHUNK)
        for r in range(MLP_CHAINS):
            u = jnp.maximum(_dot(hb[r], wup_ref[:, cols]), 0.0)
            part = _dot((u * u).astype(BF16), wdn_ref[cols, :])
            ff[r] = part if ff[r] is None else ff[r] + part
    for r, rows in enumerate(chains):
        out_ref[rows, :] = _layer_norm(DEEPNORM_ALPHA * h2[r] + ff[r], g_ref[...], b_ref[...])


def _mlp(pre2, ln2_g, ln2_b, w_up, w_down, ln_g, ln_b):
    tokens, d = pre2.shape
    tm = TOKEN_TILE
    row = pl.BlockSpec((tm, d), lambda i: (i, 0))
    vec = _full((1, d))
    return pl.pallas_call(
        _mlp_kernel,
        out_shape=jax.ShapeDtypeStruct((tokens, d), F32),
        grid=(tokens // tm,),
        in_specs=[row, vec, vec, _full(w_up.shape), _full(w_down.shape), vec, vec],
        out_specs=row,
        compiler_params=_params("parallel"),
        name="mlp",
    )(pre2, ln2_g, ln2_b, w_up, w_down, ln_g, ln_b)


def _rot_cols(w):
    half = MLA_ROPE // 2
    return jnp.concatenate([-w[..., half:], w[..., :half]], axis=-1)


def _pack_in_proj(w_in):
    o = 0
    segs = {}
    for name, width in (("z", SSD_INNER), ("xbc", SSD_XBC), ("dt", SSD_HEADS), ("q", MLA_Q_RANK),
                        ("kv", MLA_KV_RANK), ("kr", MLA_ROPE)):
        segs[name] = w_in[:, o:o + width]
        o += width
    kr, krs = segs["kr"], _rot_cols(segs["kr"])
    dt = jnp.pad(segs["dt"], ((0, 0), (0, LANES - _KR_DT_LANE - SSD_HEADS)))
    return jnp.concatenate([segs["z"], segs["xbc"], segs["q"], segs["kv"], kr, krs, dt], axis=1).astype(BF16)


def _pack_q_up(w_q_up):
    r = w_q_up.shape[0]
    w = w_q_up.reshape(r, MLA_HEADS, MLA_QK)
    rope = w[:, :, MLA_NOPE:]
    return jnp.concatenate([w[:, :, :MLA_NOPE], rope, _rot_cols(rope)], axis=-1).reshape(
        r, MLA_HEADS * HEAD_PAD).astype(BF16)


def _pack_kv_up(w_kv_up):
    r = w_kv_up.shape[0]
    w = w_kv_up.reshape(r, MLA_HEADS, MLA_NOPE + MLA_V)
    zeros = jnp.zeros((r, MLA_HEADS, HEAD_PAD - MLA_NOPE), w.dtype)
    wk = jnp.concatenate([w[:, :, :MLA_NOPE], zeros], axis=-1)
    wv = w[:, :, MLA_NOPE:]
    zv = jnp.zeros_like(wv)
    even = (jnp.arange(MLA_HEADS) % 2 == 0)[None, :, None]
    wv = jnp.concatenate([jnp.where(even, wv, zv), jnp.where(even, zv, wv)], axis=-1)
    return jnp.concatenate([wk.reshape(r, -1), wv.reshape(r, -1)], axis=1).astype(BF16)


def kernel(x, mem, positions, ln_in_g, ln_in_b, w_in, conv_w, conv_b, dt_bias, a_log, d_skip, ssd_norm_g, q_norm_g, w_q_up, kv_norm_g, w_kv_up, w_mix_out, ln1_g, ln1_b, w_mem_q, w_mem_k, w_mem_v, w_mem_o, ln2_g, ln2_b, w_up, w_down, ln3_g, ln3_b):
    batch, seq, d = x.shape
    mem_tokens = mem.shape[1]
    tokens = batch * seq
    assert w_in.shape[0] == DEPTH == 1
    assert seq % (ATTN_TILE * ATTN_Q_SUB) == 0 and seq % TOKEN_TILE == 0 and seq % SSD_ROWS == 0

    def vec(v):
        return v.reshape(1, -1).astype(F32)

    def head_rows(v):
        return jnp.broadcast_to(v.astype(F32)[:, None], (v.size, LANES))

    x2 = x.reshape(tokens, d)

    inv_freq = jnp.power(ROPE_THETA, -jnp.arange(ROPE_FREQS, dtype=F32) / ROPE_FREQS)
    head_lane = jnp.arange(MLA_HEADS * HEAD_PAD) % HEAD_PAD
    head_odd = (jnp.arange(MLA_HEADS * HEAD_PAD) // HEAD_PAD) % 2
    vone = (head_lane == jnp.where(head_odd == 1, 0, HALF)).astype(F32)[None, :]

    h, z, xbc, lat = _in_proj(x2, vec(ln_in_g), vec(ln_in_b), _pack_in_proj(w_in[0]))
    q, k, v = _mla_prep(lat, positions, inv_freq, vone, vec(q_norm_g[0]), _pack_q_up(w_q_up[0]),
                        vec(kv_norm_g[0]), _pack_kv_up(w_kv_up[0]))

    y = _ssd(xbc, lat, z, conv_w[0].astype(F32), vec(conv_b[0]), head_rows(dt_bias[0]), head_rows(a_log[0]),
             vec(jnp.repeat(d_skip[0], SSD_HEAD_DIM)), vec(ssd_norm_g[0]), batch, seq)
    o = _mla_attn(q, k, v, batch, seq)

    k_mem, v_mem = _mem_kv(mem.reshape(batch * mem_tokens, d), w_mem_k[0].astype(BF16),
                           w_mem_v[0].astype(BF16), batch, mem_tokens)
    pre2 = _mix_xattn(h, y, o, k_mem, v_mem, w_mix_out[0].astype(BF16),
                      vec(ln1_g[0]), vec(ln1_b[0]), w_mem_q[0].astype(BF16), w_mem_o[0].astype(BF16),
                      seq, mem_tokens)
    out = _mlp(pre2, vec(ln2_g[0]), vec(ln2_b[0]), w_up[0].astype(BF16), w_down[0].astype(BF16),
               vec(ln3_g[0]), vec(ln3_b[0]))
    return out.reshape(batch, seq, d)
```

```python
import jax
import jax.numpy as jnp
from jax import lax
from jax.experimental import pallas as pl
from jax.experimental.pallas import tpu as pltpu

F32 = jnp.float32
BF16 = jnp.bfloat16

SSD_HEADS = 8
SSD_HEAD_DIM = 64
SSD_INNER = SSD_HEADS * SSD_HEAD_DIM
SSD_GROUPS = 2
SSD_STATE = 128
SSD_CONV = 4
SSD_CHUNK = 128
SSD_XBC = SSD_INNER + 2 * SSD_GROUPS * SSD_STATE
MLA_HEADS = 8
MLA_NOPE = 64
MLA_ROPE = 32
MLA_QK = MLA_NOPE + MLA_ROPE
MLA_V = 64
MLA_Q_RANK = 384
MLA_KV_RANK = 256
ROPE_THETA = 10000.0
ROPE_FREQS = MLA_ROPE // 2
MEM_HEADS = 4
LN_EPS = 1e-5
RMS_EPS = 1e-6
DEPTH = 1
DEEPNORM_ALPHA = (2.0 * DEPTH) ** 0.25

LANES = 128
SUBLANES = 8
VMEM_LIMIT_BYTES = 56 * 1024 * 1024

TOKEN_TILE = 1024
ROW_CHAINS = 4
MLA_PREP_TILE = 2048
SSD_ROWS = 2048
SSD_CONV_ROWS = 256
ATTN_TILE = 512
ATTN_Q_SUB = 2
ATTN_PAIRS = 2
FF_CHUNK = 1024
MLP_CHAINS = 4

HEAD_PAD = LANES
HALF = LANES // 2


def _params(*semantics):
    return pltpu.CompilerParams(dimension_semantics=semantics, vmem_limit_bytes=VMEM_LIMIT_BYTES)


def _full(shape):
    zeros = (0,) * len(shape)
    return pl.BlockSpec(shape, lambda *_: zeros, pipeline_mode=pl.Buffered(1))


def _layer_norm(x, g, b):
    mu = jnp.mean(x, axis=-1, keepdims=True)
    xc = x - mu
    var = jnp.mean(xc * xc, axis=-1, keepdims=True)
    return xc * lax.rsqrt(var + LN_EPS) * g + b


def _rms_norm(x, g):
    ms = jnp.mean(x * x, axis=-1, keepdims=True)
    return x * lax.rsqrt(ms + RMS_EPS) * g


def _silu(x):
    hx = 0.5 * x
    return hx + hx * jnp.tanh(hx)


def _dot(a, b):
    return jnp.dot(a, b, preferred_element_type=F32)


def _dot_nt(a, b):
    return lax.dot_general(a, b, (((1,), (1,)), ((), ())), preferred_element_type=F32)


_C_Z = 0
_C_XBC = _C_Z + SSD_INNER
_C_QLAT = _C_XBC + SSD_XBC
_C_KVLAT = _C_QLAT + MLA_Q_RANK
_C_KR = _C_KVLAT + MLA_KV_RANK
_C_END = _C_KR + LANES
_KR_DT_LANE = 2 * MLA_ROPE

LOG2_E = 1.4426950408889634
MLA_Q_SCALE = MLA_QK ** -0.5 * LOG2_E


def _in_proj_kernel(x_ref, g_ref, b_ref, w1_ref, h_ref, z_ref, xbc_ref, lat_ref):
    sub = x_ref.shape[0] // ROW_CHAINS
    for r in range(ROW_CHAINS):
        rows = slice(r * sub, (r + 1) * sub)
        hf = _layer_norm(x_ref[rows, :], g_ref[...], b_ref[...])
        h_ref[rows, :] = hf
        h = hf.astype(BF16)
        z_ref[rows, :] = _dot(h, w1_ref[:, _C_Z:_C_XBC]).astype(BF16)
        xbc_ref[rows, :] = _dot(h, w1_ref[:, _C_XBC:_C_QLAT]).astype(BF16)
        lat_ref[rows, :] = _dot(h, w1_ref[:, _C_QLAT:_C_END])


def _split3(x):
    x1 = x.astype(BF16)
    r1 = x - x1.astype(F32)
    x2 = r1.astype(BF16)
    x3 = (r1 - x2.astype(F32)).astype(BF16)
    return [x1, x2, x3]


def _mla_prep_kernel(lat_ref, pos_ref, invf_ref, rep_ref, fold_ref, vone_ref, qg_ref, wq_ref, kvg_ref, wkv_ref,
                     q_ref, k_ref, v_ref):
    tm = lat_ref.shape[0]
    sub = tm // ROW_CHAINS
    chains = [slice(r * sub, (r + 1) * sub) for r in range(ROW_CHAINS)]
    lane = lax.broadcasted_iota(jnp.int32, (sub, LANES), 1)
    kw = MLA_HEADS * HEAD_PAD
    c_kv, c_kr = MLA_Q_RANK, MLA_Q_RANK + MLA_KV_RANK

    ang = pos_ref[...].astype(F32) * invf_ref[...]
    thirds = jnp.concatenate(_split3(jnp.cos(ang)) + _split3(jnp.sin(ang)), axis=1)
    spread = _dot(rep_ref[...], thirds)
    own = (lax.broadcasted_iota(jnp.int32, (tm, LANES), 1) // ROPE_FREQS
           == lax.broadcasted_iota(jnp.int32, (tm, LANES), 0) % (LANES // ROPE_FREQS))
    own_only = jnp.concatenate([jnp.where(own, spread[:, b * LANES:(b + 1) * LANES], 0.0) for b in range(6)],
                               axis=1).astype(BF16)
    cos_sin = _dot(own_only, fold_ref[...])

    qn = [_rms_norm(lat_ref[rows, :c_kv], qg_ref[...]).astype(BF16) for rows in chains]
    kvn = [_rms_norm(lat_ref[rows, c_kv:c_kr], kvg_ref[...]).astype(BF16) for rows in chains]
    kr = [lat_ref[rows, c_kr:] for rows in chains]
    csq, csk = [], []
    for rows in chains:
        cos = cos_sin[rows, :LANES]
        sin = cos_sin[rows, LANES:]
        csq.append(MLA_Q_SCALE * jnp.where(lane < MLA_NOPE, 1.0, jnp.where(lane < MLA_NOPE + MLA_ROPE, cos, sin)))
        csk.append(jnp.where(lane < MLA_ROPE, cos, jnp.where(lane < 2 * MLA_ROPE, sin, 0.0)))

    for r, rows in enumerate(chains):
        q_all = _dot(qn[r], wq_ref[...])
        for hd in range(MLA_HEADS):
            cols = slice(hd * HEAD_PAD, (hd + 1) * HEAD_PAD)
            q_ref[rows, cols] = (q_all[:, cols] * csq[r]).astype(BF16)
        t = kr[r] * csk[r]
        rk = jnp.where((lane >= MLA_ROPE) & (lane < 2 * MLA_ROPE), t + pltpu.roll(t, MLA_ROPE, 1), 0.0)
        kpe = pltpu.roll(rk, MLA_ROPE, 1) + pltpu.roll(rk, 2 * MLA_ROPE, 1)
        kv_all = _dot(kvn[r], wkv_ref[...])
        for hd in range(MLA_HEADS):
            cols = slice(hd * HEAD_PAD, (hd + 1) * HEAD_PAD)
            k_ref[rows, cols] = (kv_all[:, cols] + kpe).astype(BF16)
        v_ref[rows, :] = (kv_all[:, kw:] + vone_ref[...]).astype(BF16)


def _in_proj(x2, ln_g, ln_b, w1):
    tokens, d = x2.shape
    tm = TOKEN_TILE
    lw = _C_END - _C_QLAT

    def row(w):
        return pl.BlockSpec((tm, w), lambda i: (i, 0))

    out_shape = (
        jax.ShapeDtypeStruct((tokens, d), F32),
        jax.ShapeDtypeStruct((tokens, SSD_INNER), BF16),
        jax.ShapeDtypeStruct((tokens, SSD_XBC), BF16),
        jax.ShapeDtypeStruct((tokens, lw), F32),
    )
    return pl.pallas_call(
        _in_proj_kernel,
        out_shape=out_shape,
        grid=(tokens // tm,),
        in_specs=[row(d), _full((1, d)), _full((1, d)), _full(w1.shape)],
        out_specs=(row(d), row(SSD_INNER), row(SSD_XBC), row(lw)),
        compiler_params=_params("parallel"),
        name="in_proj",
    )(x2, ln_g, ln_b, w1)


def _mla_prep(lat, positions, inv_freq, vone, q_norm_g, wq, kv_norm_g, wkv):
    tokens, lw = lat.shape
    tm = MLA_PREP_TILE
    hw = MLA_HEADS * HEAD_PAD
    per_row = LANES // ROPE_FREQS

    def row(w):
        return pl.BlockSpec((tm, w), lambda i: (i, 0))

    pos_c = jnp.repeat(positions.reshape(tokens // per_row, per_row), ROPE_FREQS, axis=1)
    invf = jnp.tile(inv_freq, per_row)[None, :]
    rep = (jnp.arange(tm)[:, None] // per_row == jnp.arange(tm // per_row)[None, :]).astype(BF16)
    src = jnp.arange(6 * LANES)[:, None]
    dst = jnp.arange(2 * LANES)[None, :]
    fold = ((src // (3 * LANES) == dst // LANES) & (src % ROPE_FREQS == dst % ROPE_FREQS)).astype(BF16)

    return pl.pallas_call(
        _mla_prep_kernel,
        out_shape=(jax.ShapeDtypeStruct((tokens, hw), BF16),) * 3,
        grid=(tokens // tm,),
        in_specs=[row(lw), pl.BlockSpec((tm // per_row, LANES), lambda i: (i, 0)), _full((1, LANES)),
                  _full(rep.shape), _full(fold.shape), _full((1, hw)), _full((1, MLA_Q_RANK)), _full(wq.shape),
                  _full((1, MLA_KV_RANK)), _full(wkv.shape)],
        out_specs=(row(hw), row(hw), row(hw)),
        compiler_params=_params("parallel"),
        name="mla_prep",
    )(lat, pos_c, invf, rep, fold, vone, q_norm_g, wq, kv_norm_g, wkv)


def _cumsum_lanes(x, tri):
    x1 = x.astype(BF16).astype(F32)
    r1 = x - x1
    x2 = r1.astype(BF16).astype(F32)
    x3 = r1 - x2
    parts = jnp.concatenate([x1, x2, x3, jnp.zeros_like(x)], axis=0).astype(BF16)
    sums = _dot(parts, tri)
    n = x.shape[0]
    return sums[0:n] + sums[n:2 * n] + sums[2 * n:3 * n]


def _ssd_kernel(xbc_ref, dt_ref, z_ref, shift_ref, tri_ref, cw_ref, cb_ref, dtb_ref, alog_ref, dskip_ref, ng_ref,
                y_ref, tail_ref, state_ref):
    rows = xbc_ref.shape[0]
    L = SSD_CHUNK
    pairs = SSD_HEADS // 2
    pairs_per_group = pairs // SSD_GROUPS

    @pl.when(pl.program_id(1) == 0)
    def _():
        tail_ref[...] = jnp.zeros(tail_ref.shape, F32)
        state_ref[...] = jnp.zeros(state_ref.shape, F32)

    a_head = -jnp.exp(alog_ref[...])
    scalars = []
    for c in range(rows // L):
        dt_in = dt_ref[c * L:(c + 1) * L, :].T[_KR_DT_LANE:_KR_DT_LANE + SSD_HEADS, :] + dtb_ref[...]
        dt_t = jnp.maximum(dt_in, 0.0) + jnp.log1p(jnp.exp(-jnp.abs(dt_in)))
        a_cs_t = _cumsum_lanes(dt_t * a_head, tri_ref[...])
        a_last = a_cs_t[:, L - 1:L]
        c_dec = jnp.exp(jnp.broadcast_to(a_last, a_cs_t.shape))
        w_end_t = dt_t * jnp.exp(a_last - a_cs_t)
        a_cs = jnp.concatenate([a_cs_t, jnp.zeros((L - SSD_HEADS, L), F32)], axis=0).T
        scalars.append((dt_t, a_cs_t, w_end_t, c_dec, a_cs))

    cr = shift_ref.shape[1]
    tail = tail_ref[...]
    row8 = lax.broadcasted_iota(jnp.int32, tail.shape, 0)
    xc_blocks = []
    for blk in range(rows // cr):
        u = xbc_ref[blk * cr:(blk + 1) * cr, :]
        u32 = u.astype(F32)
        sh = _dot(shift_ref[...], u)
        conv = cb_ref[...] + cw_ref[SSD_CONV - 1:SSD_CONV, :] * u32
        for k in range(SSD_CONV - 1):
            conv = conv + cw_ref[k:k + 1, :] * sh[k * cr:(k + 1) * cr, :]
        head_fix = jnp.zeros(tail.shape, F32)
        for k in range(SSD_CONV - 1):
            delay = SSD_CONV - 1 - k
            head_fix = head_fix + cw_ref[k:k + 1, :] * jnp.where(row8 < delay, pltpu.roll(tail, delay, 0), 0.0)
        conv = jnp.concatenate([conv[0:SUBLANES, :] + head_fix, conv[SUBLANES:, :]], axis=0)
        tail = u32[cr - SUBLANES:, :]
        xc_blocks.append(_silu(conv))
    tail_ref[...] = tail

    sq = (L, L)
    row_i = lax.broadcasted_iota(jnp.int32, sq, 0)
    col_i = lax.broadcasted_iota(jnp.int32, sq, 1)
    causal = row_i >= col_i
    left = col_i < HALF
    left_row = lax.broadcasted_iota(jnp.int32, (1, LANES), 1) < HALF

    states = [state_ref[pj] for pj in range(pairs)]
    for c in range(rows // L):
        r0 = c * L
        xc = xc_blocks[r0 // cr][r0 % cr:r0 % cr + L, :]
        xs = xc[:, 0:SSD_INNER]
        bm = xc[:, SSD_INNER:SSD_INNER + SSD_GROUPS * SSD_STATE]
        cm = xc[:, SSD_INNER + SSD_GROUPS * SSD_STATE:]
        dt_t, a_cs_t, w_end_t, c_dec, a_cs = scalars[c]

        y_pairs = []
        for g in range(SSD_GROUPS):
            b_g = bm[:, g * SSD_STATE:(g + 1) * SSD_STATE]
            c_g = cm[:, g * SSD_STATE:(g + 1) * SSD_STATE]
            b_gt = b_g.T
            cb = _dot(c_g.astype(BF16), b_gt.astype(BF16))
            for j in range(pairs_per_group):
                pj = g * pairs_per_group + j
                x_p = xs[:, pj * LANES:(pj + 1) * LANES]
                st = states[pj]
                y_p = x_p * dskip_ref[:, pj * LANES:(pj + 1) * LANES]
                new = None
                for hh in range(2):
                    hd = 2 * pj + hh
                    cs_l = jnp.broadcast_to(a_cs[:, hd:hd + 1], sq)
                    m = cb * jnp.where(causal, jnp.exp(cs_l - a_cs_t[hd:hd + 1, :]), 0.0) * dt_t[hd:hd + 1, :]
                    lhs = jnp.concatenate([m, c_g * jnp.exp(cs_l)], axis=1).astype(BF16)
                    keep = left if hh == 0 else jnp.logical_not(left)
                    x_h = jnp.where(keep, x_p, 0.0).astype(BF16)
                    rhs = jnp.concatenate([x_h, jnp.where(keep, st, 0.0).astype(BF16)], axis=0)
                    y_p = y_p + _dot(lhs, rhs)
                    part = _dot((b_gt * w_end_t[hd:hd + 1, :]).astype(BF16), x_h)
                    new = part if new is None else new + part
                dec = jnp.where(left_row, c_dec[2 * pj:2 * pj + 1, :], c_dec[2 * pj + 1:2 * pj + 2, :])
                states[pj] = st * dec + new
                y_pairs.append(y_p)

        gw = SSD_INNER // SSD_GROUPS
        zt = z_ref[r0:r0 + L, :].astype(F32)
        for g in range(SSD_GROUPS):
            y_g = jnp.concatenate(y_pairs[g * pairs_per_group:(g + 1) * pairs_per_group], axis=1)
            y_g = y_g * _silu(zt[:, g * gw:(g + 1) * gw])
            y_g = _rms_norm(y_g, ng_ref[:, g * gw:(g + 1) * gw])
            y_ref[r0:r0 + L, g * gw:(g + 1) * gw] = y_g.astype(BF16)
    for pj in range(pairs):
        state_ref[pj] = states[pj]


def _ssd(xbc, dt, z, conv_w, conv_b, dt_bias, a_log, d_skip, norm_g, batch, seq):
    rows = SSD_ROWS
    steps = seq // rows

    def tok(w):
        return pl.BlockSpec((rows, w), lambda b, c: (b * steps + c, 0))

    t_out = jnp.arange(SSD_CONV_ROWS)[:, None]
    t_in = jnp.arange(SSD_CONV_ROWS)[None, :]
    shift = jnp.concatenate([(t_out - t_in == SSD_CONV - 1 - k) for k in range(SSD_CONV - 1)], axis=0).astype(BF16)
    tri = (jnp.arange(SSD_CHUNK)[:, None] <= jnp.arange(SSD_CHUNK)[None, :]).astype(BF16)

    dt_block = dt.shape[1] // LANES - 1
    pairs = SSD_HEADS // 2
    return pl.pallas_call(
        _ssd_kernel,
        out_shape=jax.ShapeDtypeStruct((batch * seq, SSD_INNER), BF16),
        grid=(batch, steps),
        in_specs=[tok(SSD_XBC), pl.BlockSpec((rows, LANES), lambda b, c: (b * steps + c, dt_block)),
                  tok(SSD_INNER), _full(shift.shape), _full(tri.shape),
                  _full(conv_w.shape), _full(conv_b.shape), _full(dt_bias.shape), _full(a_log.shape),
                  _full(d_skip.shape), _full(norm_g.shape)],
        out_specs=tok(SSD_INNER),
        scratch_shapes=[pltpu.VMEM((SUBLANES, SSD_XBC), F32),
                        pltpu.VMEM((pairs, SSD_STATE, LANES), F32)],
        compiler_params=_params("parallel", "arbitrary"),
        name="ssd",
    )(xbc, dt, z, shift, tri, conv_w, conv_b, dt_bias, a_log, d_skip, norm_g)


def _mla_attn_kernel(q_ref, k_ref, v_ref, o_ref, m_ref, acc_ref):
    seq = q_ref.shape[0]
    t = ATTN_TILE
    tq = ATTN_Q_SUB * t
    n_heads = q_ref.shape[1] // HEAD_PAD
    causal = [lax.broadcasted_iota(jnp.int32, (tq - d * t, t), 0) >= lax.broadcasted_iota(jnp.int32, (tq - d * t, t), 1)
              for d in range(ATTN_Q_SUB)]
    lane = lax.broadcasted_iota(jnp.int32, (tq, LANES), 1)
    heads = tuple(slice(hh * HEAD_PAD, (hh + 1) * HEAD_PAD) for hh in range(n_heads))

    def tile_update(qi, hh, r0, q, k0, width, mask, first=False):
        c = heads[hh]
        s = _dot_nt(q, k_ref[k0:k0 + width, c])
        if mask is not None:
            s = jnp.where(mask, s, -jnp.inf)
        m_new = jnp.max(s, axis=-1, keepdims=True)
        if first:
            m_new = jnp.broadcast_to(m_new, (s.shape[0], LANES))
        else:
            m = m_ref[qi, hh, r0:, :]
            m_new = jnp.maximum(m, m_new)
        p = jnp.exp2(s - jnp.concatenate([m_new] * (width // LANES), axis=1)).astype(BF16)
        pv = _dot(p, v_ref[k0:k0 + width, c])
        acc_ref[qi, hh, r0:, :] = pv if first else jnp.exp2(m - m_new) * acc_ref[qi, hh, r0:, :] + pv
        m_ref[qi, hh, r0:, :] = m_new

    for pair in range(n_heads // 2):
        pair_heads = (2 * pair, 2 * pair + 1)
        for qi in range(seq // tq):
            q0 = qi * tq
            qs = {hh: q_ref[q0:q0 + tq, heads[hh]] for hh in pair_heads}
            for d in range(ATTN_Q_SUB):
                for hh in pair_heads:
                    tile_update(qi, hh, d * t, qs[hh][d * t:], q0 + d * t, t, causal[d], first=(d == 0))
            for kb in range(qi):
                for hh in pair_heads:
                    tile_update(qi, hh, 0, qs[hh], kb * tq, tq, None)
            acc_a, acc_b = acc_ref[qi, pair_heads[0]], acc_ref[qi, pair_heads[1]]
            l_a = acc_a[:, HALF:HALF + 1]
            l_b = acc_b[:, 0:1]
            out = jnp.where(lane < HALF, acc_a * (1.0 / l_a), acc_b * (1.0 / l_b))
            o_ref[q0:q0 + tq, pair * LANES:(pair + 1) * LANES] = out.astype(BF16)


def _mla_attn(q, k, v, batch, seq):
    n_heads = 2 * ATTN_PAIRS
    steps = MLA_HEADS // n_heads
    tq = ATTN_Q_SUB * ATTN_TILE
    spec = pl.BlockSpec((seq, n_heads * HEAD_PAD), lambda b, p: (b, p))
    return pl.pallas_call(
        _mla_attn_kernel,
        out_shape=jax.ShapeDtypeStruct((batch * seq, MLA_HEADS * MLA_V), BF16),
        grid=(batch, steps),
        in_specs=[spec, spec, spec],
        out_specs=pl.BlockSpec((seq, ATTN_PAIRS * LANES), lambda b, p: (b, p)),
        scratch_shapes=[pltpu.VMEM((seq // tq, n_heads, tq, LANES), F32)] * 2,
        compiler_params=_params("parallel", "parallel"),
        name="mla_attn",
    )(q, k, v)


def _mem_kv_kernel(mem_ref, wk_ref, wv_ref, k_ref, v_ref):
    m = mem_ref[...].astype(BF16)
    k_ref[...] = _dot(m, wk_ref[...]).astype(BF16)
    v_ref[...] = _dot(m, wv_ref[...]).astype(BF16)


def _mem_kv(mem2, wk, wv, batch, mem_tokens):
    d = mem2.shape[1]
    spec = pl.BlockSpec((mem_tokens, d), lambda b: (b, 0))
    return pl.pallas_call(
        _mem_kv_kernel,
        out_shape=(jax.ShapeDtypeStruct(mem2.shape, BF16),) * 2,
        grid=(batch,),
        in_specs=[spec, _full(wk.shape), _full(wv.shape)],
        out_specs=(spec, spec),
        compiler_params=_params("parallel"),
        name="mem_kv",
    )(mem2, wk, wv)


def _mix_xattn_kernel(h_ref, y_ref, o_ref, km_ref, vm_ref, wout_ref, g1_ref, b1_ref,
                      wq_ref, wo_ref, pre2_ref):
    d = h_ref.shape[1]
    hd_w = d // MEM_HEADS
    scale = hd_w ** -0.5
    sub = h_ref.shape[0] // ROW_CHAINS
    chains = [slice(r * sub, (r + 1) * sub) for r in range(ROW_CHAINS)]
    mix = [_dot(y_ref[rows, :], wout_ref[0:SSD_INNER, :]) + _dot(o_ref[rows, :], wout_ref[SSD_INNER:, :])
           for rows in chains]
    h1 = [_layer_norm(DEEPNORM_ALPHA * h_ref[rows, :] + mix[r], g1_ref[...], b1_ref[...])
          for r, rows in enumerate(chains)]
    h1b = [v.astype(BF16) for v in h1]
    xa = [None] * ROW_CHAINS
    for hd in range(MEM_HEADS):
        cols = slice(hd * hd_w, (hd + 1) * hd_w)
        q = [(_dot(v, wq_ref[:, cols]) * scale).astype(BF16) for v in h1b]
        s = [_dot_nt(v, km_ref[:, cols]) for v in q]
        p = [jnp.exp(v - jnp.max(v, axis=-1, keepdims=True)) for v in s]
        p = [(v * (1.0 / jnp.sum(v, axis=-1, keepdims=True))).astype(BF16) for v in p]
        oh = [_dot(v, vm_ref[:, cols]).astype(BF16) for v in p]
        for r in range(ROW_CHAINS):
            part = _dot(oh[r], wo_ref[cols, :])
            xa[r] = part if xa[r] is None else xa[r] + part
    for r, rows in enumerate(chains):
        pre2_ref[rows, :] = DEEPNORM_ALPHA * h1[r] + xa[r]


def _mix_xattn(h, y, o, k_mem, v_mem, w_out, ln1_g, ln1_b, wq, wo, seq, mem_tokens):
    tokens, d = h.shape
    tm = TOKEN_TILE
    per_batch = seq // tm

    def row(w):
        return pl.BlockSpec((tm, w), lambda i: (i, 0))

    mem_spec = pl.BlockSpec((mem_tokens, d), lambda i: (i // per_batch, 0))
    vec = _full((1, d))
    return pl.pallas_call(
        _mix_xattn_kernel,
        out_shape=jax.ShapeDtypeStruct((tokens, d), F32),
        grid=(tokens // tm,),
        in_specs=[row(d), row(y.shape[1]), row(o.shape[1]), mem_spec, mem_spec,
                  _full(w_out.shape), vec, vec, _full(wq.shape), _full(wo.shape)],
        out_specs=row(d),
        compiler_params=_params("parallel"),
        name="mix_xattn",
    )(h, y, o, k_mem, v_mem, w_out, ln1_g, ln1_b, wq, wo)


def _mlp_kernel(pre2_ref, g2_ref, b2_ref, wup_ref, wdn_ref, g_ref, b_ref, out_ref):
    sub = pre2_ref.shape[0] // MLP_CHAINS
    chains = [slice(r * sub, (r + 1) * sub) for r in range(MLP_CHAINS)]
    h2 = [_layer_norm(pre2_ref[rows, :], g2_ref[...], b2_ref[...]) for rows in chains]
    hb = [v.astype(BF16) for v in h2]
    ff = [None] * MLP_CHAINS
    for c in range(wup_ref.shape[1] // FF_CHUNK):
        cols = slice(c * FF_CHUNK, (c + 1) * FF_CHUNK)
        for r in range(MLP_CHAINS):
            u = jnp.maximum(_dot(hb[r], wup_ref[:, cols]), 0.0)
            part = _dot((u * u).astype(BF16), wdn_ref[cols, :])
            ff[r] = part if ff[r] is None else ff[r] + part
    for r, rows in enumerate(chains):
        out_ref[rows, :] = _layer_norm(DEEPNORM_ALPHA * h2[r] + ff[r], g_ref[...], b_ref[...])


def _mlp(pre2, ln2_g, ln2_b, w_up, w_down, ln_g, ln_b):
    tokens, d = pre2.shape
    tm = TOKEN_TILE
    row = pl.BlockSpec((tm, d), lambda i: (i, 0))
    vec = _full((1, d))
    return pl.pallas_call(
        _mlp_kernel,
        out_shape=jax.ShapeDtypeStruct((tokens, d), F32),
        grid=(tokens // tm,),
        in_specs=[row, vec, vec, _full(w_up.shape), _full(w_down.shape), vec, vec],
        out_specs=row,
        compiler_params=_params("parallel"),
        name="mlp",
    )(pre2, ln2_g, ln2_b, w_up, w_down, ln_g, ln_b)


def _rot_cols(w):
    half = MLA_ROPE // 2
    return jnp.concatenate([-w[..., half:], w[..., :half]], axis=-1)


def _pack_in_proj(w_in):
    o = 0
    segs = {}
    for name, width in (("z", SSD_INNER), ("xbc", SSD_XBC), ("dt", SSD_HEADS), ("q", MLA_Q_RANK),
                        ("kv", MLA_KV_RANK), ("kr", MLA_ROPE)):
        segs[name] = w_in[:, o:o + width]
        o += width
    kr, krs = segs["kr"], _rot_cols(segs["kr"])
    dt = jnp.pad(segs["dt"], ((0, 0), (0, LANES - _KR_DT_LANE - SSD_HEADS)))
    return jnp.concatenate([segs["z"], segs["xbc"], segs["q"], segs["kv"], kr, krs, dt], axis=1).astype(BF16)


def _pack_q_up(w_q_up):
    r = w_q_up.shape[0]
    w = w_q_up.reshape(r, MLA_HEADS, MLA_QK)
    rope = w[:, :, MLA_NOPE:]
    return jnp.concatenate([w[:, :, :MLA_NOPE], rope, _rot_cols(rope)], axis=-1).reshape(
        r, MLA_HEADS * HEAD_PAD).astype(BF16)


def _pack_kv_up(w_kv_up):
    r = w_kv_up.shape[0]
    w = w_kv_up.reshape(r, MLA_HEADS, MLA_NOPE + MLA_V)
    zeros = jnp.zeros((r, MLA_HEADS, HEAD_PAD - MLA_NOPE), w.dtype)
    wk = jnp.concatenate([w[:, :, :MLA_NOPE], zeros], axis=-1)
    wv = w[:, :, MLA_NOPE:]
    zv = jnp.zeros_like(wv)
    even = (jnp.arange(MLA_HEADS) % 2 == 0)[None, :, None]
    wv = jnp.concatenate([jnp.where(even, wv, zv), jnp.where(even, zv, wv)], axis=-1)
    return jnp.concatenate([wk.reshape(r, -1), wv.reshape(r, -1)], axis=1).astype(BF16)


def kernel(x, mem, positions, ln_in_g, ln_in_b, w_in, conv_w, conv_b, dt_bias, a_log, d_skip, ssd_norm_g, q_norm_g, w_q_up, kv_norm_g, w_kv_up, w_mix_out, ln1_g, ln1_b, w_mem_q, w_mem_k, w_mem_v, w_mem_o, ln2_g, ln2_b, w_up, w_down, ln3_g, ln3_b):
    batch, seq, d = x.shape
    mem_tokens = mem.shape[1]
    tokens = batch * seq
    assert w_in.shape[0] == DEPTH == 1
    assert seq % (ATTN_TILE * ATTN_Q_SUB) == 0 and seq % TOKEN_TILE == 0 and seq % SSD_ROWS == 0

    def vec(v):
        return v.reshape(1, -1).astype(F32)

    def head_rows(v):
        return jnp.broadcast_to(v.astype(F32)[:, None], (v.size, LANES))

    x2 = x.reshape(tokens, d)

    inv_freq = jnp.power(ROPE_THETA, -jnp.arange(ROPE_FREQS, dtype=F32) / ROPE_FREQS)
    head_lane = jnp.arange(MLA_HEADS * HEAD_PAD) % HEAD_PAD
    head_odd = (jnp.arange(MLA_HEADS * HEAD_PAD) // HEAD_PAD) % 2
    vone = (head_lane == jnp.where(head_odd == 1, 0, HALF)).astype(F32)[None, :]

    h, z, xbc, lat = _in_proj(x2, vec(ln_in_g), vec(ln_in_b), _pack_in_proj(w_in[0]))
    q, k, v = _mla_prep(lat, positions, inv_freq, vone, vec(q_norm_g[0]), _pack_q_up(w_q_up[0]),
                        vec(kv_norm_g[0]), _pack_kv_up(w_kv_up[0]))

    y = _ssd(xbc, lat, z, conv_w[0].astype(F32), vec(conv_b[0]), head_rows(dt_bias[0]), head_rows(a_log[0]),
             vec(jnp.repeat(d_skip[0], SSD_HEAD_DIM)), vec(ssd_norm_g[0]), batch, seq)
    o = _mla_attn(q, k, v, batch, seq)

    k_mem, v_mem = _mem_kv(mem.reshape(batch * mem_tokens, d), w_mem_k[0].astype(BF16),
                           w_mem_v[0].astype(BF16), batch, mem_tokens)
    pre2 = _mix_xattn(h, y, o, k_mem, v_mem, w_mix_out[0].astype(BF16),
                      vec(ln1_g[0]), vec(ln1_b[0]), w_mem_q[0].astype(BF16), w_mem_o[0].astype(BF16),
                      seq, mem_tokens)
    out = _mlp(pre2, vec(ln2_g[0]), vec(ln2_b[0]), w_up[0].astype(BF16), w_down[0].astype(BF16),
               vec(ln3_g[0]), vec(ln3_b[0]))
    return out.reshape(batch, seq, d)
```

```python
import jax
import jax.numpy as jnp
from jax import lax
from jax.experimental import pallas as pl
from jax.experimental.pallas import tpu as pltpu

F32 = jnp.float32
BF16 = jnp.bfloat16

SSD_HEADS = 8
SSD_HEAD_DIM = 64
SSD_INNER = SSD_HEADS * SSD_HEAD_DIM
SSD_GROUPS = 2
SSD_STATE = 128
SSD_CONV = 4
SSD_CHUNK = 128
SSD_XBC = SSD_INNER + 2 * SSD_GROUPS * SSD_STATE
MLA_HEADS = 8
MLA_NOPE = 64
MLA_ROPE = 32
MLA_QK = MLA_NOPE + MLA_ROPE
MLA_V = 64
MLA_Q_RANK = 384
MLA_KV_RANK = 256
ROPE_THETA = 10000.0
ROPE_FREQS = MLA_ROPE // 2
MEM_HEADS = 4
LN_EPS = 1e-5
RMS_EPS = 1e-6
DEPTH = 1
DEEPNORM_ALPHA = (2.0 * DEPTH) ** 0.25

LANES = 128
SUBLANES = 8
VMEM_LIMIT_BYTES = 56 * 1024 * 1024

TOKEN_TILE = 1024
ROW_CHAINS = 4
MLA_PREP_TILE = 2048
SSD_ROWS = 2048
SSD_CONV_ROWS = 256
ATTN_TILE = 512
ATTN_Q_SUB = 2
FF_CHUNK = 2048
MLP_CHAINS = 4

HEAD_PAD = LANES
HALF = LANES // 2


def _params(*semantics):
    return pltpu.CompilerParams(dimension_semantics=semantics, vmem_limit_bytes=VMEM_LIMIT_BYTES)


def _full(shape):
    zeros = (0,) * len(shape)
    return pl.BlockSpec(shape, lambda *_: zeros, pipeline_mode=pl.Buffered(1))


def _layer_norm(x, g, b):
    mu = jnp.mean(x, axis=-1, keepdims=True)
    xc = x - mu
    var = jnp.mean(xc * xc, axis=-1, keepdims=True)
    return xc * lax.rsqrt(var + LN_EPS) * g + b


def _rms_norm(x, g):
    ms = jnp.mean(x * x, axis=-1, keepdims=True)
    return x * lax.rsqrt(ms + RMS_EPS) * g


def _silu(x):
    hx = 0.5 * x
    return hx + hx * jnp.tanh(hx)


def _dot(a, b):
    return jnp.dot(a, b, preferred_element_type=F32)


def _dot_nt(a, b):
    return lax.dot_general(a, b, (((1,), (1,)), ((), ())), preferred_element_type=F32)


_C_Z = 0
_C_XBC = _C_Z + SSD_INNER
_C_QLAT = _C_XBC + SSD_XBC
_C_KVLAT = _C_QLAT + MLA_Q_RANK
_C_KR = _C_KVLAT + MLA_KV_RANK
_C_END = _C_KR + LANES
_KR_DT_LANE = 2 * MLA_ROPE

LOG2_E = 1.4426950408889634
MLA_Q_SCALE = MLA_QK ** -0.5 * LOG2_E


def _in_proj_kernel(x_ref, g_ref, b_ref, w1_ref, h_ref, z_ref, xbc_ref, lat_ref):
    sub = x_ref.shape[0] // ROW_CHAINS
    for r in range(ROW_CHAINS):
        rows = slice(r * sub, (r + 1) * sub)
        hf = _layer_norm(x_ref[rows, :], g_ref[...], b_ref[...])
        h_ref[rows, :] = hf
        h = hf.astype(BF16)
        z_ref[rows, :] = _dot(h, w1_ref[:, _C_Z:_C_XBC]).astype(BF16)
        xbc_ref[rows, :] = _dot(h, w1_ref[:, _C_XBC:_C_QLAT]).astype(BF16)
        lat_ref[rows, :] = _dot(h, w1_ref[:, _C_QLAT:_C_END])


def _split3(x):
    x1 = x.astype(BF16)
    r1 = x - x1.astype(F32)
    x2 = r1.astype(BF16)
    x3 = (r1 - x2.astype(F32)).astype(BF16)
    return [x1, x2, x3]


def _mla_prep_kernel(lat_ref, pos_ref, invf_ref, rep_ref, fold_ref, vone_ref, qg_ref, wq_ref, kvg_ref, wkv_ref,
                     q_ref, k_ref, v_ref):
    tm = lat_ref.shape[0]
    sub = tm // ROW_CHAINS
    chains = [slice(r * sub, (r + 1) * sub) for r in range(ROW_CHAINS)]
    lane = lax.broadcasted_iota(jnp.int32, (sub, LANES), 1)
    kw = MLA_HEADS * HEAD_PAD
    c_kv, c_kr = MLA_Q_RANK, MLA_Q_RANK + MLA_KV_RANK

    ang = pos_ref[...].astype(F32) * invf_ref[...]
    thirds = jnp.concatenate(_split3(jnp.cos(ang)) + _split3(jnp.sin(ang)), axis=1)
    spread = _dot(rep_ref[...], thirds)
    own = (lax.broadcasted_iota(jnp.int32, (tm, LANES), 1) // ROPE_FREQS
           == lax.broadcasted_iota(jnp.int32, (tm, LANES), 0) % (LANES // ROPE_FREQS))
    own_only = jnp.concatenate([jnp.where(own, spread[:, b * LANES:(b + 1) * LANES], 0.0) for b in range(6)],
                               axis=1).astype(BF16)
    cos_sin = _dot(own_only, fold_ref[...])

    qn = [_rms_norm(lat_ref[rows, :c_kv], qg_ref[...]).astype(BF16) for rows in chains]
    kvn = [_rms_norm(lat_ref[rows, c_kv:c_kr], kvg_ref[...]).astype(BF16) for rows in chains]
    kr = [lat_ref[rows, c_kr:] for rows in chains]
    csq, csk = [], []
    for rows in chains:
        cos = cos_sin[rows, :LANES]
        sin = cos_sin[rows, LANES:]
        csq.append(MLA_Q_SCALE * jnp.where(lane < MLA_NOPE, 1.0, jnp.where(lane < MLA_NOPE + MLA_ROPE, cos, sin)))
        csk.append(jnp.where(lane < MLA_ROPE, cos, jnp.where(lane < 2 * MLA_ROPE, sin, 0.0)))

    for r, rows in enumerate(chains):
        q_all = _dot(qn[r], wq_ref[...])
        for hd in range(MLA_HEADS):
            cols = slice(hd * HEAD_PAD, (hd + 1) * HEAD_PAD)
            q_ref[rows, cols] = (q_all[:, cols] * csq[r]).astype(BF16)
        t = kr[r] * csk[r]
        rk = jnp.where((lane >= MLA_ROPE) & (lane < 2 * MLA_ROPE), t + pltpu.roll(t, MLA_ROPE, 1), 0.0)
        kpe = pltpu.roll(rk, MLA_ROPE, 1) + pltpu.roll(rk, 2 * MLA_ROPE, 1)
        kv_all = _dot(kvn[r], wkv_ref[...])
        for hd in range(MLA_HEADS):
            cols = slice(hd * HEAD_PAD, (hd + 1) * HEAD_PAD)
            k_ref[rows, cols] = (kv_all[:, cols] + kpe).astype(BF16)
        v_ref[rows, :] = (kv_all[:, kw:] + vone_ref[...]).astype(BF16)


def _in_proj(x2, ln_g, ln_b, w1):
    tokens, d = x2.shape
    tm = TOKEN_TILE
    lw = _C_END - _C_QLAT

    def row(w):
        return pl.BlockSpec((tm, w), lambda i: (i, 0))

    out_shape = (
        jax.ShapeDtypeStruct((tokens, d), F32),
        jax.ShapeDtypeStruct((tokens, SSD_INNER), BF16),
        jax.ShapeDtypeStruct((tokens, SSD_XBC), BF16),
        jax.ShapeDtypeStruct((tokens, lw), F32),
    )
    return pl.pallas_call(
        _in_proj_kernel,
        out_shape=out_shape,
        grid=(tokens // tm,),
        in_specs=[row(d), _full((1, d)), _full((1, d)), _full(w1.shape)],
        out_specs=(row(d), row(SSD_INNER), row(SSD_XBC), row(lw)),
        compiler_params=_params("parallel"),
        name="in_proj",
    )(x2, ln_g, ln_b, w1)


def _mla_prep(lat, positions, inv_freq, vone, q_norm_g, wq, kv_norm_g, wkv):
    tokens, lw = lat.shape
    tm = MLA_PREP_TILE
    hw = MLA_HEADS * HEAD_PAD
    per_row = LANES // ROPE_FREQS

    def row(w):
        return pl.BlockSpec((tm, w), lambda i: (i, 0))

    pos_c = jnp.repeat(positions.reshape(tokens // per_row, per_row), ROPE_FREQS, axis=1)
    invf = jnp.tile(inv_freq, per_row)[None, :]
    rep = (jnp.arange(tm)[:, None] // per_row == jnp.arange(tm // per_row)[None, :]).astype(BF16)
    src = jnp.arange(6 * LANES)[:, None]
    dst = jnp.arange(2 * LANES)[None, :]
    fold = ((src // (3 * LANES) == dst // LANES) & (src % ROPE_FREQS == dst % ROPE_FREQS)).astype(BF16)

    return pl.pallas_call(
        _mla_prep_kernel,
        out_shape=(jax.ShapeDtypeStruct((tokens, hw), BF16),) * 3,
        grid=(tokens // tm,),
        in_specs=[row(lw), pl.BlockSpec((tm // per_row, LANES), lambda i: (i, 0)), _full((1, LANES)),
                  _full(rep.shape), _full(fold.shape), _full((1, hw)), _full((1, MLA_Q_RANK)), _full(wq.shape),
                  _full((1, MLA_KV_RANK)), _full(wkv.shape)],
        out_specs=(row(hw), row(hw), row(hw)),
        compiler_params=_params("parallel"),
        name="mla_prep",
    )(lat, pos_c, invf, rep, fold, vone, q_norm_g, wq, kv_norm_g, wkv)


def _cumsum_lanes(x, tri):
    x1 = x.astype(BF16).astype(F32)
    r1 = x - x1
    x2 = r1.astype(BF16).astype(F32)
    x3 = r1 - x2
    parts = jnp.concatenate([x1, x2, x3, jnp.zeros_like(x)], axis=0).astype(BF16)
    sums = _dot(parts, tri)
    n = x.shape[0]
    return sums[0:n] + sums[n:2 * n] + sums[2 * n:3 * n]


def _ssd_kernel(xbc_ref, dt_ref, z_ref, shift_ref, tri_ref, cw_ref, cb_ref, dtb_ref, alog_ref, dskip_ref, ng_ref,
                y_ref, tail_ref, state_ref):
    rows = xbc_ref.shape[0]
    L = SSD_CHUNK
    pairs = SSD_HEADS // 2
    pairs_per_group = pairs // SSD_GROUPS

    @pl.when(pl.program_id(1) == 0)
    def _():
        tail_ref[...] = jnp.zeros(tail_ref.shape, F32)
        state_ref[...] = jnp.zeros(state_ref.shape, F32)

    a_head = -jnp.exp(alog_ref[...])
    scalars = []
    for c in range(rows // L):
        dt_in = dt_ref[c * L:(c + 1) * L, :].T[_KR_DT_LANE:_KR_DT_LANE + SSD_HEADS, :] + dtb_ref[...]
        dt_t = jnp.maximum(dt_in, 0.0) + jnp.log1p(jnp.exp(-jnp.abs(dt_in)))
        a_cs_t = _cumsum_lanes(dt_t * a_head, tri_ref[...])
        a_last = a_cs_t[:, L - 1:L]
        c_dec = jnp.exp(jnp.broadcast_to(a_last, a_cs_t.shape))
        w_end_t = dt_t * jnp.exp(a_last - a_cs_t)
        a_cs = jnp.concatenate([a_cs_t, jnp.zeros((L - SSD_HEADS, L), F32)], axis=0).T
        scalars.append((dt_t, a_cs_t, w_end_t, c_dec, a_cs))

    cr = shift_ref.shape[1]
    tail = tail_ref[...]
    row8 = lax.broadcasted_iota(jnp.int32, tail.shape, 0)
    xc_blocks = []
    for blk in range(rows // cr):
        u = xbc_ref[blk * cr:(blk + 1) * cr, :]
        u32 = u.astype(F32)
        sh = _dot(shift_ref[...], u)
        conv = cb_ref[...] + cw_ref[SSD_CONV - 1:SSD_CONV, :] * u32
        for k in range(SSD_CONV - 1):
            conv = conv + cw_ref[k:k + 1, :] * sh[k * cr:(k + 1) * cr, :]
        head_fix = jnp.zeros(tail.shape, F32)
        for k in range(SSD_CONV - 1):
            delay = SSD_CONV - 1 - k
            head_fix = head_fix + cw_ref[k:k + 1, :] * jnp.where(row8 < delay, pltpu.roll(tail, delay, 0), 0.0)
        conv = jnp.concatenate([conv[0:SUBLANES, :] + head_fix, conv[SUBLANES:, :]], axis=0)
        tail = u32[cr - SUBLANES:, :]
        xc_blocks.append(_silu(conv))
    tail_ref[...] = tail

    sq = (L, L)
    row_i = lax.broadcasted_iota(jnp.int32, sq, 0)
    col_i = lax.broadcasted_iota(jnp.int32, sq, 1)
    causal = row_i >= col_i
    left = col_i < HALF
    left_row = lax.broadcasted_iota(jnp.int32, (1, LANES), 1) < HALF

    states = [state_ref[pj] for pj in range(pairs)]
    for c in range(rows // L):
        r0 = c * L
        xc = xc_blocks[r0 // cr][r0 % cr:r0 % cr + L, :]
        xs = xc[:, 0:SSD_INNER]
        bm = xc[:, SSD_INNER:SSD_INNER + SSD_GROUPS * SSD_STATE]
        cm = xc[:, SSD_INNER + SSD_GROUPS * SSD_STATE:]
        dt_t, a_cs_t, w_end_t, c_dec, a_cs = scalars[c]

        y_pairs = []
        for g in range(SSD_GROUPS):
            b_g = bm[:, g * SSD_STATE:(g + 1) * SSD_STATE]
            c_g = cm[:, g * SSD_STATE:(g + 1) * SSD_STATE]
            b_gt = b_g.T
            cb = _dot(c_g.astype(BF16), b_gt.astype(BF16))
            for j in range(pairs_per_group):
                pj = g * pairs_per_group + j
                x_p = xs[:, pj * LANES:(pj + 1) * LANES]
                st = states[pj]
                y_p = x_p * dskip_ref[:, pj * LANES:(pj + 1) * LANES]
                new = None
                for hh in range(2):
                    hd = 2 * pj + hh
                    cs_l = jnp.broadcast_to(a_cs[:, hd:hd + 1], sq)
                    m = cb * jnp.where(causal, jnp.exp(cs_l - a_cs_t[hd:hd + 1, :]), 0.0) * dt_t[hd:hd + 1, :]
                    lhs = jnp.concatenate([m, c_g * jnp.exp(cs_l)], axis=1).astype(BF16)
                    keep = left if hh == 0 else jnp.logical_not(left)
                    x_h = jnp.where(keep, x_p, 0.0).astype(BF16)
                    rhs = jnp.concatenate([x_h, jnp.where(keep, st, 0.0).astype(BF16)], axis=0)
                    y_p = y_p + _dot(lhs, rhs)
                    part = _dot((b_gt * w_end_t[hd:hd + 1, :]).astype(BF16), x_h)
                    new = part if new is None else new + part
                dec = jnp.where(left_row, c_dec[2 * pj:2 * pj + 1, :], c_dec[2 * pj + 1:2 * pj + 2, :])
                states[pj] = st * dec + new
                y_pairs.append(y_p)

        gw = SSD_INNER // SSD_GROUPS
        zt = z_ref[r0:r0 + L, :].astype(F32)
        for g in range(SSD_GROUPS):
            y_g = jnp.concatenate(y_pairs[g * pairs_per_group:(g + 1) * pairs_per_group], axis=1)
            y_g = y_g * _silu(zt[:, g * gw:(g + 1) * gw])
            y_g = _rms_norm(y_g, ng_ref[:, g * gw:(g + 1) * gw])
            y_ref[r0:r0 + L, g * gw:(g + 1) * gw] = y_g.astype(BF16)
    for pj in range(pairs):
        state_ref[pj] = states[pj]


def _ssd(xbc, dt, z, conv_w, conv_b, dt_bias, a_log, d_skip, norm_g, batch, seq):
    rows = SSD_ROWS
    steps = seq // rows

    def tok(w):
        return pl.BlockSpec((rows, w), lambda b, c: (b * steps + c, 0))

    t_out = jnp.arange(SSD_CONV_ROWS)[:, None]
    t_in = jnp.arange(SSD_CONV_ROWS)[None, :]
    shift = jnp.concatenate([(t_out - t_in == SSD_CONV - 1 - k) for k in range(SSD_CONV - 1)], axis=0).astype(BF16)
    tri = (jnp.arange(SSD_CHUNK)[:, None] <= jnp.arange(SSD_CHUNK)[None, :]).astype(BF16)

    dt_block = dt.shape[1] // LANES - 1
    pairs = SSD_HEADS // 2
    return pl.pallas_call(
        _ssd_kernel,
        out_shape=jax.ShapeDtypeStruct((batch * seq, SSD_INNER), BF16),
        grid=(batch, steps),
        in_specs=[tok(SSD_XBC), pl.BlockSpec((rows, LANES), lambda b, c: (b * steps + c, dt_block)),
                  tok(SSD_INNER), _full(shift.shape), _full(tri.shape),
                  _full(conv_w.shape), _full(conv_b.shape), _full(dt_bias.shape), _full(a_log.shape),
                  _full(d_skip.shape), _full(norm_g.shape)],
        out_specs=tok(SSD_INNER),
        scratch_shapes=[pltpu.VMEM((SUBLANES, SSD_XBC), F32),
                        pltpu.VMEM((pairs, SSD_STATE, LANES), F32)],
        compiler_params=_params("parallel", "arbitrary"),
        name="ssd",
    )(xbc, dt, z, shift, tri, conv_w, conv_b, dt_bias, a_log, d_skip, norm_g)


def _mla_attn_kernel(q_ref, k_ref, v_ref, o_ref, m_ref, acc_ref):
    seq = q_ref.shape[0]
    t = ATTN_TILE
    tq = ATTN_Q_SUB * t
    causal = [lax.broadcasted_iota(jnp.int32, (tq - d * t, t), 0) >= lax.broadcasted_iota(jnp.int32, (tq - d * t, t), 1)
              for d in range(ATTN_Q_SUB)]
    lane = lax.broadcasted_iota(jnp.int32, (tq, LANES), 1)
    heads = tuple(slice(hh * HEAD_PAD, (hh + 1) * HEAD_PAD) for hh in range(2))

    def tile_update(qi, hh, r0, q, k0, width, mask, first=False):
        c = heads[hh]
        s = _dot_nt(q, k_ref[k0:k0 + width, c])
        if mask is not None:
            s = jnp.where(mask, s, -jnp.inf)
        m_new = jnp.max(s, axis=-1, keepdims=True)
        if first:
            m_new = jnp.broadcast_to(m_new, (s.shape[0], LANES))
        else:
            m = m_ref[qi, hh, r0:, :]
            m_new = jnp.maximum(m, m_new)
        p = jnp.exp2(s - jnp.concatenate([m_new] * (width // LANES), axis=1)).astype(BF16)
        pv = _dot(p, v_ref[k0:k0 + width, c])
        acc_ref[qi, hh, r0:, :] = pv if first else jnp.exp2(m - m_new) * acc_ref[qi, hh, r0:, :] + pv
        m_ref[qi, hh, r0:, :] = m_new

    for qi in range(seq // tq):
        q0 = qi * tq
        qs = [q_ref[q0:q0 + tq, c] for c in heads]
        for d in range(ATTN_Q_SUB):
            for hh in range(2):
                tile_update(qi, hh, d * t, qs[hh][d * t:], q0 + d * t, t, causal[d], first=(d == 0))
        for kb in range(qi):
            for hh in range(2):
                tile_update(qi, hh, 0, qs[hh], kb * tq, tq, None)
        acc_a, acc_b = acc_ref[qi, 0], acc_ref[qi, 1]
        l_a = acc_a[:, HALF:HALF + 1]
        l_b = acc_b[:, 0:1]
        out = jnp.where(lane < HALF, acc_a * (1.0 / l_a), acc_b * (1.0 / l_b))
        o_ref[q0:q0 + tq, :] = out.astype(BF16)


def _mla_attn(q, k, v, batch, seq):
    pairs = MLA_HEADS // 2
    pw = 2 * HEAD_PAD
    spec = pl.BlockSpec((seq, pw), lambda b, p: (b, p))
    return pl.pallas_call(
        _mla_attn_kernel,
        out_shape=jax.ShapeDtypeStruct((batch * seq, MLA_HEADS * MLA_V), BF16),
        grid=(batch, pairs),
        in_specs=[spec, spec, spec],
        out_specs=pl.BlockSpec((seq, LANES), lambda b, p: (b, p)),
        scratch_shapes=[pltpu.VMEM((seq // (ATTN_Q_SUB * ATTN_TILE), 2, ATTN_Q_SUB * ATTN_TILE, LANES), F32)] * 2,
        compiler_params=_params("parallel", "parallel"),
        name="mla_attn",
    )(q, k, v)


def _mem_kv_kernel(mem_ref, wk_ref, wv_ref, k_ref, v_ref):
    m = mem_ref[...].astype(BF16)
    k_ref[...] = _dot(m, wk_ref[...]).astype(BF16)
    v_ref[...] = _dot(m, wv_ref[...]).astype(BF16)


def _mem_kv(mem2, wk, wv, batch, mem_tokens):
    d = mem2.shape[1]
    spec = pl.BlockSpec((mem_tokens, d), lambda b: (b, 0))
    return pl.pallas_call(
        _mem_kv_kernel,
        out_shape=(jax.ShapeDtypeStruct(mem2.shape, BF16),) * 2,
        grid=(batch,),
        in_specs=[spec, _full(wk.shape), _full(wv.shape)],
        out_specs=(spec, spec),
        compiler_params=_params("parallel"),
        name="mem_kv",
    )(mem2, wk, wv)


def _mix_xattn_kernel(h_ref, y_ref, o_ref, km_ref, vm_ref, wout_ref, g1_ref, b1_ref,
                      wq_ref, wo_ref, pre2_ref):
    d = h_ref.shape[1]
    hd_w = d // MEM_HEADS
    scale = hd_w ** -0.5
    sub = h_ref.shape[0] // ROW_CHAINS
    chains = [slice(r * sub, (r + 1) * sub) for r in range(ROW_CHAINS)]
    mix = [_dot(y_ref[rows, :], wout_ref[0:SSD_INNER, :]) + _dot(o_ref[rows, :], wout_ref[SSD_INNER:, :])
           for rows in chains]
    h1 = [_layer_norm(DEEPNORM_ALPHA * h_ref[rows, :] + mix[r], g1_ref[...], b1_ref[...])
          for r, rows in enumerate(chains)]
    h1b = [v.astype(BF16) for v in h1]
    xa = [None] * ROW_CHAINS
    for hd in range(MEM_HEADS):
        cols = slice(hd * hd_w, (hd + 1) * hd_w)
        q = [(_dot(v, wq_ref[:, cols]) * scale).astype(BF16) for v in h1b]
        s = [_dot_nt(v, km_ref[:, cols]) for v in q]
        p = [jnp.exp(v - jnp.max(v, axis=-1, keepdims=True)) for v in s]
        p = [(v * (1.0 / jnp.sum(v, axis=-1, keepdims=True))).astype(BF16) for v in p]
        oh = [_dot(v, vm_ref[:, cols]).astype(BF16) for v in p]
        for r in range(ROW_CHAINS):
            part = _dot(oh[r], wo_ref[cols, :])
            xa[r] = part if xa[r] is None else xa[r] + part
    for r, rows in enumerate(chains):
        pre2_ref[rows, :] = DEEPNORM_ALPHA * h1[r] + xa[r]


def _mix_xattn(h, y, o, k_mem, v_mem, w_out, ln1_g, ln1_b, wq, wo, seq, mem_tokens):
    tokens, d = h.shape
    tm = TOKEN_TILE
    per_batch = seq // tm

    def row(w):
        return pl.BlockSpec((tm, w), lambda i: (i, 0))

    mem_spec = pl.BlockSpec((mem_tokens, d), lambda i: (i // per_batch, 0))
    vec = _full((1, d))
    return pl.pallas_call(
        _mix_xattn_kernel,
        out_shape=jax.ShapeDtypeStruct((tokens, d), F32),
        grid=(tokens // tm,),
        in_specs=[row(d), row(y.shape[1]), row(o.shape[1]), mem_spec, mem_spec,
                  _full(w_out.shape), vec, vec, _full(wq.shape), _full(wo.shape)],
        out_specs=row(d),
        compiler_params=_params("parallel"),
        name="mix_xattn",
    )(h, y, o, k_mem, v_mem, w_out, ln1_g, ln1_b, wq, wo)


def _mlp_kernel(pre2_ref, g2_ref, b2_ref, wup_ref, wdn_ref, g_ref, b_ref, out_ref):
    sub = pre2_ref.shape[0] // MLP_CHAINS
    chains = [slice(r * sub, (r + 1) * sub) for r in range(MLP_CHAINS)]
    h2 = [_layer_norm(pre2_ref[rows, :], g2_ref[...], b2_ref[...]) for rows in chains]
    hb = [v.astype(BF16) for v in h2]
    ff = [None] * MLP_CHAINS
    for c in range(wup_ref.shape[1] // FF_CHUNK):
        cols = slice(c * FF_CHUNK, (c + 1) * FF_CHUNK)
        for r in range(MLP_CHAINS):
            u = jnp.maximum(_dot(hb[r], wup_ref[:, cols]), 0.0)
            part = _dot((u * u).astype(BF16), wdn_ref[cols, :])
            ff[r] = part if ff[r] is None else ff[r] + part
    for r, rows in enumerate(chains):
        out_ref[rows, :] = _layer_norm(DEEPNORM_ALPHA * h2[r] + ff[r], g_ref[...], b_ref[...])


def _mlp(pre2, ln2_g, ln2_b, w_up, w_down, ln_g, ln_b):
    tokens, d = pre2.shape
    tm = TOKEN_TILE
    row = pl.BlockSpec((tm, d), lambda i: (i, 0))
    vec = _full((1, d))
    return pl.pallas_call(
        _mlp_kernel,
        out_shape=jax.ShapeDtypeStruct((tokens, d), F32),
        grid=(tokens // tm,),
        in_specs=[row, vec, vec, _full(w_up.shape), _full(w_down.shape), vec, vec],
        out_specs=row,
        compiler_params=_params("parallel"),
        name="mlp",
    )(pre2, ln2_g, ln2_b, w_up, w_down, ln_g, ln_b)


def _rot_cols(w):
    half = MLA_ROPE // 2
    return jnp.concatenate([-w[..., half:], w[..., :half]], axis=-1)


def _pack_in_proj(w_in):
    o = 0
    segs = {}
    for name, width in (("z", SSD_INNER), ("xbc", SSD_XBC), ("dt", SSD_HEADS), ("q", MLA_Q_RANK),
                        ("kv", MLA_KV_RANK), ("kr", MLA_ROPE)):
        segs[name] = w_in[:, o:o + width]
        o += width
    kr, krs = segs["kr"], _rot_cols(segs["kr"])
    dt = jnp.pad(segs["dt"], ((0, 0), (0, LANES - _KR_DT_LANE - SSD_HEADS)))
    return jnp.concatenate([segs["z"], segs["xbc"], segs["q"], segs["kv"], kr, krs, dt], axis=1).astype(BF16)


def _pack_q_up(w_q_up):
    r = w_q_up.shape[0]
    w = w_q_up.reshape(r, MLA_HEADS, MLA_QK)
    rope = w[:, :, MLA_NOPE:]
    return jnp.concatenate([w[:, :, :MLA_NOPE], rope, _rot_cols(rope)], axis=-1).reshape(
        r, MLA_HEADS * HEAD_PAD).astype(BF16)


def _pack_kv_up(w_kv_up):
    r = w_kv_up.shape[0]
    w = w_kv_up.reshape(r, MLA_HEADS, MLA_NOPE + MLA_V)
    zeros = jnp.zeros((r, MLA_HEADS, HEAD_PAD - MLA_NOPE), w.dtype)
    wk = jnp.concatenate([w[:, :, :MLA_NOPE], zeros], axis=-1)
    wv = w[:, :, MLA_NOPE:]
    zv = jnp.zeros_like(wv)
    even = (jnp.arange(MLA_HEADS) % 2 == 0)[None, :, None]
    wv = jnp.concatenate([jnp.where(even, wv, zv), jnp.where(even, zv, wv)], axis=-1)
    return jnp.concatenate([wk.reshape(r, -1), wv.reshape(r, -1)], axis=1).astype(BF16)


def kernel(x, mem, positions, ln_in_g, ln_in_b, w_in, conv_w, conv_b, dt_bias, a_log, d_skip, ssd_norm_g, q_norm_g, w_q_up, kv_norm_g, w_kv_up, w_mix_out, ln1_g, ln1_b, w_mem_q, w_mem_k, w_mem_v, w_mem_o, ln2_g, ln2_b, w_up, w_down, ln3_g, ln3_b):
    batch, seq, d = x.shape
    mem_tokens = mem.shape[1]
    tokens = batch * seq
    assert w_in.shape[0] == DEPTH == 1
    assert seq % (ATTN_TILE * ATTN_Q_SUB) == 0 and seq % TOKEN_TILE == 0 and seq % SSD_ROWS == 0

    def vec(v):
        return v.reshape(1, -1).astype(F32)

    def head_rows(v):
        return jnp.broadcast_to(v.astype(F32)[:, None], (v.size, LANES))

    x2 = x.reshape(tokens, d)

    inv_freq = jnp.power(ROPE_THETA, -jnp.arange(ROPE_FREQS, dtype=F32) / ROPE_FREQS)
    head_lane = jnp.arange(MLA_HEADS * HEAD_PAD) % HEAD_PAD
    head_odd = (jnp.arange(MLA_HEADS * HEAD_PAD) // HEAD_PAD) % 2
    vone = (head_lane == jnp.where(head_odd == 1, 0, HALF)).astype(F32)[None, :]

    h, z, xbc, lat = _in_proj(x2, vec(ln_in_g), vec(ln_in_b), _pack_in_proj(w_in[0]))
    q, k, v = _mla_prep(lat, positions, inv_freq, vone, vec(q_norm_g[0]), _pack_q_up(w_q_up[0]),
                        vec(kv_norm_g[0]), _pack_kv_up(w_kv_up[0]))

    y = _ssd(xbc, lat, z, conv_w[0].astype(F32), vec(conv_b[0]), head_rows(dt_bias[0]), head_rows(a_log[0]),
             vec(jnp.repeat(d_skip[0], SSD_HEAD_DIM)), vec(ssd_norm_g[0]), batch, seq)
    o = _mla_attn(q, k, v, batch, seq)

    k_mem, v_mem = _mem_kv(mem.reshape(batch * mem_tokens, d), w_mem_k[0].astype(BF16),
                           w_mem_v[0].astype(BF16), batch, mem_tokens)
    pre2 = _mix_xattn(h, y, o, k_mem, v_mem, w_mix_out[0].astype(BF16),
                      vec(ln1_g[0]), vec(ln1_b[0]), w_mem_q[0].astype(BF16), w_mem_o[0].astype(BF16),
                      seq, mem_tokens)
    out = _mlp(pre2, vec(ln2_g[0]), vec(ln2_b[0]), w_up[0].astype(BF16), w_down[0].astype(BF16),
               vec(ln3_g[0]), vec(ln3_b[0]))
    return out.reshape(batch, seq, d)
```

```python
import jax
import jax.numpy as jnp
from jax import lax
from jax.experimental import pallas as pl
from jax.experimental.pallas import tpu as pltpu

F32 = jnp.float32
BF16 = jnp.bfloat16

SSD_HEADS = 8
SSD_HEAD_DIM = 64
SSD_INNER = SSD_HEADS * SSD_HEAD_DIM
SSD_GROUPS = 2
SSD_STATE = 128
SSD_CONV = 4
SSD_CHUNK = 128
SSD_XBC = SSD_INNER + 2 * SSD_GROUPS * SSD_STATE
MLA_HEADS = 8
MLA_NOPE = 64
MLA_ROPE = 32
MLA_QK = MLA_NOPE + MLA_ROPE
MLA_V = 64
MLA_Q_RANK = 384
MLA_KV_RANK = 256
ROPE_THETA = 10000.0
ROPE_FREQS = MLA_ROPE // 2
MEM_HEADS = 4
LN_EPS = 1e-5
RMS_EPS = 1e-6
DEPTH = 1
DEEPNORM_ALPHA = (2.0 * DEPTH) ** 0.25

LANES = 128
SUBLANES = 8
VMEM_LIMIT_BYTES = 56 * 1024 * 1024

TOKEN_TILE = 1024
ROW_CHAINS = 4
MLA_PREP_TILE = 2048
SSD_ROWS = 2048
SSD_CONV_ROWS = 256
ATTN_TILE = 512
ATTN_Q_SUB = 2
FF_CHUNK = 1024
MLP_CHAINS = 4

HEAD_PAD = LANES
HALF = LANES // 2


def _params(*semantics):
    return pltpu.CompilerParams(dimension_semantics=semantics, vmem_limit_bytes=VMEM_LIMIT_BYTES)


def _full(shape):
    zeros = (0,) * len(shape)
    return pl.BlockSpec(shape, lambda *_: zeros, pipeline_mode=pl.Buffered(1))


def _layer_norm(x, g, b):
    mu = jnp.mean(x, axis=-1, keepdims=True)
    xc = x - mu
    var = jnp.mean(xc * xc, axis=-1, keepdims=True)
    return xc * lax.rsqrt(var + LN_EPS) * g + b


def _rms_norm(x, g):
    ms = jnp.mean(x * x, axis=-1, keepdims=True)
    return x * lax.rsqrt(ms + RMS_EPS) * g


def _silu(x):
    hx = 0.5 * x
    return hx + hx * jnp.tanh(hx)


def _dot(a, b):
    return jnp.dot(a, b, preferred_element_type=F32)


def _dot_nt(a, b):
    return lax.dot_general(a, b, (((1,), (1,)), ((), ())), preferred_element_type=F32)


_C_Z = 0
_C_XBC = _C_Z + SSD_INNER
_C_QLAT = _C_XBC + SSD_XBC
_C_KVLAT = _C_QLAT + MLA_Q_RANK
_C_KR = _C_KVLAT + MLA_KV_RANK
_C_END = _C_KR + LANES
_KR_DT_LANE = 2 * MLA_ROPE

LOG2_E = 1.4426950408889634
MLA_Q_SCALE = MLA_QK ** -0.5 * LOG2_E


def _in_proj_kernel(x_ref, g_ref, b_ref, w1_ref, h_ref, z_ref, xbc_ref, lat_ref):
    sub = x_ref.shape[0] // ROW_CHAINS
    for r in range(ROW_CHAINS):
        rows = slice(r * sub, (r + 1) * sub)
        hf = _layer_norm(x_ref[rows, :], g_ref[...], b_ref[...])
        h_ref[rows, :] = hf
        h = hf.astype(BF16)
        z_ref[rows, :] = _dot(h, w1_ref[:, _C_Z:_C_XBC]).astype(BF16)
        xbc_ref[rows, :] = _dot(h, w1_ref[:, _C_XBC:_C_QLAT]).astype(BF16)
        lat_ref[rows, :] = _dot(h, w1_ref[:, _C_QLAT:_C_END])


def _split3(x):
    x1 = x.astype(BF16)
    r1 = x - x1.astype(F32)
    x2 = r1.astype(BF16)
    x3 = (r1 - x2.astype(F32)).astype(BF16)
    return [x1, x2, x3]


def _mla_prep_kernel(lat_ref, pos_ref, invf_ref, rep_ref, fold_ref, vone_ref, qg_ref, wq_ref, kvg_ref, wkv_ref,
                     q_ref, k_ref, v_ref):
    tm = lat_ref.shape[0]
    sub = tm // ROW_CHAINS
    chains = [slice(r * sub, (r + 1) * sub) for r in range(ROW_CHAINS)]
    lane = lax.broadcasted_iota(jnp.int32, (sub, LANES), 1)
    kw = MLA_HEADS * HEAD_PAD
    c_kv, c_kr = MLA_Q_RANK, MLA_Q_RANK + MLA_KV_RANK

    ang = pos_ref[...].astype(F32) * invf_ref[...]
    thirds = jnp.concatenate(_split3(jnp.cos(ang)) + _split3(jnp.sin(ang)), axis=1)
    spread = _dot(rep_ref[...], thirds)
    own = (lax.broadcasted_iota(jnp.int32, (tm, LANES), 1) // ROPE_FREQS
           == lax.broadcasted_iota(jnp.int32, (tm, LANES), 0) % (LANES // ROPE_FREQS))
    own_only = jnp.concatenate([jnp.where(own, spread[:, b * LANES:(b + 1) * LANES], 0.0) for b in range(6)],
                               axis=1).astype(BF16)
    cos_sin = _dot(own_only, fold_ref[...])

    qn = [_rms_norm(lat_ref[rows, :c_kv], qg_ref[...]).astype(BF16) for rows in chains]
    kvn = [_rms_norm(lat_ref[rows, c_kv:c_kr], kvg_ref[...]).astype(BF16) for rows in chains]
    kr = [lat_ref[rows, c_kr:] for rows in chains]
    csq, csk = [], []
    for rows in chains:
        cos = cos_sin[rows, :LANES]
        sin = cos_sin[rows, LANES:]
        csq.append(MLA_Q_SCALE * jnp.where(lane < MLA_NOPE, 1.0, jnp.where(lane < MLA_NOPE + MLA_ROPE, cos, sin)))
        csk.append(jnp.where(lane < MLA_ROPE, cos, jnp.where(lane < 2 * MLA_ROPE, sin, 0.0)))

    for r, rows in enumerate(chains):
        q_all = _dot(qn[r], wq_ref[...])
        for hd in range(MLA_HEADS):
            cols = slice(hd * HEAD_PAD, (hd + 1) * HEAD_PAD)
            q_ref[rows, cols] = (q_all[:, cols] * csq[r]).astype(BF16)
        t = kr[r] * csk[r]
        rk = jnp.where((lane >= MLA_ROPE) & (lane < 2 * MLA_ROPE), t + pltpu.roll(t, MLA_ROPE, 1), 0.0)
        kpe = pltpu.roll(rk, MLA_ROPE, 1) + pltpu.roll(rk, 2 * MLA_ROPE, 1)
        kv_all = _dot(kvn[r], wkv_ref[...])
        for hd in range(MLA_HEADS):
            cols = slice(hd * HEAD_PAD, (hd + 1) * HEAD_PAD)
            k_ref[rows, cols] = (kv_all[:, cols] + kpe).astype(BF16)
        v_ref[rows, :] = (kv_all[:, kw:] + vone_ref[...]).astype(BF16)


def _in_proj(x2, ln_g, ln_b, w1):
    tokens, d = x2.shape
    tm = TOKEN_TILE
    lw = _C_END - _C_QLAT

    def row(w):
        return pl.BlockSpec((tm, w), lambda i: (i, 0))

    out_shape = (
        jax.ShapeDtypeStruct((tokens, d), F32),
        jax.ShapeDtypeStruct((tokens, SSD_INNER), BF16),
        jax.ShapeDtypeStruct((tokens, SSD_XBC), BF16),
        jax.ShapeDtypeStruct((tokens, lw), F32),
    )
    return pl.pallas_call(
        _in_proj_kernel,
        out_shape=out_shape,
        grid=(tokens // tm,),
        in_specs=[row(d), _full((1, d)), _full((1, d)), _full(w1.shape)],
        out_specs=(row(d), row(SSD_INNER), row(SSD_XBC), row(lw)),
        compiler_params=_params("parallel"),
        name="in_proj",
    )(x2, ln_g, ln_b, w1)


def _mla_prep(lat, positions, inv_freq, vone, q_norm_g, wq, kv_norm_g, wkv):
    tokens, lw = lat.shape
    tm = MLA_PREP_TILE
    hw = MLA_HEADS * HEAD_PAD
    per_row = LANES // ROPE_FREQS

    def row(w):
        return pl.BlockSpec((tm, w), lambda i: (i, 0))

    pos_c = jnp.repeat(positions.reshape(tokens // per_row, per_row), ROPE_FREQS, axis=1)
    invf = jnp.tile(inv_freq, per_row)[None, :]
    rep = (jnp.arange(tm)[:, None] // per_row == jnp.arange(tm // per_row)[None, :]).astype(BF16)
    src = jnp.arange(6 * LANES)[:, None]
    dst = jnp.arange(2 * LANES)[None, :]
    fold = ((src // (3 * LANES) == dst // LANES) & (src % ROPE_FREQS == dst % ROPE_FREQS)).astype(BF16)

    return pl.pallas_call(
        _mla_prep_kernel,
        out_shape=(jax.ShapeDtypeStruct((tokens, hw), BF16),) * 3,
        grid=(tokens // tm,),
        in_specs=[row(lw), pl.BlockSpec((tm // per_row, LANES), lambda i: (i, 0)), _full((1, LANES)),
                  _full(rep.shape), _full(fold.shape), _full((1, hw)), _full((1, MLA_Q_RANK)), _full(wq.shape),
                  _full((1, MLA_KV_RANK)), _full(wkv.shape)],
        out_specs=(row(hw), row(hw), row(hw)),
        compiler_params=_params("parallel"),
        name="mla_prep",
    )(lat, pos_c, invf, rep, fold, vone, q_norm_g, wq, kv_norm_g, wkv)


def _cumsum_lanes(x, tri):
    x1 = x.astype(BF16).astype(F32)
    r1 = x - x1
    x2 = r1.astype(BF16).astype(F32)
    x3 = r1 - x2
    parts = jnp.concatenate([x1, x2, x3, jnp.zeros_like(x)], axis=0).astype(BF16)
    sums = _dot(parts, tri)
    n = x.shape[0]
    return sums[0:n] + sums[n:2 * n] + sums[2 * n:3 * n]


def _ssd_kernel(xbc_ref, dt_ref, z_ref, shift_ref, tri_ref, cw_ref, cb_ref, dtb_ref, alog_ref, dskip_ref, ng_ref,
                y_ref, tail_ref, state_ref):
    rows = xbc_ref.shape[0]
    L = SSD_CHUNK
    pairs = SSD_HEADS // 2
    pairs_per_group = pairs // SSD_GROUPS

    @pl.when(pl.program_id(1) == 0)
    def _():
        tail_ref[...] = jnp.zeros(tail_ref.shape, F32)
        state_ref[...] = jnp.zeros(state_ref.shape, F32)

    a_head = -jnp.exp(alog_ref[...])
    scalars = []
    for c in range(rows // L):
        dt_in = dt_ref[c * L:(c + 1) * L, :].T[_KR_DT_LANE:_KR_DT_LANE + SSD_HEADS, :] + dtb_ref[...]
        dt_t = jnp.maximum(dt_in, 0.0) + jnp.log1p(jnp.exp(-jnp.abs(dt_in)))
        a_cs_t = _cumsum_lanes(dt_t * a_head, tri_ref[...])
        a_last = a_cs_t[:, L - 1:L]
        c_dec = jnp.exp(jnp.broadcast_to(a_last, a_cs_t.shape))
        w_end_t = dt_t * jnp.exp(a_last - a_cs_t)
        a_cs = jnp.concatenate([a_cs_t, jnp.zeros((L - SSD_HEADS, L), F32)], axis=0).T
        scalars.append((dt_t, a_cs_t, w_end_t, c_dec, a_cs))

    cr = shift_ref.shape[1]
    tail = tail_ref[...]
    row8 = lax.broadcasted_iota(jnp.int32, tail.shape, 0)
    xc_blocks = []
    for blk in range(rows // cr):
        u = xbc_ref[blk * cr:(blk + 1) * cr, :]
        u32 = u.astype(F32)
        sh = _dot(shift_ref[...], u)
        conv = cb_ref[...] + cw_ref[SSD_CONV - 1:SSD_CONV, :] * u32
        for k in range(SSD_CONV - 1):
            conv = conv + cw_ref[k:k + 1, :] * sh[k * cr:(k + 1) * cr, :]
        head_fix = jnp.zeros(tail.shape, F32)
        for k in range(SSD_CONV - 1):
            delay = SSD_CONV - 1 - k
            head_fix = head_fix + cw_ref[k:k + 1, :] * jnp.where(row8 < delay, pltpu.roll(tail, delay, 0), 0.0)
        conv = jnp.concatenate([conv[0:SUBLANES, :] + head_fix, conv[SUBLANES:, :]], axis=0)
        tail = u32[cr - SUBLANES:, :]
        xc_blocks.append(_silu(conv))
    tail_ref[...] = tail

    sq = (L, L)
    row_i = lax.broadcasted_iota(jnp.int32, sq, 0)
    col_i = lax.broadcasted_iota(jnp.int32, sq, 1)
    causal = row_i >= col_i
    left = col_i < HALF
    left_row = lax.broadcasted_iota(jnp.int32, (1, LANES), 1) < HALF

    states = [state_ref[pj] for pj in range(pairs)]
    for c in range(rows // L):
        r0 = c * L
        xc = xc_blocks[r0 // cr][r0 % cr:r0 % cr + L, :]
        xs = xc[:, 0:SSD_INNER]
        bm = xc[:, SSD_INNER:SSD_INNER + SSD_GROUPS * SSD_STATE]
        cm = xc[:, SSD_INNER + SSD_GROUPS * SSD_STATE:]
        dt_t, a_cs_t, w_end_t, c_dec, a_cs = scalars[c]

        y_pairs = []
        for g in range(SSD_GROUPS):
            b_g = bm[:, g * SSD_STATE:(g + 1) * SSD_STATE]
            c_g = cm[:, g * SSD_STATE:(g + 1) * SSD_STATE]
            b_gt = b_g.T
            cb = _dot(c_g.astype(BF16), b_gt.astype(BF16))
            for j in range(pairs_per_group):
                pj = g * pairs_per_group + j
                x_p = xs[:, pj * LANES:(pj + 1) * LANES]
                st = states[pj]
                y_p = x_p * dskip_ref[:, pj * LANES:(pj + 1) * LANES]
                new = None
                for hh in range(2):
                    hd = 2 * pj + hh
                    cs_l = jnp.broadcast_to(a_cs[:, hd:hd + 1], sq)
                    m = cb * jnp.where(causal, jnp.exp(cs_l - a_cs_t[hd:hd + 1, :]), 0.0) * dt_t[hd:hd + 1, :]
                    lhs = jnp.concatenate([m, c_g * jnp.exp(cs_l)], axis=1).astype(BF16)
                    keep = left if hh == 0 else jnp.logical_not(left)
                    x_h = jnp.where(keep, x_p, 0.0).astype(BF16)
                    rhs = jnp.concatenate([x_h, jnp.where(keep, st, 0.0).astype(BF16)], axis=0)
                    y_p = y_p + _dot(lhs, rhs)
                    part = _dot((b_gt * w_end_t[hd:hd + 1, :]).astype(BF16), x_h)
                    new = part if new is None else new + part
                dec = jnp.where(left_row, c_dec[2 * pj:2 * pj + 1, :], c_dec[2 * pj + 1:2 * pj + 2, :])
                states[pj] = st * dec + new
                y_pairs.append(y_p)

        gw = SSD_INNER // SSD_GROUPS
        zt = z_ref[r0:r0 + L, :].astype(F32)
        for g in range(SSD_GROUPS):
            y_g = jnp.concatenate(y_pairs[g * pairs_per_group:(g + 1) * pairs_per_group], axis=1)
            y_g = y_g * _silu(zt[:, g * gw:(g + 1) * gw])
            y_g = _rms_norm(y_g, ng_ref[:, g * gw:(g + 1) * gw])
            y_ref[r0:r0 + L, g * gw:(g + 1) * gw] = y_g.astype(BF16)
    for pj in range(pairs):
        state_ref[pj] = states[pj]


def _ssd(xbc, dt, z, conv_w, conv_b, dt_bias, a_log, d_skip, norm_g, batch, seq):
    rows = SSD_ROWS
    steps = seq // rows

    def tok(w):
        return pl.BlockSpec((rows, w), lambda b, c: (b * steps + c, 0))

    t_out = jnp.arange(SSD_CONV_ROWS)[:, None]
    t_in = jnp.arange(SSD_CONV_ROWS)[None, :]
    shift = jnp.concatenate([(t_out - t_in == SSD_CONV - 1 - k) for k in range(SSD_CONV - 1)], axis=0).astype(BF16)
    tri = (jnp.arange(SSD_CHUNK)[:, None] <= jnp.arange(SSD_CHUNK)[None, :]).astype(BF16)

    dt_block = dt.shape[1] // LANES - 1
    pairs = SSD_HEADS // 2
    return pl.pallas_call(
        _ssd_kernel,
        out_shape=jax.ShapeDtypeStruct((batch * seq, SSD_INNER), BF16),
        grid=(batch, steps),
        in_specs=[tok(SSD_XBC), pl.BlockSpec((rows, LANES), lambda b, c: (b * steps + c, dt_block)),
                  tok(SSD_INNER), _full(shift.shape), _full(tri.shape),
                  _full(conv_w.shape), _full(conv_b.shape), _full(dt_bias.shape), _full(a_log.shape),
                  _full(d_skip.shape), _full(norm_g.shape)],
        out_specs=tok(SSD_INNER),
        scratch_shapes=[pltpu.VMEM((SUBLANES, SSD_XBC), F32),
                        pltpu.VMEM((pairs, SSD_STATE, LANES), F32)],
        compiler_params=_params("parallel", "arbitrary"),
        name="ssd",
    )(xbc, dt, z, shift, tri, conv_w, conv_b, dt_bias, a_log, d_skip, norm_g)


def _mla_attn_kernel(q_ref, k_ref, v_ref, o_ref, m_ref, acc_ref):
    seq = q_ref.shape[0]
    t = ATTN_TILE
    tq = ATTN_Q_SUB * t
    causal = [lax.broadcasted_iota(jnp.int32, (tq - d * t, t), 0) >= lax.broadcasted_iota(jnp.int32, (tq - d * t, t), 1)
              for d in range(ATTN_Q_SUB)]
    lane = lax.broadcasted_iota(jnp.int32, (tq, LANES), 1)
    heads = tuple(slice(hh * HEAD_PAD, (hh + 1) * HEAD_PAD) for hh in range(2))

    def tile_update(qi, hh, r0, q, k0, width, mask, first=False):
        c = heads[hh]
        s = _dot_nt(q, k_ref[k0:k0 + width, c])
        if mask is not None:
            s = jnp.where(mask, s, -jnp.inf)
        m_new = jnp.max(s, axis=-1, keepdims=True)
        if first:
            m_new = jnp.broadcast_to(m_new, (s.shape[0], LANES))
        else:
            m = m_ref[qi, hh, r0:, :]
            m_new = jnp.maximum(m, m_new)
        p = jnp.exp2(s - jnp.concatenate([m_new] * (width // LANES), axis=1)).astype(BF16)
        pv = _dot(p, v_ref[k0:k0 + width, c])
        acc_ref[qi, hh, r0:, :] = pv if first else jnp.exp2(m - m_new) * acc_ref[qi, hh, r0:, :] + pv
        m_ref[qi, hh, r0:, :] = m_new

    for qi in range(seq // tq):
        q0 = qi * tq
        qs = [q_ref[q0:q0 + tq, c] for c in heads]
        for d in range(ATTN_Q_SUB):
            for hh in range(2):
                tile_update(qi, hh, d * t, qs[hh][d * t:], q0 + d * t, t, causal[d], first=(d == 0))
        for kb in range(qi):
            for hh in range(2):
                tile_update(qi, hh, 0, qs[hh], kb * tq, tq, None)
        acc_a, acc_b = acc_ref[qi, 0], acc_ref[qi, 1]
        l_a = acc_a[:, HALF:HALF + 1]
        l_b = acc_b[:, 0:1]
        out = jnp.where(lane < HALF, acc_a * (1.0 / l_a), acc_b * (1.0 / l_b))
        o_ref[q0:q0 + tq, :] = out.astype(BF16)


def _mla_attn(q, k, v, batch, seq):
    pairs = MLA_HEADS // 2
    pw = 2 * HEAD_PAD
    spec = pl.BlockSpec((seq, pw), lambda b, p: (b, p))
    return pl.pallas_call(
        _mla_attn_kernel,
        out_shape=jax.ShapeDtypeStruct((batch * seq, MLA_HEADS * MLA_V), BF16),
        grid=(batch, pairs),
        in_specs=[spec, spec, spec],
        out_specs=pl.BlockSpec((seq, LANES), lambda b, p: (b, p)),
        scratch_shapes=[pltpu.VMEM((seq // (ATTN_Q_SUB * ATTN_TILE), 2, ATTN_Q_SUB * ATTN_TILE, LANES), F32)] * 2,
        compiler_params=_params("parallel", "parallel"),
        name="mla_attn",
    )(q, k, v)


def _mem_kv_kernel(mem_ref, wk_ref, wv_ref, k_ref, v_ref):
    m = mem_ref[...].astype(BF16)
    k_ref[...] = _dot(m, wk_ref[...]).astype(BF16)
    v_ref[...] = _dot(m, wv_ref[...]).astype(BF16)


def _mem_kv(mem2, wk, wv, batch, mem_tokens):
    d = mem2.shape[1]
    spec = pl.BlockSpec((mem_tokens, d), lambda b: (b, 0))
    return pl.pallas_call(
        _mem_kv_kernel,
        out_shape=(jax.ShapeDtypeStruct(mem2.shape, BF16),) * 2,
        grid=(batch,),
        in_specs=[spec, _full(wk.shape), _full(wv.shape)],
        out_specs=(spec, spec),
        compiler_params=_params("parallel"),
        name="mem_kv",
    )(mem2, wk, wv)


def _mix_xattn_kernel(h_ref, y_ref, o_ref, km_ref, vm_ref, wout_ref, g1_ref, b1_ref,
                      wq_ref, wo_ref, pre2_ref):
    d = h_ref.shape[1]
    hd_w = d // MEM_HEADS
    scale = hd_w ** -0.5
    sub = h_ref.shape[0] // ROW_CHAINS
    chains = [slice(r * sub, (r + 1) * sub) for r in range(ROW_CHAINS)]
    mix = [_dot(y_ref[rows, :], wout_ref[0:SSD_INNER, :]) + _dot(o_ref[rows, :], wout_ref[SSD_INNER:, :])
           for rows in chains]
    h1 = [_layer_norm(DEEPNORM_ALPHA * h_ref[rows, :] + mix[r], g1_ref[...], b1_ref[...])
          for r, rows in enumerate(chains)]
    h1b = [v.astype(BF16) for v in h1]
    xa = [None] * ROW_CHAINS
    for hd in range(MEM_HEADS):
        cols = slice(hd * hd_w, (hd + 1) * hd_w)
        q = [(_dot(v, wq_ref[:, cols]) * scale).astype(BF16) for v in h1b]
        s = [_dot_nt(v, km_ref[:, cols]) for v in q]
        p = [jnp.exp(v - jnp.max(v, axis=-1, keepdims=True)) for v in s]
        p = [(v * (1.0 / jnp.sum(v, axis=-1, keepdims=True))).astype(BF16) for v in p]
        oh = [_dot(v, vm_ref[:, cols]).astype(BF16) for v in p]
        for r in range(ROW_CHAINS):
            part = _dot(oh[r], wo_ref[cols, :])
            xa[r] = part if xa[r] is None else xa[r] + part
    for r, rows in enumerate(chains):
        pre2_ref[rows, :] = DEEPNORM_ALPHA * h1[r] + xa[r]


def _mix_xattn(h, y, o, k_mem, v_mem, w_out, ln1_g, ln1_b, wq, wo, seq, mem_tokens):
    tokens, d = h.shape
    tm = TOKEN_TILE
    per_batch = seq // tm

    def row(w):
        return pl.BlockSpec((tm, w), lambda i: (i, 0))

    mem_spec = pl.BlockSpec((mem_tokens, d), lambda i: (i // per_batch, 0))
    vec = _full((1, d))
    return pl.pallas_call(
        _mix_xattn_kernel,
        out_shape=jax.ShapeDtypeStruct((tokens, d), F32),
        grid=(tokens // tm,),
        in_specs=[row(d), row(y.shape[1]), row(o.shape[1]), mem_spec, mem_spec,
                  _full(w_out.shape), vec, vec, _full(wq.shape), _full(wo.shape)],
        out_specs=row(d),
        compiler_params=_params("parallel"),
        name="mix_xattn",
    )(h, y, o, k_mem, v_mem, w_out, ln1_g, ln1_b, wq, wo)


def _mlp_kernel(pre2_ref, g2_ref, b2_ref, wup_ref, wdn_ref, g_ref, b_ref, out_ref):
    sub = pre2_ref.shape[0] // MLP_CHAINS
    chains = [slice(r * sub, (r + 1) * sub) for r in range(MLP_CHAINS)]
    h2 = [_layer_norm(pre2_ref[rows, :], g2_ref[...], b2_ref[...]) for rows in chains]
    hb = [v.astype(BF16) for v in h2]
    ff = [None] * MLP_CHAINS
    for c in range(wup_ref.shape[1] // FF_CHUNK):
        cols = slice(c * FF_CHUNK, (c + 1) * FF_CHUNK)
        for r in range(MLP_CHAINS):
            u = jnp.maximum(_dot(hb[r], wup_ref[:, cols]), 0.0)
            part = _dot((u * u).astype(BF16), wdn_ref[cols, :])
            ff[r] = part if ff[r] is None else ff[r] + part
    for r, rows in enumerate(chains):
        out_ref[rows, :] = _layer_norm(DEEPNORM_ALPHA * h2[r] + ff[r], g_ref[...], b_ref[...])


def _mlp(pre2, ln2_g, ln2_b, w_up, w_down, ln_g, ln_b):
    tokens, d = pre2.shape
    tm = TOKEN_TILE
    row = pl.BlockSpec((tm, d), lambda i: (i, 0))
    vec = _full((1, d))
    return pl.pallas_call(
        _mlp_kernel,
        out_shape=jax.ShapeDtypeStruct((tokens, d), F32),
        grid=(tokens // tm,),
        in_specs=[row, vec, vec, _full(w_up.shape), _full(w_down.shape), vec, vec],
        out_specs=row,
        compiler_params=_params("parallel"),
        name="mlp",
    )(pre2, ln2_g, ln2_b, w_up, w_down, ln_g, ln_b)


def _rot_cols(w):
    half = MLA_ROPE // 2
    return jnp.concatenate([-w[..., half:], w[..., :half]], axis=-1)


def _pack_in_proj(w_in):
    o = 0
    segs = {}
    for name, width in (("z", SSD_INNER), ("xbc", SSD_XBC), ("dt", SSD_HEADS), ("q", MLA_Q_RANK),
                        ("kv", MLA_KV_RANK), ("kr", MLA_ROPE)):
        segs[name] = w_in[:, o:o + width]
        o += width
    kr, krs = segs["kr"], _rot_cols(segs["kr"])
    dt = jnp.pad(segs["dt"], ((0, 0), (0, LANES - _KR_DT_LANE - SSD_HEADS)))
    return jnp.concatenate([segs["z"], segs["xbc"], segs["q"], segs["kv"], kr, krs, dt], axis=1).astype(BF16)


def _pack_q_up(w_q_up):
    r = w_q_up.shape[0]
    w = w_q_up.reshape(r, MLA_HEADS, MLA_QK)
    rope = w[:, :, MLA_NOPE:]
    return jnp.concatenate([w[:, :, :MLA_NOPE], rope, _rot_cols(rope)], axis=-1).reshape(
        r, MLA_HEADS * HEAD_PAD).astype(BF16)


def _pack_kv_up(w_kv_up):
    r = w_kv_up.shape[0]
    w = w_kv_up.reshape(r, MLA_HEADS, MLA_NOPE + MLA_V)
    zeros = jnp.zeros((r, MLA_HEADS, HEAD_PAD - MLA_NOPE), w.dtype)
    wk = jnp.concatenate([w[:, :, :MLA_NOPE], zeros], axis=-1)
    wv = w[:, :, MLA_NOPE:]
    zv = jnp.zeros_like(wv)
    even = (jnp.arange(MLA_HEADS) % 2 == 0)[None, :, None]
    wv = jnp.concatenate([jnp.where(even, wv, zv), jnp.where(even, zv, wv)], axis=-1)
    return jnp.concatenate([wk.reshape(r, -1), wv.reshape(r, -1)], axis=1).astype(BF16)


def kernel(x, mem, positions, ln_in_g, ln_in_b, w_in, conv_w, conv_b, dt_bias, a_log, d_skip, ssd_norm_g, q_norm_g, w_q_up, kv_norm_g, w_kv_up, w_mix_out, ln1_g, ln1_b, w_mem_q, w_mem_k, w_mem_v, w_mem_o, ln2_g, ln2_b, w_up, w_down, ln3_g, ln3_b):
    batch, seq, d = x.shape
    mem_tokens = mem.shape[1]
    tokens = batch * seq
    assert w_in.shape[0] == DEPTH == 1
    assert seq % (ATTN_TILE * ATTN_Q_SUB) == 0 and seq % TOKEN_TILE == 0 and seq % SSD_ROWS == 0

    def vec(v):
        return v.reshape(1, -1).astype(F32)

    def head_rows(v):
        return jnp.broadcast_to(v.astype(F32)[:, None], (v.size, LANES))

    x2 = x.reshape(tokens, d)

    inv_freq = jnp.power(ROPE_THETA, -jnp.arange(ROPE_FREQS, dtype=F32) / ROPE_FREQS)
    head_lane = jnp.arange(MLA_HEADS * HEAD_PAD) % HEAD_PAD
    head_odd = (jnp.arange(MLA_HEADS * HEAD_PAD) // HEAD_PAD) % 2
    vone = (head_lane == jnp.where(head_odd == 1, 0, HALF)).astype(F32)[None, :]

    h, z, xbc, lat = _in_proj(x2, vec(ln_in_g), vec(ln_in_b), _pack_in_proj(w_in[0]))
    q, k, v = _mla_prep(lat, positions, inv_freq, vone, vec(q_norm_g[0]), _pack_q_up(w_q_up[0]),
                        vec(kv_norm_g[0]), _pack_kv_up(w_kv_up[0]))

    y = _ssd(xbc, lat, z, conv_w[0].astype(F32), vec(conv_b[0]), head_rows(dt_bias[0]), head_rows(a_log[0]),
             vec(jnp.repeat(d_skip[0], SSD_HEAD_DIM)), vec(ssd_norm_g[0]), batch, seq)
    o = _mla_attn(q, k, v, batch, seq)

    k_mem, v_mem = _mem_kv(mem.reshape(batch * mem_tokens, d), w_mem_k[0].astype(BF16),
                           w_mem_v[0].astype(BF16), batch, mem_tokens)
    pre2 = _mix_xattn(h, y, o, k_mem, v_mem, w_mix_out[0].astype(BF16),
                      vec(ln1_g[0]), vec(ln1_b[0]), w_mem_q[0].astype(BF16), w_mem_o[0].astype(BF16),
                      seq, mem_tokens)
    out = _mlp(pre2, vec(ln2_g[0]), vec(ln2_b[0]), w_up[0].astype(BF16), w_down[0].astype(BF16),
               vec(ln3_g[0]), vec(ln3_b[0]))
    return out.reshape(batch, seq, d)
```

```python
import jax
import jax.numpy as jnp
from jax import lax
from jax.experimental import pallas as pl
from jax.experimental.pallas import tpu as pltpu

F32 = jnp.float32
BF16 = jnp.bfloat16

SSD_HEADS = 8
SSD_HEAD_DIM = 64
SSD_INNER = SSD_HEADS * SSD_HEAD_DIM
SSD_GROUPS = 2
SSD_STATE = 128
SSD_CONV = 4
SSD_CHUNK = 128
SSD_XBC = SSD_INNER + 2 * SSD_GROUPS * SSD_STATE
MLA_HEADS = 8
MLA_NOPE = 64
MLA_ROPE = 32
MLA_QK = MLA_NOPE + MLA_ROPE
MLA_V = 64
MLA_Q_RANK = 384
MLA_KV_RANK = 256
ROPE_THETA = 10000.0
ROPE_FREQS = MLA_ROPE // 2
MEM_HEADS = 4
LN_EPS = 1e-5
RMS_EPS = 1e-6
DEPTH = 1
DEEPNORM_ALPHA = (2.0 * DEPTH) ** 0.25

LANES = 128
SUBLANES = 8
VMEM_LIMIT_BYTES = 56 * 1024 * 1024

TOKEN_TILE = 1024
ROW_CHAINS = 4
MLA_PREP_TILE = 2048
SSD_ROWS = 2048
SSD_CONV_ROWS = 256
ATTN_TILE = 256
ATTN_Q_SUB = 4
ATTN_FULL_KEYS = 512
FF_CHUNK = 1024
MLP_CHAINS = 4

HEAD_PAD = LANES
HALF = LANES // 2


def _params(*semantics):
    return pltpu.CompilerParams(dimension_semantics=semantics, vmem_limit_bytes=VMEM_LIMIT_BYTES)


def _full(shape):
    zeros = (0,) * len(shape)
    return pl.BlockSpec(shape, lambda *_: zeros, pipeline_mode=pl.Buffered(1))


def _layer_norm(x, g, b):
    mu = jnp.mean(x, axis=-1, keepdims=True)
    xc = x - mu
    var = jnp.mean(xc * xc, axis=-1, keepdims=True)
    return xc * lax.rsqrt(var + LN_EPS) * g + b


def _rms_norm(x, g):
    ms = jnp.mean(x * x, axis=-1, keepdims=True)
    return x * lax.rsqrt(ms + RMS_EPS) * g


def _silu(x):
    hx = 0.5 * x
    return hx + hx * jnp.tanh(hx)


def _dot(a, b):
    return jnp.dot(a, b, preferred_element_type=F32)


def _dot_nt(a, b):
    return lax.dot_general(a, b, (((1,), (1,)), ((), ())), preferred_element_type=F32)


_C_Z = 0
_C_XBC = _C_Z + SSD_INNER
_C_QLAT = _C_XBC + SSD_XBC
_C_KVLAT = _C_QLAT + MLA_Q_RANK
_C_KR = _C_KVLAT + MLA_KV_RANK
_C_END = _C_KR + LANES
_KR_DT_LANE = 2 * MLA_ROPE

LOG2_E = 1.4426950408889634
MLA_Q_SCALE = MLA_QK ** -0.5 * LOG2_E


def _in_proj_kernel(x_ref, g_ref, b_ref, w1_ref, h_ref, z_ref, xbc_ref, lat_ref):
    sub = x_ref.shape[0] // ROW_CHAINS
    for r in range(ROW_CHAINS):
        rows = slice(r * sub, (r + 1) * sub)
        hf = _layer_norm(x_ref[rows, :], g_ref[...], b_ref[...])
        h_ref[rows, :] = hf
        h = hf.astype(BF16)
        z_ref[rows, :] = _dot(h, w1_ref[:, _C_Z:_C_XBC]).astype(BF16)
        xbc_ref[rows, :] = _dot(h, w1_ref[:, _C_XBC:_C_QLAT]).astype(BF16)
        lat_ref[rows, :] = _dot(h, w1_ref[:, _C_QLAT:_C_END])


def _split3(x):
    x1 = x.astype(BF16)
    r1 = x - x1.astype(F32)
    x2 = r1.astype(BF16)
    x3 = (r1 - x2.astype(F32)).astype(BF16)
    return [x1, x2, x3]


def _mla_prep_kernel(lat_ref, pos_ref, invf_ref, rep_ref, fold_ref, vone_ref, qg_ref, wq_ref, kvg_ref, wkv_ref,
                     q_ref, k_ref, v_ref):
    tm = lat_ref.shape[0]
    sub = tm // ROW_CHAINS
    chains = [slice(r * sub, (r + 1) * sub) for r in range(ROW_CHAINS)]
    lane = lax.broadcasted_iota(jnp.int32, (sub, LANES), 1)
    kw = MLA_HEADS * HEAD_PAD
    c_kv, c_kr = MLA_Q_RANK, MLA_Q_RANK + MLA_KV_RANK

    ang = pos_ref[...].astype(F32) * invf_ref[...]
    thirds = jnp.concatenate(_split3(jnp.cos(ang)) + _split3(jnp.sin(ang)), axis=1)
    spread = _dot(rep_ref[...], thirds)
    own = (lax.broadcasted_iota(jnp.int32, (tm, LANES), 1) // ROPE_FREQS
           == lax.broadcasted_iota(jnp.int32, (tm, LANES), 0) % (LANES // ROPE_FREQS))
    own_only = jnp.concatenate([jnp.where(own, spread[:, b * LANES:(b + 1) * LANES], 0.0) for b in range(6)],
                               axis=1).astype(BF16)
    cos_sin = _dot(own_only, fold_ref[...])

    qn = [_rms_norm(lat_ref[rows, :c_kv], qg_ref[...]).astype(BF16) for rows in chains]
    kvn = [_rms_norm(lat_ref[rows, c_kv:c_kr], kvg_ref[...]).astype(BF16) for rows in chains]
    kr = [lat_ref[rows, c_kr:] for rows in chains]
    csq, csk = [], []
    for rows in chains:
        cos = cos_sin[rows, :LANES]
        sin = cos_sin[rows, LANES:]
        csq.append(MLA_Q_SCALE * jnp.where(lane < MLA_NOPE, 1.0, jnp.where(lane < MLA_NOPE + MLA_ROPE, cos, sin)))
        csk.append(jnp.where(lane < MLA_ROPE, cos, jnp.where(lane < 2 * MLA_ROPE, sin, 0.0)))

    for r, rows in enumerate(chains):
        q_all = _dot(qn[r], wq_ref[...])
        for hd in range(MLA_HEADS):
            cols = slice(hd * HEAD_PAD, (hd + 1) * HEAD_PAD)
            q_ref[rows, cols] = (q_all[:, cols] * csq[r]).astype(BF16)
        t = kr[r] * csk[r]
        rk = jnp.where((lane >= MLA_ROPE) & (lane < 2 * MLA_ROPE), t + pltpu.roll(t, MLA_ROPE, 1), 0.0)
        kpe = pltpu.roll(rk, MLA_ROPE, 1) + pltpu.roll(rk, 2 * MLA_ROPE, 1)
        kv_all = _dot(kvn[r], wkv_ref[...])
        for hd in range(MLA_HEADS):
            cols = slice(hd * HEAD_PAD, (hd + 1) * HEAD_PAD)
            k_ref[rows, cols] = (kv_all[:, cols] + kpe).astype(BF16)
        v_ref[rows, :] = (kv_all[:, kw:] + vone_ref[...]).astype(BF16)


def _in_proj(x2, ln_g, ln_b, w1):
    tokens, d = x2.shape
    tm = TOKEN_TILE
    lw = _C_END - _C_QLAT

    def row(w):
        return pl.BlockSpec((tm, w), lambda i: (i, 0))

    out_shape = (
        jax.ShapeDtypeStruct((tokens, d), F32),
        jax.ShapeDtypeStruct((tokens, SSD_INNER), BF16),
        jax.ShapeDtypeStruct((tokens, SSD_XBC), BF16),
        jax.ShapeDtypeStruct((tokens, lw), F32),
    )
    return pl.pallas_call(
        _in_proj_kernel,
        out_shape=out_shape,
        grid=(tokens // tm,),
        in_specs=[row(d), _full((1, d)), _full((1, d)), _full(w1.shape)],
        out_specs=(row(d), row(SSD_INNER), row(SSD_XBC), row(lw)),
        compiler_params=_params("parallel"),
        name="in_proj",
    )(x2, ln_g, ln_b, w1)


def _mla_prep(lat, positions, inv_freq, vone, q_norm_g, wq, kv_norm_g, wkv):
    tokens, lw = lat.shape
    tm = MLA_PREP_TILE
    hw = MLA_HEADS * HEAD_PAD
    per_row = LANES // ROPE_FREQS

    def row(w):
        return pl.BlockSpec((tm, w), lambda i: (i, 0))

    pos_c = jnp.repeat(positions.reshape(tokens // per_row, per_row), ROPE_FREQS, axis=1)
    invf = jnp.tile(inv_freq, per_row)[None, :]
    rep = (jnp.arange(tm)[:, None] // per_row == jnp.arange(tm // per_row)[None, :]).astype(BF16)
    src = jnp.arange(6 * LANES)[:, None]
    dst = jnp.arange(2 * LANES)[None, :]
    fold = ((src // (3 * LANES) == dst // LANES) & (src % ROPE_FREQS == dst % ROPE_FREQS)).astype(BF16)

    return pl.pallas_call(
        _mla_prep_kernel,
        out_shape=(jax.ShapeDtypeStruct((tokens, hw), BF16),) * 3,
        grid=(tokens // tm,),
        in_specs=[row(lw), pl.BlockSpec((tm // per_row, LANES), lambda i: (i, 0)), _full((1, LANES)),
                  _full(rep.shape), _full(fold.shape), _full((1, hw)), _full((1, MLA_Q_RANK)), _full(wq.shape),
                  _full((1, MLA_KV_RANK)), _full(wkv.shape)],
        out_specs=(row(hw), row(hw), row(hw)),
        compiler_params=_params("parallel"),
        name="mla_prep",
    )(lat, pos_c, invf, rep, fold, vone, q_norm_g, wq, kv_norm_g, wkv)


def _cumsum_lanes(x, tri):
    x1 = x.astype(BF16).astype(F32)
    r1 = x - x1
    x2 = r1.astype(BF16).astype(F32)
    x3 = r1 - x2
    parts = jnp.concatenate([x1, x2, x3, jnp.zeros_like(x)], axis=0).astype(BF16)
    sums = _dot(parts, tri)
    n = x.shape[0]
    return sums[0:n] + sums[n:2 * n] + sums[2 * n:3 * n]


def _ssd_kernel(xbc_ref, dt_ref, z_ref, shift_ref, tri_ref, cw_ref, cb_ref, dtb_ref, alog_ref, dskip_ref, ng_ref,
                y_ref, tail_ref, state_ref):
    rows = xbc_ref.shape[0]
    L = SSD_CHUNK
    pairs = SSD_HEADS // 2
    pairs_per_group = pairs // SSD_GROUPS

    @pl.when(pl.program_id(1) == 0)
    def _():
        tail_ref[...] = jnp.zeros(tail_ref.shape, F32)
        state_ref[...] = jnp.zeros(state_ref.shape, F32)

    a_head = -jnp.exp(alog_ref[...])
    scalars = []
    for c in range(rows // L):
        dt_in = dt_ref[c * L:(c + 1) * L, :].T[_KR_DT_LANE:_KR_DT_LANE + SSD_HEADS, :] + dtb_ref[...]
        dt_t = jnp.maximum(dt_in, 0.0) + jnp.log1p(jnp.exp(-jnp.abs(dt_in)))
        a_cs_t = _cumsum_lanes(dt_t * a_head, tri_ref[...])
        a_last = a_cs_t[:, L - 1:L]
        c_dec = jnp.exp(jnp.broadcast_to(a_last, a_cs_t.shape))
        w_end_t = dt_t * jnp.exp(a_last - a_cs_t)
        a_cs = jnp.concatenate([a_cs_t, jnp.zeros((L - SSD_HEADS, L), F32)], axis=0).T
        scalars.append((dt_t, a_cs_t, w_end_t, c_dec, a_cs))

    cr = shift_ref.shape[1]
    tail = tail_ref[...]
    row8 = lax.broadcasted_iota(jnp.int32, tail.shape, 0)
    xc_blocks = []
    for blk in range(rows // cr):
        u = xbc_ref[blk * cr:(blk + 1) * cr, :]
        u32 = u.astype(F32)
        sh = _dot(shift_ref[...], u)
        conv = cb_ref[...] + cw_ref[SSD_CONV - 1:SSD_CONV, :] * u32
        for k in range(SSD_CONV - 1):
            conv = conv + cw_ref[k:k + 1, :] * sh[k * cr:(k + 1) * cr, :]
        head_fix = jnp.zeros(tail.shape, F32)
        for k in range(SSD_CONV - 1):
            delay = SSD_CONV - 1 - k
            head_fix = head_fix + cw_ref[k:k + 1, :] * jnp.where(row8 < delay, pltpu.roll(tail, delay, 0), 0.0)
        conv = jnp.concatenate([conv[0:SUBLANES, :] + head_fix, conv[SUBLANES:, :]], axis=0)
        tail = u32[cr - SUBLANES:, :]
        xc_blocks.append(_silu(conv))
    tail_ref[...] = tail

    sq = (L, L)
    row_i = lax.broadcasted_iota(jnp.int32, sq, 0)
    col_i = lax.broadcasted_iota(jnp.int32, sq, 1)
    causal = row_i >= col_i
    left = col_i < HALF
    left_row = lax.broadcasted_iota(jnp.int32, (1, LANES), 1) < HALF

    states = [state_ref[pj] for pj in range(pairs)]
    for c in range(rows // L):
        r0 = c * L
        xc = xc_blocks[r0 // cr][r0 % cr:r0 % cr + L, :]
        xs = xc[:, 0:SSD_INNER]
        bm = xc[:, SSD_INNER:SSD_INNER + SSD_GROUPS * SSD_STATE]
        cm = xc[:, SSD_INNER + SSD_GROUPS * SSD_STATE:]
        dt_t, a_cs_t, w_end_t, c_dec, a_cs = scalars[c]

        y_pairs = []
        for g in range(SSD_GROUPS):
            b_g = bm[:, g * SSD_STATE:(g + 1) * SSD_STATE]
            c_g = cm[:, g * SSD_STATE:(g + 1) * SSD_STATE]
            b_gt = b_g.T
            cb = _dot(c_g.astype(BF16), b_gt.astype(BF16))
            for j in range(pairs_per_group):
                pj = g * pairs_per_group + j
                x_p = xs[:, pj * LANES:(pj + 1) * LANES]
                st = states[pj]
                y_p = x_p * dskip_ref[:, pj * LANES:(pj + 1) * LANES]
                new = None
                for hh in range(2):
                    hd = 2 * pj + hh
                    cs_l = jnp.broadcast_to(a_cs[:, hd:hd + 1], sq)
                    m = cb * jnp.where(causal, jnp.exp(cs_l - a_cs_t[hd:hd + 1, :]), 0.0) * dt_t[hd:hd + 1, :]
                    lhs = jnp.concatenate([m, c_g * jnp.exp(cs_l)], axis=1).astype(BF16)
                    keep = left if hh == 0 else jnp.logical_not(left)
                    x_h = jnp.where(keep, x_p, 0.0).astype(BF16)
                    rhs = jnp.concatenate([x_h, jnp.where(keep, st, 0.0).astype(BF16)], axis=0)
                    y_p = y_p + _dot(lhs, rhs)
                    part = _dot((b_gt * w_end_t[hd:hd + 1, :]).astype(BF16), x_h)
                    new = part if new is None else new + part
                dec = jnp.where(left_row, c_dec[2 * pj:2 * pj + 1, :], c_dec[2 * pj + 1:2 * pj + 2, :])
                states[pj] = st * dec + new
                y_pairs.append(y_p)

        gw = SSD_INNER // SSD_GROUPS
        zt = z_ref[r0:r0 + L, :].astype(F32)
        for g in range(SSD_GROUPS):
            y_g = jnp.concatenate(y_pairs[g * pairs_per_group:(g + 1) * pairs_per_group], axis=1)
            y_g = y_g * _silu(zt[:, g * gw:(g + 1) * gw])
            y_g = _rms_norm(y_g, ng_ref[:, g * gw:(g + 1) * gw])
            y_ref[r0:r0 + L, g * gw:(g + 1) * gw] = y_g.astype(BF16)
    for pj in range(pairs):
        state_ref[pj] = states[pj]


def _ssd(xbc, dt, z, conv_w, conv_b, dt_bias, a_log, d_skip, norm_g, batch, seq):
    rows = SSD_ROWS
    steps = seq // rows

    def tok(w):
        return pl.BlockSpec((rows, w), lambda b, c: (b * steps + c, 0))

    t_out = jnp.arange(SSD_CONV_ROWS)[:, None]
    t_in = jnp.arange(SSD_CONV_ROWS)[None, :]
    shift = jnp.concatenate([(t_out - t_in == SSD_CONV - 1 - k) for k in range(SSD_CONV - 1)], axis=0).astype(BF16)
    tri = (jnp.arange(SSD_CHUNK)[:, None] <= jnp.arange(SSD_CHUNK)[None, :]).astype(BF16)

    dt_block = dt.shape[1] // LANES - 1
    pairs = SSD_HEADS // 2
    return pl.pallas_call(
        _ssd_kernel,
        out_shape=jax.ShapeDtypeStruct((batch * seq, SSD_INNER), BF16),
        grid=(batch, steps),
        in_specs=[tok(SSD_XBC), pl.BlockSpec((rows, LANES), lambda b, c: (b * steps + c, dt_block)),
                  tok(SSD_INNER), _full(shift.shape), _full(tri.shape),
                  _full(conv_w.shape), _full(conv_b.shape), _full(dt_bias.shape), _full(a_log.shape),
                  _full(d_skip.shape), _full(norm_g.shape)],
        out_specs=tok(SSD_INNER),
        scratch_shapes=[pltpu.VMEM((SUBLANES, SSD_XBC), F32),
                        pltpu.VMEM((pairs, SSD_STATE, LANES), F32)],
        compiler_params=_params("parallel", "arbitrary"),
        name="ssd",
    )(xbc, dt, z, shift, tri, conv_w, conv_b, dt_bias, a_log, d_skip, norm_g)


def _mla_attn_kernel(q_ref, k_ref, v_ref, o_ref, m_ref, acc_ref):
    seq = q_ref.shape[0]
    t = ATTN_TILE
    tq = ATTN_Q_SUB * t
    causal = [lax.broadcasted_iota(jnp.int32, (tq - d * t, t), 0) >= lax.broadcasted_iota(jnp.int32, (tq - d * t, t), 1)
              for d in range(ATTN_Q_SUB)]
    lane = lax.broadcasted_iota(jnp.int32, (tq, LANES), 1)
    heads = tuple(slice(hh * HEAD_PAD, (hh + 1) * HEAD_PAD) for hh in range(2))

    def tile_update(qi, hh, r0, q, k0, width, mask, first=False):
        c = heads[hh]
        s = _dot_nt(q, k_ref[k0:k0 + width, c])
        if mask is not None:
            s = jnp.where(mask, s, -jnp.inf)
        m_new = jnp.max(s, axis=-1, keepdims=True)
        if first:
            m_new = jnp.broadcast_to(m_new, (s.shape[0], LANES))
        else:
            m = m_ref[qi, hh, r0:, :]
            m_new = jnp.maximum(m, m_new)
        p = jnp.exp2(s - jnp.concatenate([m_new] * (width // LANES), axis=1)).astype(BF16)
        pv = _dot(p, v_ref[k0:k0 + width, c])
        acc_ref[qi, hh, r0:, :] = pv if first else jnp.exp2(m - m_new) * acc_ref[qi, hh, r0:, :] + pv
        m_ref[qi, hh, r0:, :] = m_new

    for qi in range(seq // tq):
        q0 = qi * tq
        qs = [q_ref[q0:q0 + tq, c] for c in heads]
        for d in range(ATTN_Q_SUB):
            for hh in range(2):
                tile_update(qi, hh, d * t, qs[hh][d * t:], q0 + d * t, t, causal[d], first=(d == 0))
        for kb in range(q0 // ATTN_FULL_KEYS):
            for hh in range(2):
                tile_update(qi, hh, 0, qs[hh], kb * ATTN_FULL_KEYS, ATTN_FULL_KEYS, None)
        acc_a, acc_b = acc_ref[qi, 0], acc_ref[qi, 1]
        l_a = acc_a[:, HALF:HALF + 1]
        l_b = acc_b[:, 0:1]
        out = jnp.where(lane < HALF, acc_a * (1.0 / l_a), acc_b * (1.0 / l_b))
        o_ref[q0:q0 + tq, :] = out.astype(BF16)


def _mla_attn(q, k, v, batch, seq):
    pairs = MLA_HEADS // 2
    pw = 2 * HEAD_PAD
    spec = pl.BlockSpec((seq, pw), lambda b, p: (b, p))
    return pl.pallas_call(
        _mla_attn_kernel,
        out_shape=jax.ShapeDtypeStruct((batch * seq, MLA_HEADS * MLA_V), BF16),
        grid=(batch, pairs),
        in_specs=[spec, spec, spec],
        out_specs=pl.BlockSpec((seq, LANES), lambda b, p: (b, p)),
        scratch_shapes=[pltpu.VMEM((seq // (ATTN_Q_SUB * ATTN_TILE), 2, ATTN_Q_SUB * ATTN_TILE, LANES), F32)] * 2,
        compiler_params=_params("parallel", "parallel"),
        name="mla_attn",
    )(q, k, v)


def _mem_kv_kernel(mem_ref, wk_ref, wv_ref, k_ref, v_ref):
    m = mem_ref[...].astype(BF16)
    k_ref[...] = _dot(m, wk_ref[...]).astype(BF16)
    v_ref[...] = _dot(m, wv_ref[...]).astype(BF16)


def _mem_kv(mem2, wk, wv, batch, mem_tokens):
    d = mem2.shape[1]
    spec = pl.BlockSpec((mem_tokens, d), lambda b: (b, 0))
    return pl.pallas_call(
        _mem_kv_kernel,
        out_shape=(jax.ShapeDtypeStruct(mem2.shape, BF16),) * 2,
        grid=(batch,),
        in_specs=[spec, _full(wk.shape), _full(wv.shape)],
        out_specs=(spec, spec),
        compiler_params=_params("parallel"),
        name="mem_kv",
    )(mem2, wk, wv)


def _mix_xattn_kernel(h_ref, y_ref, o_ref, km_ref, vm_ref, wout_ref, g1_ref, b1_ref,
                      wq_ref, wo_ref, pre2_ref):
    d = h_ref.shape[1]
    hd_w = d // MEM_HEADS
    scale = hd_w ** -0.5
    sub = h_ref.shape[0] // ROW_CHAINS
    chains = [slice(r * sub, (r + 1) * sub) for r in range(ROW_CHAINS)]
    mix = [_dot(y_ref[rows, :], wout_ref[0:SSD_INNER, :]) + _dot(o_ref[rows, :], wout_ref[SSD_INNER:, :])
           for rows in chains]
    h1 = [_layer_norm(DEEPNORM_ALPHA * h_ref[rows, :] + mix[r], g1_ref[...], b1_ref[...])
          for r, rows in enumerate(chains)]
    h1b = [v.astype(BF16) for v in h1]
    xa = [None] * ROW_CHAINS
    for hd in range(MEM_HEADS):
        cols = slice(hd * hd_w, (hd + 1) * hd_w)
        q = [(_dot(v, wq_ref[:, cols]) * scale).astype(BF16) for v in h1b]
        s = [_dot_nt(v, km_ref[:, cols]) for v in q]
        p = [jnp.exp(v - jnp.max(v, axis=-1, keepdims=True)) for v in s]
        p = [(v * (1.0 / jnp.sum(v, axis=-1, keepdims=True))).astype(BF16) for v in p]
        oh = [_dot(v, vm_ref[:, cols]).astype(BF16) for v in p]
        for r in range(ROW_CHAINS):
            part = _dot(oh[r], wo_ref[cols, :])
            xa[r] = part if xa[r] is None else xa[r] + part
    for r, rows in enumerate(chains):
        pre2_ref[rows, :] = DEEPNORM_ALPHA * h1[r] + xa[r]


def _mix_xattn(h, y, o, k_mem, v_mem, w_out, ln1_g, ln1_b, wq, wo, seq, mem_tokens):
    tokens, d = h.shape
    tm = TOKEN_TILE
    per_batch = seq // tm

    def row(w):
        return pl.BlockSpec((tm, w), lambda i: (i, 0))

    mem_spec = pl.BlockSpec((mem_tokens, d), lambda i: (i // per_batch, 0))
    vec = _full((1, d))
    return pl.pallas_call(
        _mix_xattn_kernel,
        out_shape=jax.ShapeDtypeStruct((tokens, d), F32),
        grid=(tokens // tm,),
        in_specs=[row(d), row(y.shape[1]), row(o.shape[1]), mem_spec, mem_spec,
                  _full(w_out.shape), vec, vec, _full(wq.shape), _full(wo.shape)],
        out_specs=row(d),
        compiler_params=_params("parallel"),
        name="mix_xattn",
    )(h, y, o, k_mem, v_mem, w_out, ln1_g, ln1_b, wq, wo)


def _mlp_kernel(pre2_ref, g2_ref, b2_ref, wup_ref, wdn_ref, g_ref, b_ref, out_ref):
    sub = pre2_ref.shape[0] // MLP_CHAINS
    chains = [slice(r * sub, (r + 1) * sub) for r in range(MLP_CHAINS)]
    h2 = [_layer_norm(pre2_ref[rows, :], g2_ref[...], b2_ref[...]) for rows in chains]
    hb = [v.astype(BF16) for v in h2]
    ff = [None] * MLP_CHAINS
    for c in range(wup_ref.shape[1] // FF_CHUNK):
        cols = slice(c * FF_CHUNK, (c + 1) * FF_CHUNK)
        for r in range(MLP_CHAINS):
            u = jnp.maximum(_dot(hb[r], wup_ref[:, cols]), 0.0)
            part = _dot((u * u).astype(BF16), wdn_ref[cols, :])
            ff[r] = part if ff[r] is None else ff[r] + part
    for r, rows in enumerate(chains):
        out_ref[rows, :] = _layer_norm(DEEPNORM_ALPHA * h2[r] + ff[r], g_ref[...], b_ref[...])


def _mlp(pre2, ln2_g, ln2_b, w_up, w_down, ln_g, ln_b):
    tokens, d = pre2.shape
    tm = TOKEN_TILE
    row = pl.BlockSpec((tm, d), lambda i: (i, 0))
    vec = _full((1, d))
    return pl.pallas_call(
        _mlp_kernel,
        out_shape=jax.ShapeDtypeStruct((tokens, d), F32),
        grid=(tokens // tm,),
        in_specs=[row, vec, vec, _full(w_up.shape), _full(w_down.shape), vec, vec],
        out_specs=row,
        compiler_params=_params("parallel"),
        name="mlp",
    )(pre2, ln2_g, ln2_b, w_up, w_down, ln_g, ln_b)


def _rot_cols(w):
    half = MLA_ROPE // 2
    return jnp.concatenate([-w[..., half:], w[..., :half]], axis=-1)


def _pack_in_proj(w_in):
    o = 0
    segs = {}
    for name, width in (("z", SSD_INNER), ("xbc", SSD_XBC), ("dt", SSD_HEADS), ("q", MLA_Q_RANK),
                        ("kv", MLA_KV_RANK), ("kr", MLA_ROPE)):
        segs[name] = w_in[:, o:o + width]
        o += width
    kr, krs = segs["kr"], _rot_cols(segs["kr"])
    dt = jnp.pad(segs["dt"], ((0, 0), (0, LANES - _KR_DT_LANE - SSD_HEADS)))
    return jnp.concatenate([segs["z"], segs["xbc"], segs["q"], segs["kv"], kr, krs, dt], axis=1).astype(BF16)


def _pack_q_up(w_q_up):
    r = w_q_up.shape[0]
    w = w_q_up.reshape(r, MLA_HEADS, MLA_QK)
    rope = w[:, :, MLA_NOPE:]
    return jnp.concatenate([w[:, :, :MLA_NOPE], rope, _rot_cols(rope)], axis=-1).reshape(
        r, MLA_HEADS * HEAD_PAD).astype(BF16)


def _pack_kv_up(w_kv_up):
    r = w_kv_up.shape[0]
    w = w_kv_up.reshape(r, MLA_HEADS, MLA_NOPE + MLA_V)
    zeros = jnp.zeros((r, MLA_HEADS, HEAD_PAD - MLA_NOPE), w.dtype)
    wk = jnp.concatenate([w[:, :, :MLA_NOPE], zeros], axis=-1)
    wv = w[:, :, MLA_NOPE:]
    zv = jnp.zeros_like(wv)
    even = (jnp.arange(MLA_HEADS) % 2 == 0)[None, :, None]
    wv = jnp.concatenate([jnp.where(even, wv, zv), jnp.where(even, zv, wv)], axis=-1)
    return jnp.concatenate([wk.reshape(r, -1), wv.reshape(r, -1)], axis=1).astype(BF16)


def kernel(x, mem, positions, ln_in_g, ln_in_b, w_in, conv_w, conv_b, dt_bias, a_log, d_skip, ssd_norm_g, q_norm_g, w_q_up, kv_norm_g, w_kv_up, w_mix_out, ln1_g, ln1_b, w_mem_q, w_mem_k, w_mem_v, w_mem_o, ln2_g, ln2_b, w_up, w_down, ln3_g, ln3_b):
    batch, seq, d = x.shape
    mem_tokens = mem.shape[1]
    tokens = batch * seq
    assert w_in.shape[0] == DEPTH == 1
    assert seq % (ATTN_TILE * ATTN_Q_SUB) == 0 and seq % TOKEN_TILE == 0 and seq % SSD_ROWS == 0

    def vec(v):
        return v.reshape(1, -1).astype(F32)

    def head_rows(v):
        return jnp.broadcast_to(v.astype(F32)[:, None], (v.size, LANES))

    x2 = x.reshape(tokens, d)

    inv_freq = jnp.power(ROPE_THETA, -jnp.arange(ROPE_FREQS, dtype=F32) / ROPE_FREQS)
    head_lane = jnp.arange(MLA_HEADS * HEAD_PAD) % HEAD_PAD
    head_odd = (jnp.arange(MLA_HEADS * HEAD_PAD) // HEAD_PAD) % 2
    vone = (head_lane == jnp.where(head_odd == 1, 0, HALF)).astype(F32)[None, :]

    h, z, xbc, lat = _in_proj(x2, vec(ln_in_g), vec(ln_in_b), _pack_in_proj(w_in[0]))
    q, k, v = _mla_prep(lat, positions, inv_freq, vone, vec(q_norm_g[0]), _pack_q_up(w_q_up[0]),
                        vec(kv_norm_g[0]), _pack_kv_up(w_kv_up[0]))

    y = _ssd(xbc, lat, z, conv_w[0].astype(F32), vec(conv_b[0]), head_rows(dt_bias[0]), head_rows(a_log[0]),
             vec(jnp.repeat(d_skip[0], SSD_HEAD_DIM)), vec(ssd_norm_g[0]), batch, seq)
    o = _mla_attn(q, k, v, batch, seq)

    k_mem, v_mem = _mem_kv(mem.reshape(batch * mem_tokens, d), w_mem_k[0].astype(BF16),
                           w_mem_v[0].astype(BF16), batch, mem_tokens)
    pre2 = _mix_xattn(h, y, o, k_mem, v_mem, w_mix_out[0].astype(BF16),
                      vec(ln1_g[0]), vec(ln1_b[0]), w_mem_q[0].astype(BF16), w_mem_o[0].astype(BF16),
                      seq, mem_tokens)
    out = _mlp(pre2, vec(ln2_g[0]), vec(ln2_b[0]), w_up[0].astype(BF16), w_down[0].astype(BF16),
               vec(ln3_g[0]), vec(ln3_b[0]))
    return out.reshape(batch, seq, d)
```

```python
import jax
import jax.numpy as jnp
from jax import lax
from jax.experimental import pallas as pl
from jax.experimental.pallas import tpu as pltpu

F32 = jnp.float32
BF16 = jnp.bfloat16

SSD_HEADS = 8
SSD_HEAD_DIM = 64
SSD_INNER = SSD_HEADS * SSD_HEAD_DIM
SSD_GROUPS = 2
SSD_STATE = 128
SSD_CONV = 4
SSD_CHUNK = 128
SSD_XBC = SSD_INNER + 2 * SSD_GROUPS * SSD_STATE
MLA_HEADS = 8
MLA_NOPE = 64
MLA_ROPE = 32
MLA_QK = MLA_NOPE + MLA_ROPE
MLA_V = 64
MLA_Q_RANK = 384
MLA_KV_RANK = 256
ROPE_THETA = 10000.0
ROPE_FREQS = MLA_ROPE // 2
MEM_HEADS = 4
LN_EPS = 1e-5
RMS_EPS = 1e-6
DEPTH = 1
DEEPNORM_ALPHA = (2.0 * DEPTH) ** 0.25

LANES = 128
SUBLANES = 8
VMEM_LIMIT_BYTES = 56 * 1024 * 1024

TOKEN_TILE = 1024
ROW_CHAINS = 4
MLA_PREP_TILE = 2048
SSD_ROWS = 2048
SSD_CONV_ROWS = 256
ATTN_TILE = 256
ATTN_Q_SUB = 4
ATTN_FULL_KEYS = 2048
FF_CHUNK = 1024
MLP_CHAINS = 4

HEAD_PAD = LANES
HALF = LANES // 2


def _params(*semantics):
    return pltpu.CompilerParams(dimension_semantics=semantics, vmem_limit_bytes=VMEM_LIMIT_BYTES)


def _full(shape):
    zeros = (0,) * len(shape)
    return pl.BlockSpec(shape, lambda *_: zeros, pipeline_mode=pl.Buffered(1))


def _layer_norm(x, g, b):
    mu = jnp.mean(x, axis=-1, keepdims=True)
    xc = x - mu
    var = jnp.mean(xc * xc, axis=-1, keepdims=True)
    return xc * lax.rsqrt(var + LN_EPS) * g + b


def _rms_norm(x, g):
    ms = jnp.mean(x * x, axis=-1, keepdims=True)
    return x * lax.rsqrt(ms + RMS_EPS) * g


def _silu(x):
    hx = 0.5 * x
    return hx + hx * jnp.tanh(hx)


def _dot(a, b):
    return jnp.dot(a, b, preferred_element_type=F32)


def _dot_nt(a, b):
    return lax.dot_general(a, b, (((1,), (1,)), ((), ())), preferred_element_type=F32)


_C_Z = 0
_C_XBC = _C_Z + SSD_INNER
_C_QLAT = _C_XBC + SSD_XBC
_C_KVLAT = _C_QLAT + MLA_Q_RANK
_C_KR = _C_KVLAT + MLA_KV_RANK
_C_END = _C_KR + LANES
_KR_DT_LANE = 2 * MLA_ROPE

LOG2_E = 1.4426950408889634
MLA_Q_SCALE = MLA_QK ** -0.5 * LOG2_E


def _in_proj_kernel(x_ref, g_ref, b_ref, w1_ref, h_ref, z_ref, xbc_ref, lat_ref):
    sub = x_ref.shape[0] // ROW_CHAINS
    for r in range(ROW_CHAINS):
        rows = slice(r * sub, (r + 1) * sub)
        hf = _layer_norm(x_ref[rows, :], g_ref[...], b_ref[...])
        h_ref[rows, :] = hf
        h = hf.astype(BF16)
        z_ref[rows, :] = _dot(h, w1_ref[:, _C_Z:_C_XBC]).astype(BF16)
        xbc_ref[rows, :] = _dot(h, w1_ref[:, _C_XBC:_C_QLAT]).astype(BF16)
        lat_ref[rows, :] = _dot(h, w1_ref[:, _C_QLAT:_C_END])


def _split3(x):
    x1 = x.astype(BF16)
    r1 = x - x1.astype(F32)
    x2 = r1.astype(BF16)
    x3 = (r1 - x2.astype(F32)).astype(BF16)
    return [x1, x2, x3]


def _mla_prep_kernel(lat_ref, pos_ref, invf_ref, rep_ref, fold_ref, vone_ref, qg_ref, wq_ref, kvg_ref, wkv_ref,
                     q_ref, k_ref, v_ref):
    tm = lat_ref.shape[0]
    sub = tm // ROW_CHAINS
    chains = [slice(r * sub, (r + 1) * sub) for r in range(ROW_CHAINS)]
    lane = lax.broadcasted_iota(jnp.int32, (sub, LANES), 1)
    kw = MLA_HEADS * HEAD_PAD
    c_kv, c_kr = MLA_Q_RANK, MLA_Q_RANK + MLA_KV_RANK

    ang = pos_ref[...].astype(F32) * invf_ref[...]
    thirds = jnp.concatenate(_split3(jnp.cos(ang)) + _split3(jnp.sin(ang)), axis=1)
    spread = _dot(rep_ref[...], thirds)
    own = (lax.broadcasted_iota(jnp.int32, (tm, LANES), 1) // ROPE_FREQS
           == lax.broadcasted_iota(jnp.int32, (tm, LANES), 0) % (LANES // ROPE_FREQS))
    own_only = jnp.concatenate([jnp.where(own, spread[:, b * LANES:(b + 1) * LANES], 0.0) for b in range(6)],
                               axis=1).astype(BF16)
    cos_sin = _dot(own_only, fold_ref[...])

    qn = [_rms_norm(lat_ref[rows, :c_kv], qg_ref[...]).astype(BF16) for rows in chains]
    kvn = [_rms_norm(lat_ref[rows, c_kv:c_kr], kvg_ref[...]).astype(BF16) for rows in chains]
    kr = [lat_ref[rows, c_kr:] for rows in chains]
    csq, csk = [], []
    for rows in chains:
        cos = cos_sin[rows, :LANES]
        sin = cos_sin[rows, LANES:]
        csq.append(MLA_Q_SCALE * jnp.where(lane < MLA_NOPE, 1.0, jnp.where(lane < MLA_NOPE + MLA_ROPE, cos, sin)))
        csk.append(jnp.where(lane < MLA_ROPE, cos, jnp.where(lane < 2 * MLA_ROPE, sin, 0.0)))

    for r, rows in enumerate(chains):
        q_all = _dot(qn[r], wq_ref[...])
        for hd in range(MLA_HEADS):
            cols = slice(hd * HEAD_PAD, (hd + 1) * HEAD_PAD)
            q_ref[rows, cols] = (q_all[:, cols] * csq[r]).astype(BF16)
        t = kr[r] * csk[r]
        rk = jnp.where((lane >= MLA_ROPE) & (lane < 2 * MLA_ROPE), t + pltpu.roll(t, MLA_ROPE, 1), 0.0)
        kpe = pltpu.roll(rk, MLA_ROPE, 1) + pltpu.roll(rk, 2 * MLA_ROPE, 1)
        kv_all = _dot(kvn[r], wkv_ref[...])
        for hd in range(MLA_HEADS):
            cols = slice(hd * HEAD_PAD, (hd + 1) * HEAD_PAD)
            k_ref[rows, cols] = (kv_all[:, cols] + kpe).astype(BF16)
        v_ref[rows, :] = (kv_all[:, kw:] + vone_ref[...]).astype(BF16)


def _in_proj(x2, ln_g, ln_b, w1):
    tokens, d = x2.shape
    tm = TOKEN_TILE
    lw = _C_END - _C_QLAT

    def row(w):
        return pl.BlockSpec((tm, w), lambda i: (i, 0))

    out_shape = (
        jax.ShapeDtypeStruct((tokens, d), F32),
        jax.ShapeDtypeStruct((tokens, SSD_INNER), BF16),
        jax.ShapeDtypeStruct((tokens, SSD_XBC), BF16),
        jax.ShapeDtypeStruct((tokens, lw), F32),
    )
    return pl.pallas_call(
        _in_proj_kernel,
        out_shape=out_shape,
        grid=(tokens // tm,),
        in_specs=[row(d), _full((1, d)), _full((1, d)), _full(w1.shape)],
        out_specs=(row(d), row(SSD_INNER), row(SSD_XBC), row(lw)),
        compiler_params=_params("parallel"),
        name="in_proj",
    )(x2, ln_g, ln_b, w1)


def _mla_prep(lat, positions, inv_freq, vone, q_norm_g, wq, kv_norm_g, wkv):
    tokens, lw = lat.shape
    tm = MLA_PREP_TILE
    hw = MLA_HEADS * HEAD_PAD
    per_row = LANES // ROPE_FREQS

    def row(w):
        return pl.BlockSpec((tm, w), lambda i: (i, 0))

    pos_c = jnp.repeat(positions.reshape(tokens // per_row, per_row), ROPE_FREQS, axis=1)
    invf = jnp.tile(inv_freq, per_row)[None, :]
    rep = (jnp.arange(tm)[:, None] // per_row == jnp.arange(tm // per_row)[None, :]).astype(BF16)
    src = jnp.arange(6 * LANES)[:, None]
    dst = jnp.arange(2 * LANES)[None, :]
    fold = ((src // (3 * LANES) == dst // LANES) & (src % ROPE_FREQS == dst % ROPE_FREQS)).astype(BF16)

    return pl.pallas_call(
        _mla_prep_kernel,
        out_shape=(jax.ShapeDtypeStruct((tokens, hw), BF16),) * 3,
        grid=(tokens // tm,),
        in_specs=[row(lw), pl.BlockSpec((tm // per_row, LANES), lambda i: (i, 0)), _full((1, LANES)),
                  _full(rep.shape), _full(fold.shape), _full((1, hw)), _full((1, MLA_Q_RANK)), _full(wq.shape),
                  _full((1, MLA_KV_RANK)), _full(wkv.shape)],
        out_specs=(row(hw), row(hw), row(hw)),
        compiler_params=_params("parallel"),
        name="mla_prep",
    )(lat, pos_c, invf, rep, fold, vone, q_norm_g, wq, kv_norm_g, wkv)


def _cumsum_lanes(x, tri):
    x1 = x.astype(BF16).astype(F32)
    r1 = x - x1
    x2 = r1.astype(BF16).astype(F32)
    x3 = r1 - x2
    parts = jnp.concatenate([x1, x2, x3, jnp.zeros_like(x)], axis=0).astype(BF16)
    sums = _dot(parts, tri)
    n = x.shape[0]
    return sums[0:n] + sums[n:2 * n] + sums[2 * n:3 * n]


def _ssd_kernel(xbc_ref, dt_ref, z_ref, shift_ref, tri_ref, cw_ref, cb_ref, dtb_ref, alog_ref, dskip_ref, ng_ref,
                y_ref, tail_ref, state_ref):
    rows = xbc_ref.shape[0]
    L = SSD_CHUNK
    pairs = SSD_HEADS // 2
    pairs_per_group = pairs // SSD_GROUPS

    @pl.when(pl.program_id(1) == 0)
    def _():
        tail_ref[...] = jnp.zeros(tail_ref.shape, F32)
        state_ref[...] = jnp.zeros(state_ref.shape, F32)

    a_head = -jnp.exp(alog_ref[...])
    scalars = []
    for c in range(rows // L):
        dt_in = dt_ref[c * L:(c + 1) * L, :].T[_KR_DT_LANE:_KR_DT_LANE + SSD_HEADS, :] + dtb_ref[...]
        dt_t = jnp.maximum(dt_in, 0.0) + jnp.log1p(jnp.exp(-jnp.abs(dt_in)))
        a_cs_t = _cumsum_lanes(dt_t * a_head, tri_ref[...])
        a_last = a_cs_t[:, L - 1:L]
        c_dec = jnp.exp(jnp.broadcast_to(a_last, a_cs_t.shape))
        w_end_t = dt_t * jnp.exp(a_last - a_cs_t)
        a_cs = jnp.concatenate([a_cs_t, jnp.zeros((L - SSD_HEADS, L), F32)], axis=0).T
        scalars.append((dt_t, a_cs_t, w_end_t, c_dec, a_cs))

    cr = shift_ref.shape[1]
    tail = tail_ref[...]
    row8 = lax.broadcasted_iota(jnp.int32, tail.shape, 0)
    xc_blocks = []
    for blk in range(rows // cr):
        u = xbc_ref[blk * cr:(blk + 1) * cr, :]
        u32 = u.astype(F32)
        sh = _dot(shift_ref[...], u)
        conv = cb_ref[...] + cw_ref[SSD_CONV - 1:SSD_CONV, :] * u32
        for k in range(SSD_CONV - 1):
            conv = conv + cw_ref[k:k + 1, :] * sh[k * cr:(k + 1) * cr, :]
        head_fix = jnp.zeros(tail.shape, F32)
        for k in range(SSD_CONV - 1):
            delay = SSD_CONV - 1 - k
            head_fix = head_fix + cw_ref[k:k + 1, :] * jnp.where(row8 < delay, pltpu.roll(tail, delay, 0), 0.0)
        conv = jnp.concatenate([conv[0:SUBLANES, :] + head_fix, conv[SUBLANES:, :]], axis=0)
        tail = u32[cr - SUBLANES:, :]
        xc_blocks.append(_silu(conv))
    tail_ref[...] = tail

    sq = (L, L)
    row_i = lax.broadcasted_iota(jnp.int32, sq, 0)
    col_i = lax.broadcasted_iota(jnp.int32, sq, 1)
    causal = row_i >= col_i
    left = col_i < HALF
    left_row = lax.broadcasted_iota(jnp.int32, (1, LANES), 1) < HALF

    states = [state_ref[pj] for pj in range(pairs)]
    for c in range(rows // L):
        r0 = c * L
        xc = xc_blocks[r0 // cr][r0 % cr:r0 % cr + L, :]
        xs = xc[:, 0:SSD_INNER]
        bm = xc[:, SSD_INNER:SSD_INNER + SSD_GROUPS * SSD_STATE]
        cm = xc[:, SSD_INNER + SSD_GROUPS * SSD_STATE:]
        dt_t, a_cs_t, w_end_t, c_dec, a_cs = scalars[c]

        y_pairs = []
        for g in range(SSD_GROUPS):
            b_g = bm[:, g * SSD_STATE:(g + 1) * SSD_STATE]
            c_g = cm[:, g * SSD_STATE:(g + 1) * SSD_STATE]
            b_gt = b_g.T
            cb = _dot(c_g.astype(BF16), b_gt.astype(BF16))
            for j in range(pairs_per_group):
                pj = g * pairs_per_group + j
                x_p = xs[:, pj * LANES:(pj + 1) * LANES]
                st = states[pj]
                y_p = x_p * dskip_ref[:, pj * LANES:(pj + 1) * LANES]
                new = None
                for hh in range(2):
                    hd = 2 * pj + hh
                    cs_l = jnp.broadcast_to(a_cs[:, hd:hd + 1], sq)
                    m = cb * jnp.where(causal, jnp.exp(cs_l - a_cs_t[hd:hd + 1, :]), 0.0) * dt_t[hd:hd + 1, :]
                    lhs = jnp.concatenate([m, c_g * jnp.exp(cs_l)], axis=1).astype(BF16)
                    keep = left if hh == 0 else jnp.logical_not(left)
                    x_h = jnp.where(keep, x_p, 0.0).astype(BF16)
                    rhs = jnp.concatenate([x_h, jnp.where(keep, st, 0.0).astype(BF16)], axis=0)
                    y_p = y_p + _dot(lhs, rhs)
                    part = _dot((b_gt * w_end_t[hd:hd + 1, :]).astype(BF16), x_h)
                    new = part if new is None else new + part
                dec = jnp.where(left_row, c_dec[2 * pj:2 * pj + 1, :], c_dec[2 * pj + 1:2 * pj + 2, :])
                states[pj] = st * dec + new
                y_pairs.append(y_p)

        gw = SSD_INNER // SSD_GROUPS
        zt = z_ref[r0:r0 + L, :].astype(F32)
        for g in range(SSD_GROUPS):
            y_g = jnp.concatenate(y_pairs[g * pairs_per_group:(g + 1) * pairs_per_group], axis=1)
            y_g = y_g * _silu(zt[:, g * gw:(g + 1) * gw])
            y_g = _rms_norm(y_g, ng_ref[:, g * gw:(g + 1) * gw])
            y_ref[r0:r0 + L, g * gw:(g + 1) * gw] = y_g.astype(BF16)
    for pj in range(pairs):
        state_ref[pj] = states[pj]


def _ssd(xbc, dt, z, conv_w, conv_b, dt_bias, a_log, d_skip, norm_g, batch, seq):
    rows = SSD_ROWS
    steps = seq // rows

    def tok(w):
        return pl.BlockSpec((rows, w), lambda b, c: (b * steps + c, 0))

    t_out = jnp.arange(SSD_CONV_ROWS)[:, None]
    t_in = jnp.arange(SSD_CONV_ROWS)[None, :]
    shift = jnp.concatenate([(t_out - t_in == SSD_CONV - 1 - k) for k in range(SSD_CONV - 1)], axis=0).astype(BF16)
    tri = (jnp.arange(SSD_CHUNK)[:, None] <= jnp.arange(SSD_CHUNK)[None, :]).astype(BF16)

    dt_block = dt.shape[1] // LANES - 1
    pairs = SSD_HEADS // 2
    return pl.pallas_call(
        _ssd_kernel,
        out_shape=jax.ShapeDtypeStruct((batch * seq, SSD_INNER), BF16),
        grid=(batch, steps),
        in_specs=[tok(SSD_XBC), pl.BlockSpec((rows, LANES), lambda b, c: (b * steps + c, dt_block)),
                  tok(SSD_INNER), _full(shift.shape), _full(tri.shape),
                  _full(conv_w.shape), _full(conv_b.shape), _full(dt_bias.shape), _full(a_log.shape),
                  _full(d_skip.shape), _full(norm_g.shape)],
        out_specs=tok(SSD_INNER),
        scratch_shapes=[pltpu.VMEM((SUBLANES, SSD_XBC), F32),
                        pltpu.VMEM((pairs, SSD_STATE, LANES), F32)],
        compiler_params=_params("parallel", "arbitrary"),
        name="ssd",
    )(xbc, dt, z, shift, tri, conv_w, conv_b, dt_bias, a_log, d_skip, norm_g)


def _mla_attn_kernel(q_ref, k_ref, v_ref, o_ref, m_ref, acc_ref):
    seq = q_ref.shape[0]
    t = ATTN_TILE
    tq = ATTN_Q_SUB * t
    causal = [lax.broadcasted_iota(jnp.int32, (tq - d * t, t), 0) >= lax.broadcasted_iota(jnp.int32, (tq - d * t, t), 1)
              for d in range(ATTN_Q_SUB)]
    lane = lax.broadcasted_iota(jnp.int32, (tq, LANES), 1)
    heads = tuple(slice(hh * HEAD_PAD, (hh + 1) * HEAD_PAD) for hh in range(2))

    def tile_update(qi, hh, r0, q, k0, width, mask, first=False):
        c = heads[hh]
        s = _dot_nt(q, k_ref[k0:k0 + width, c])
        if mask is not None:
            s = jnp.where(mask, s, -jnp.inf)
        m_new = jnp.max(s, axis=-1, keepdims=True)
        if first:
            m_new = jnp.broadcast_to(m_new, (s.shape[0], LANES))
        else:
            m = m_ref[qi, hh, r0:, :]
            m_new = jnp.maximum(m, m_new)
        p = jnp.exp2(s - jnp.concatenate([m_new] * (width // LANES), axis=1)).astype(BF16)
        pv = _dot(p, v_ref[k0:k0 + width, c])
        acc_ref[qi, hh, r0:, :] = pv if first else jnp.exp2(m - m_new) * acc_ref[qi, hh, r0:, :] + pv
        m_ref[qi, hh, r0:, :] = m_new

    for qi in range(seq // tq):
        q0 = qi * tq
        qs = [q_ref[q0:q0 + tq, c] for c in heads]
        for d in range(ATTN_Q_SUB):
            for hh in range(2):
                tile_update(qi, hh, d * t, qs[hh][d * t:], q0 + d * t, t, causal[d], first=(d == 0))
        k0 = 0
        while k0 < q0:
            width = min(ATTN_FULL_KEYS, q0 - k0)
            for hh in range(2):
                tile_update(qi, hh, 0, qs[hh], k0, width, None)
            k0 += width
        acc_a, acc_b = acc_ref[qi, 0], acc_ref[qi, 1]
        l_a = acc_a[:, HALF:HALF + 1]
        l_b = acc_b[:, 0:1]
        out = jnp.where(lane < HALF, acc_a * (1.0 / l_a), acc_b * (1.0 / l_b))
        o_ref[q0:q0 + tq, :] = out.astype(BF16)


def _mla_attn(q, k, v, batch, seq):
    pairs = MLA_HEADS // 2
    pw = 2 * HEAD_PAD
    spec = pl.BlockSpec((seq, pw), lambda b, p: (b, p))
    return pl.pallas_call(
        _mla_attn_kernel,
        out_shape=jax.ShapeDtypeStruct((batch * seq, MLA_HEADS * MLA_V), BF16),
        grid=(batch, pairs),
        in_specs=[spec, spec, spec],
        out_specs=pl.BlockSpec((seq, LANES), lambda b, p: (b, p)),
        scratch_shapes=[pltpu.VMEM((seq // (ATTN_Q_SUB * ATTN_TILE), 2, ATTN_Q_SUB * ATTN_TILE, LANES), F32)] * 2,
        compiler_params=_params("parallel", "parallel"),
        name="mla_attn",
    )(q, k, v)


def _mem_kv_kernel(mem_ref, wk_ref, wv_ref, k_ref, v_ref):
    m = mem_ref[...].astype(BF16)
    k_ref[...] = _dot(m, wk_ref[...]).astype(BF16)
    v_ref[...] = _dot(m, wv_ref[...]).astype(BF16)


def _mem_kv(mem2, wk, wv, batch, mem_tokens):
    d = mem2.shape[1]
    spec = pl.BlockSpec((mem_tokens, d), lambda b: (b, 0))
    return pl.pallas_call(
        _mem_kv_kernel,
        out_shape=(jax.ShapeDtypeStruct(mem2.shape, BF16),) * 2,
        grid=(batch,),
        in_specs=[spec, _full(wk.shape), _full(wv.shape)],
        out_specs=(spec, spec),
        compiler_params=_params("parallel"),
        name="mem_kv",
    )(mem2, wk, wv)


def _mix_xattn_kernel(h_ref, y_ref, o_ref, km_ref, vm_ref, wout_ref, g1_ref, b1_ref,
                      wq_ref, wo_ref, pre2_ref):
    d = h_ref.shape[1]
    hd_w = d // MEM_HEADS
    scale = hd_w ** -0.5
    sub = h_ref.shape[0] // ROW_CHAINS
    chains = [slice(r * sub, (r + 1) * sub) for r in range(ROW_CHAINS)]
    mix = [_dot(y_ref[rows, :], wout_ref[0:SSD_INNER, :]) + _dot(o_ref[rows, :], wout_ref[SSD_INNER:, :])
           for rows in chains]
    h1 = [_layer_norm(DEEPNORM_ALPHA * h_ref[rows, :] + mix[r], g1_ref[...], b1_ref[...])
          for r, rows in enumerate(chains)]
    h1b = [v.astype(BF16) for v in h1]
    xa = [None] * ROW_CHAINS
    for hd in range(MEM_HEADS):
        cols = slice(hd * hd_w, (hd + 1) * hd_w)
        q = [(_dot(v, wq_ref[:, cols]) * scale).astype(BF16) for v in h1b]
        s = [_dot_nt(v, km_ref[:, cols]) for v in q]
        p = [jnp.exp(v - jnp.max(v, axis=-1, keepdims=True)) for v in s]
        p = [(v * (1.0 / jnp.sum(v, axis=-1, keepdims=True))).astype(BF16) for v in p]
        oh = [_dot(v, vm_ref[:, cols]).astype(BF16) for v in p]
        for r in range(ROW_CHAINS):
            part = _dot(oh[r], wo_ref[cols, :])
            xa[r] = part if xa[r] is None else xa[r] + part
    for r, rows in enumerate(chains):
        pre2_ref[rows, :] = DEEPNORM_ALPHA * h1[r] + xa[r]


def _mix_xattn(h, y, o, k_mem, v_mem, w_out, ln1_g, ln1_b, wq, wo, seq, mem_tokens):
    tokens, d = h.shape
    tm = TOKEN_TILE
    per_batch = seq // tm

    def row(w):
        return pl.BlockSpec((tm, w), lambda i: (i, 0))

    mem_spec = pl.BlockSpec((mem_tokens, d), lambda i: (i // per_batch, 0))
    vec = _full((1, d))
    return pl.pallas_call(
        _mix_xattn_kernel,
        out_shape=jax.ShapeDtypeStruct((tokens, d), F32),
        grid=(tokens // tm,),
        in_specs=[row(d), row(y.shape[1]), row(o.shape[1]), mem_spec, mem_spec,
                  _full(w_out.shape), vec, vec, _full(wq.shape), _full(wo.shape)],
        out_specs=row(d),
        compiler_params=_params("parallel"),
        name="mix_xattn",
    )(h, y, o, k_mem, v_mem, w_out, ln1_g, ln1_b, wq, wo)


def _mlp_kernel(pre2_ref, g2_ref, b2_ref, wup_ref, wdn_ref, g_ref, b_ref, out_ref):
    sub = pre2_ref.shape[0] // MLP_CHAINS
    chains = [slice(r * sub, (r + 1) * sub) for r in range(MLP_CHAINS)]
    h2 = [_layer_norm(pre2_ref[rows, :], g2_ref[...], b2_ref[...]) for rows in chains]
    hb = [v.astype(BF16) for v in h2]
    ff = [None] * MLP_CHAINS
    for c in range(wup_ref.shape[1] // FF_CHUNK):
        cols = slice(c * FF_CHUNK, (c + 1) * FF_CHUNK)
        for r in range(MLP_CHAINS):
            u = jnp.maximum(_dot(hb[r], wup_ref[:, cols]), 0.0)
            part = _dot((u * u).astype(BF16), wdn_ref[cols, :])
            ff[r] = part if ff[r] is None else ff[r] + part
    for r, rows in enumerate(chains):
        out_ref[rows, :] = _layer_norm(DEEPNORM_ALPHA * h2[r] + ff[r], g_ref[...], b_ref[...])


def _mlp(pre2, ln2_g, ln2_b, w_up, w_down, ln_g, ln_b):
    tokens, d = pre2.shape
    tm = TOKEN_TILE
    row = pl.BlockSpec((tm, d), lambda i: (i, 0))
    vec = _full((1, d))
    return pl.pallas_call(
        _mlp_kernel,
        out_shape=jax.ShapeDtypeStruct((tokens, d), F32),
        grid=(tokens // tm,),
        in_specs=[row, vec, vec, _full(w_up.shape), _full(w_down.shape), vec, vec],
        out_specs=row,
        compiler_params=_params("parallel"),
        name="mlp",
    )(pre2, ln2_g, ln2_b, w_up, w_down, ln_g, ln_b)


def _rot_cols(w):
    half = MLA_ROPE // 2
    return jnp.concatenate([-w[..., half:], w[..., :half]], axis=-1)


def _pack_in_proj(w_in):
    o = 0
    segs = {}
    for name, width in (("z", SSD_INNER), ("xbc", SSD_XBC), ("dt", SSD_HEADS), ("q", MLA_Q_RANK),
                        ("kv", MLA_KV_RANK), ("kr", MLA_ROPE)):
        segs[name] = w_in[:, o:o + width]
        o += width
    kr, krs = segs["kr"], _rot_cols(segs["kr"])
    dt = jnp.pad(segs["dt"], ((0, 0), (0, LANES - _KR_DT_LANE - SSD_HEADS)))
    return jnp.concatenate([segs["z"], segs["xbc"], segs["q"], segs["kv"], kr, krs, dt], axis=1).astype(BF16)


def _pack_q_up(w_q_up):
    r = w_q_up.shape[0]
    w = w_q_up.reshape(r, MLA_HEADS, MLA_QK)
    rope = w[:, :, MLA_NOPE:]
    return jnp.concatenate([w[:, :, :MLA_NOPE], rope, _rot_cols(rope)], axis=-1).reshape(
        r, MLA_HEADS * HEAD_PAD).astype(BF16)


def _pack_kv_up(w_kv_up):
    r = w_kv_up.shape[0]
    w = w_kv_up.reshape(r, MLA_HEADS, MLA_NOPE + MLA_V)
    zeros = jnp.zeros((r, MLA_HEADS, HEAD_PAD - MLA_NOPE), w.dtype)
    wk = jnp.concatenate([w[:, :, :MLA_NOPE], zeros], axis=-1)
    wv = w[:, :, MLA_NOPE:]
    zv = jnp.zeros_like(wv)
    even = (jnp.arange(MLA_HEADS) % 2 == 0)[None, :, None]
    wv = jnp.concatenate([jnp.where(even, wv, zv), jnp.where(even, zv, wv)], axis=-1)
    return jnp.concatenate([wk.reshape(r, -1), wv.reshape(r, -1)], axis=1).astype(BF16)


def kernel(x, mem, positions, ln_in_g, ln_in_b, w_in, conv_w, conv_b, dt_bias, a_log, d_skip, ssd_norm_g, q_norm_g, w_q_up, kv_norm_g, w_kv_up, w_mix_out, ln1_g, ln1_b, w_mem_q, w_mem_k, w_mem_v, w_mem_o, ln2_g, ln2_b, w_up, w_down, ln3_g, ln3_b):
    batch, seq, d = x.shape
    mem_tokens = mem.shape[1]
    tokens = batch * seq
    assert w_in.shape[0] == DEPTH == 1
    assert seq % (ATTN_TILE * ATTN_Q_SUB) == 0 and seq % TOKEN_TILE == 0 and seq % SSD_ROWS == 0

    def vec(v):
        return v.reshape(1, -1).astype(F32)

    def head_rows(v):
        return jnp.broadcast_to(v.astype(F32)[:, None], (v.size, LANES))

    x2 = x.reshape(tokens, d)

    inv_freq = jnp.power(ROPE_THETA, -jnp.arange(ROPE_FREQS, dtype=F32) / ROPE_FREQS)
    head_lane = jnp.arange(MLA_HEADS * HEAD_PAD) % HEAD_PAD
    head_odd = (jnp.arange(MLA_HEADS * HEAD_PAD) // HEAD_PAD) % 2
    vone = (head_lane == jnp.where(head_odd == 1, 0, HALF)).astype(F32)[None, :]

    h, z, xbc, lat = _in_proj(x2, vec(ln_in_g), vec(ln_in_b), _pack_in_proj(w_in[0]))
    q, k, v = _mla_prep(lat, positions, inv_freq, vone, vec(q_norm_g[0]), _pack_q_up(w_q_up[0]),
                        vec(kv_norm_g[0]), _pack_kv_up(w_kv_up[0]))

    y = _ssd(xbc, lat, z, conv_w[0].astype(F32), vec(conv_b[0]), head_rows(dt_bias[0]), head_rows(a_log[0]),
             vec(jnp.repeat(d_skip[0], SSD_HEAD_DIM)), vec(ssd_norm_g[0]), batch, seq)
    o = _mla_attn(q, k, v, batch, seq)

    k_mem, v_mem = _mem_kv(mem.reshape(batch * mem_tokens, d), w_mem_k[0].astype(BF16),
                           w_mem_v[0].astype(BF16), batch, mem_tokens)
    pre2 = _mix_xattn(h, y, o, k_mem, v_mem, w_mix_out[0].astype(BF16),
                      vec(ln1_g[0]), vec(ln1_b[0]), w_mem_q[0].astype(BF16), w_mem_o[0].astype(BF16),
                      seq, mem_tokens)
    out = _mlp(pre2, vec(ln2_g[0]), vec(ln2_b[0]), w_up[0].astype(BF16), w_down[0].astype(BF16),
               vec(ln3_g[0]), vec(ln3_b[0]))
    return out.reshape(batch, seq, d)
```
